```python
import math
import jax, jax.numpy as jnp
from jax import lax
import numpy as np

D_MODEL = 1024
BATCH = 8
SEQ = 4096
DEPTH = 1

HEAD_DIM = 64
RWKV_HEADS = 8
ATTN_HEADS = 8
RWKV_WIDTH = RWKV_HEADS * HEAD_DIM
ATTN_WIDTH = ATTN_HEADS * HEAD_DIM
D_MIX = RWKV_WIDTH + ATTN_WIDTH
DECAY_RANK = 64
ICLR_RANK = 64
MOBA_BLOCK = 256
MOBA_TOPK = 3
Q_CHUNK = 32
RMS_EPS = 1e-6
GN_EPS = 64e-5
NEG = -1e30

N_RWKV_COLS = 3 * RWKV_WIDTH + DECAY_RANK + ICLR_RANK
N_ATTN_COLS = 3 * ATTN_WIDTH
D_IN_PROJ = N_RWKV_COLS + N_ATTN_COLS + D_MIX
RWKV_SPLITS = (RWKV_WIDTH, 2 * RWKV_WIDTH, 3 * RWKV_WIDTH, 3 * RWKV_WIDTH + DECAY_RANK)

kernel_name = "hymba_rwkv7_moba_alibi_sandwich"


def rms_norm(x, g):
    xf = x.astype(jnp.float32)
    y = xf * lax.rsqrt(jnp.mean(xf * xf, axis=-1, keepdims=True) + RMS_EPS)
    return (y * g.astype(jnp.float32)).astype(x.dtype)


def _rwkv7_step(S, inp):
    r_t, w_t, k_t, v_t, a_t, b_t = inp
    Sa = jnp.einsum('bhij,bhj->bhi', S, a_t)
    S = S * w_t[:, :, None, :] + Sa[..., None] * b_t[:, :, None, :] + v_t[..., None] * k_t[:, :, None, :]
    y = jnp.einsum('bhij,bhj->bhi', S, r_t)
    return S, y


def rwkv7_time_mix(p, mu, w0, w2, a0, a2, k_k, k_a, r_k, lnx_w, lnx_b):
    B, T, _ = p.shape
    H, N = RWKV_HEADS, HEAD_DIM
    f32 = jnp.float32
    p_prev = jnp.pad(p, ((0, 0), (1, 0), (0, 0)))[:, :T]
    p = p + (p_prev - p) * mu
    r, k, v, wd, ad = jnp.split(p, RWKV_SPLITS, axis=-1)
    w = -jax.nn.softplus(-(w0 + jnp.tanh(wd) @ w2)) - 0.5
    decay = jnp.exp(-jnp.exp(w.astype(f32)))
    a = jax.nn.sigmoid(a0 + ad @ a2)
    kk = (k * k_k).astype(f32).reshape(B, T, H, N)
    kk = kk / jnp.maximum(jnp.linalg.norm(kk, axis=-1, keepdims=True), 1e-12)
    k = k * (1 + (a - 1) * k_a)
    heads = lambda z: z.astype(f32).reshape(B, T, H, N)
    r_h, k_h, v_h, w_h, a_h = heads(r), heads(k), heads(v), heads(decay), heads(a)
    vec_a = -kk
    vec_b = kk * a_h
    tm = lambda z: jnp.moveaxis(z, 1, 0)
    S0 = jnp.zeros((B, H, N, N), f32)
    _, y = lax.scan(_rwkv7_step, S0, (tm(r_h), tm(w_h), tm(k_h), tm(v_h), tm(vec_a), tm(vec_b)))
    y = jnp.moveaxis(y, 0, 1)
    mean = jnp.mean(y, axis=-1, keepdims=True)
    var = jnp.mean(jnp.square(y - mean), axis=-1, keepdims=True)
    y = (y - mean) * lax.rsqrt(var + GN_EPS)
    y = y.reshape(B, T, H * N) * lnx_w.astype(f32) + lnx_b.astype(f32)
    bonus = jnp.sum(r_h * k_h * r_k.astype(f32), axis=-1, keepdims=True) * v_h
    y = y + bonus.reshape(B, T, H * N)
    return y.astype(p.dtype)


def moba_attention(q, k, v):
    B, T, _ = q.shape
    H, Dh, BS = ATTN_HEADS, HEAD_DIM, MOBA_BLOCK
    f32 = jnp.float32
    nb = -(-T // BS)
    tp = nb * BS
    topk = min(MOBA_TOPK, nb)
    C = Q_CHUNK
    nc = T // C
    q = q.reshape(B, T, H, Dh).transpose(0, 2, 1, 3) * (Dh ** -0.5)
    to_heads = lambda z: jnp.pad(z.reshape(B, T, H, Dh).transpose(0, 2, 1, 3),
                                 ((0, 0), (0, 0), (0, tp - T), (0, 0)))
    kp, vp = to_heads(k), to_heads(v)
    kb = kp.reshape(B, H, nb, BS, Dh)
    vb = vp.reshape(B, H, nb, BS, Dh)
    k_mean = jnp.mean(kb.astype(f32), axis=3)
    t_pos = jnp.arange(T)
    q_blk = t_pos // BS
    gate = jnp.einsum('bhtd,bhnd->bhtn', q.astype(f32), k_mean)
    past = jnp.arange(nb)[None, :] < q_blk[:, None]
    gate = jnp.where(past, gate, NEG)
    _, sel = lax.top_k(gate, topk)
    slopes = 2.0 ** (-8.0 * jnp.arange(1, H + 1, dtype=f32) / H)
    q_c = q.reshape(B, H, nc, C, Dh).transpose(2, 0, 1, 3, 4)
    sel_c = sel.reshape(B, H, nc, C, topk).transpose(2, 0, 1, 3, 4)
    b_ix = jnp.arange(B)[:, None, None, None]
    h_ix = jnp.arange(H)[None, :, None, None]

    def chunk(args):
        qc, sc, ci = args
        t = ci * C + jnp.arange(C)
        ob = (ci * C) // BS
        k_sel = kb[b_ix, h_ix, sc]
        v_sel = vb[b_ix, h_ix, sc]
        s_sel = sc[..., None] * BS + jnp.arange(BS)
        lg_sel = jnp.einsum('bhcd,bhcksd->bhcks', qc, k_sel).astype(f32)
        lg_sel = lg_sel - slopes[None, :, None, None, None] * (t[:, None, None] - s_sel).astype(f32)
        valid = jnp.arange(topk)[None, :] < (t // BS)[:, None]
        lg_sel = jnp.where(valid[:, :, None], lg_sel, NEG)
        k_own = lax.dynamic_slice_in_dim(kp, ob * BS, BS, axis=2)
        v_own = lax.dynamic_slice_in_dim(vp, ob * BS, BS, axis=2)
        s_own = ob * BS + jnp.arange(BS)
        lg_own = jnp.einsum('bhcd,bhsd->bhcs', qc, k_own).astype(f32)
        lg_own = lg_own - slopes[:, None, None] * (t[:, None] - s_own[None, :]).astype(f32)
        lg_own = jnp.where(s_own[None, :] <= t[:, None], lg_own, NEG)
        logits = jnp.concatenate([lg_sel.reshape(B, H, C, topk * BS), lg_own], axis=-1)
        prob = jax.nn.softmax(logits, axis=-1)
        p_sel = prob[..., :topk * BS].reshape(B, H, C, topk, BS).astype(vb.dtype)
        p_own = prob[..., topk * BS:].astype(vb.dtype)
        return (jnp.einsum('bhcks,bhcksd->bhcd', p_sel, v_sel)
                + jnp.einsum('bhcs,bhsd->bhcd', p_own, v_own))

    o = lax.map(chunk, (q_c, sel_c, jnp.arange(nc)))
    return o.transpose(1, 0, 3, 2, 4).reshape(B, T, H * Dh)


def setup_inputs(seed: int = 0) -> dict:
    key = jax.random.key(seed)
    ks = jax.random.split(key, 16)
    L, D = DEPTH, D_MODEL
    nrm = jax.random.normal
    return {
        "x": nrm(ks[0], (BATCH, SEQ, D), jnp.float32),
        "g_pre": 1.0 + 0.05 * nrm(ks[1], (L, D), jnp.float32),
        "w_in": nrm(ks[2], (L, D, D_IN_PROJ), jnp.float32) * D ** -0.5,
        "tshift_mu": jax.random.uniform(ks[3], (L, N_RWKV_COLS), jnp.float32),
        "w0": jax.random.uniform(ks[4], (L, RWKV_WIDTH), jnp.float32, -6.0, 1.0),
        "w2": nrm(ks[5], (L, DECAY_RANK, RWKV_WIDTH), jnp.float32) * 0.5 * DECAY_RANK ** -0.5,
        "a0": 0.1 * nrm(ks[6], (L, RWKV_WIDTH), jnp.float32),
        "a2": nrm(ks[7], (L, ICLR_RANK, RWKV_WIDTH), jnp.float32) * 0.5 * ICLR_RANK ** -0.5,
        "k_k": 0.85 + 0.05 * nrm(ks[8], (L, RWKV_WIDTH), jnp.float32),
        "k_a": 1.0 + 0.05 * nrm(ks[9], (L, RWKV_WIDTH), jnp.float32),
        "r_k": 0.1 * nrm(ks[10], (L, RWKV_HEADS, HEAD_DIM), jnp.float32),
        "lnx_w": 1.0 + 0.05 * nrm(ks[11], (L, RWKV_WIDTH), jnp.float32),
        "lnx_b": 0.02 * nrm(ks[12], (L, RWKV_WIDTH), jnp.float32),
        "w_out": nrm(ks[13], (L, D_MIX, D), jnp.float32) * D_MIX ** -0.5,
        "g_post": 1.0 + 0.05 * nrm(ks[14], (L, D), jnp.float32),
    }


def reference(x, g_pre, w_in, tshift_mu, w0, w2, a0, a2, k_k, k_a, r_k, lnx_w, lnx_b, w_out, g_post):
    for l in range(DEPTH):
        h = rms_norm(x, g_pre[l])
        proj = h @ w_in[l]
        p_rwkv = proj[..., :N_RWKV_COLS]
        p_attn = proj[..., N_RWKV_COLS:N_RWKV_COLS + N_ATTN_COLS]
        gates = proj[..., N_RWKV_COLS + N_ATTN_COLS:]
        y_r = rwkv7_time_mix(p_rwkv, tshift_mu[l], w0[l], w2[l], a0[l], a2[l],
                             k_k[l], k_a[l], r_k[l], lnx_w[l], lnx_b[l])
        q, k, v = jnp.split(p_attn, 3, axis=-1)
        y_a = moba_attention(q, k, v)
        y = jnp.concatenate([y_r, y_a], axis=-1) * jax.nn.silu(gates)
        o = y @ w_out[l]
        x = x + rms_norm(o, g_post[l])
    return x
```

```python
import functools

import jax
import jax.numpy as jnp
from jax import lax
from jax.experimental import pallas as pl
from jax.experimental.pallas import tpu as pltpu

F32 = jnp.float32
BF16 = jnp.bfloat16

HEAD_DIM = 64
RWKV_HEADS = 8
ATTN_HEADS = 8
RWKV_WIDTH = RWKV_HEADS * HEAD_DIM
ATTN_WIDTH = ATTN_HEADS * HEAD_DIM
D_MIX = RWKV_WIDTH + ATTN_WIDTH
DECAY_RANK = 64
ICLR_RANK = 64
MOBA_BLOCK = 256
MOBA_TOPK = 3
RMS_EPS = 1e-6
GN_EPS = 64e-5
NEG = -1e30
N_RWKV_COLS = 3 * RWKV_WIDTH + DECAY_RANK + ICLR_RANK
N_ATTN_COLS = 3 * ATTN_WIDTH

LANES = 128
MXU_DIM = 256
CHUNK = 64
GROUP_HEADS = MXU_DIM // HEAD_DIM
GROUP_W = GROUP_HEADS * HEAD_DIM
TOK_TILE = 256
VMEM_LIMIT = 48 * 1024 * 1024

_NT = (((1,), (1,)), ((), ()))
_TN = (((0,), (0,)), ((), ()))


def _dot(a, b, dims=None, precision=None):
    if dims is None:
        return jnp.dot(a, b, preferred_element_type=F32, precision=precision)
    return lax.dot_general(a, b, dims, preferred_element_type=F32, precision=precision)


def _inproj_body(x_ref, g_ref, wr_ref, wa_ref, wg_ref, pr_ref, q_ref, k_ref, vt_ref, km_ref, gs_ref):
    x = x_ref[...]
    ms = jnp.mean(x * x, axis=-1, keepdims=True)
    h = (x * lax.rsqrt(ms + RMS_EPS) * g_ref[...]).astype(BF16)
    pr_ref[...] = _dot(h, wr_ref[...])
    qkv = _dot(h, wa_ref[...])
    q_ref[...] = qkv[:, :ATTN_WIDTH] * (HEAD_DIM ** -0.5)
    kf = qkv[:, ATTN_WIDTH:2 * ATTN_WIDTH]
    k_ref[...] = kf.astype(BF16)
    km_ref[0] = jnp.mean(kf, axis=0, keepdims=True)
    vt_ref[0] = qkv[:, 2 * ATTN_WIDTH:].T.astype(BF16)
    g = _dot(h, wg_ref[...])
    gs_ref[...] = (g * jax.nn.sigmoid(g)).astype(BF16)


def _in_proj(x2, g_pre, w_r, w_a, w_g, batch, seq):
    m, d = x2.shape
    nblk = m // TOK_TILE
    per_b = seq // TOK_TILE
    const = lambda i: (0, 0)
    return pl.pallas_call(
        _inproj_body,
        grid=(nblk,),
        in_specs=[
            pl.BlockSpec((TOK_TILE, d), lambda i: (i, 0)),
            pl.BlockSpec((1, d), const),
            pl.BlockSpec(w_r.shape, const),
            pl.BlockSpec(w_a.shape, const),
            pl.BlockSpec(w_g.shape, const),
        ],
        out_specs=[
            pl.BlockSpec((TOK_TILE, N_RWKV_COLS), lambda i: (i, 0)),
            pl.BlockSpec((TOK_TILE, ATTN_WIDTH), lambda i: (i, 0)),
            pl.BlockSpec((TOK_TILE, ATTN_WIDTH), lambda i: (i, 0)),
            pl.BlockSpec((1, ATTN_WIDTH, TOK_TILE), lambda i: (i // per_b, 0, i % per_b)),
            pl.BlockSpec((1, 1, ATTN_WIDTH), lambda i: (i, 0, 0)),
            pl.BlockSpec((TOK_TILE, D_MIX), lambda i: (i, 0)),
        ],
        out_shape=[
            jax.ShapeDtypeStruct((m, N_RWKV_COLS), F32),
            jax.ShapeDtypeStruct((m, ATTN_WIDTH), F32),
            jax.ShapeDtypeStruct((m, ATTN_WIDTH), BF16),
            jax.ShapeDtypeStruct((batch, ATTN_WIDTH, seq), BF16),
            jax.ShapeDtypeStruct((nblk, 1, ATTN_WIDTH), F32),
            jax.ShapeDtypeStruct((m, D_MIX), BF16),
        ],
        compiler_params=pltpu.CompilerParams(
            dimension_semantics=("arbitrary",), vmem_limit_bytes=VMEM_LIMIT),
        name="in_proj",
    )(x2, g_pre, w_r, w_a, w_g)


def _split_dot(x, ones_b):
    hi = x.astype(BF16)
    lo = (x - hi.astype(F32)).astype(BF16)
    return _dot(hi, ones_b) + _dot(lo, ones_b)


def _head_sum(x, ones_b):
    parts = [_split_dot(x[:, g * GROUP_W:(g + 1) * GROUP_W], ones_b)
             for g in range(x.shape[1] // GROUP_W)]
    return jnp.concatenate(parts, axis=1)


def _rwkv_body(pr_ref, gs_ref, mu_ref, w0_ref, w2a_ref, a0_ref, kk_ref, ka_ref, rk_ref, lw_ref, lb_ref,
               o_ref, s_ref, prev_ref, at_ref, bt_ref, kt_ref, rt_ref, v_ref, gl_ref, y_ref):
    W = RWKV_WIDTH
    n_groups = W // GROUP_W
    n_chunks = TOK_TILE // CHUNK

    @pl.when(pl.program_id(1) == 0)
    def _():
        s_ref[...] = jnp.zeros_like(s_ref)
        prev_ref[...] = jnp.zeros_like(prev_ref)

    rr = lax.broadcasted_iota(jnp.int32, (GROUP_W, GROUP_W), 0)
    cc = lax.broadcasted_iota(jnp.int32, (GROUP_W, GROUP_W), 1)
    same_head = (rr // HEAD_DIM) == (cc // HEAD_DIM)
    ones_b = same_head.astype(BF16)
    eye_g = rr == cc
    tril_bd = (same_head & (cc <= rr)).astype(F32)
    wr = lax.broadcasted_iota(jnp.int32, (CHUNK, GROUP_W), 0)
    wc = lax.broadcasted_iota(jnp.int32, (CHUNK, GROUP_W), 1) % CHUNK
    strict_w = wc < wr
    incl_w = wc <= wr
    eye_w = (wc == wr).astype(F32)
    lane_g = lax.broadcasted_iota(jnp.int32, (1, GROUP_W), 1) // HEAD_DIM
    head_masks = [lane_g == h for h in range(GROUP_HEADS)]

    def mstack(x):
        return jnp.concatenate([jnp.where(hm, x, 0.0) for hm in head_masks], axis=0).astype(BF16)

    p = pr_ref[...]
    rolled = pltpu.roll(p, 1, axis=0)
    row0 = lax.broadcasted_iota(jnp.int32, (TOK_TILE, 1), 0) == 0
    p_prev = jnp.where(row0, prev_ref[...], rolled)
    prev_ref[...] = p[TOK_TILE - 1:TOK_TILE, :]
    p = p + (p_prev - p) * mu_ref[...]
    r = p[:, 0:W]
    k = p[:, W:2 * W]
    v = p[:, 2 * W:3 * W]
    z = p[:, 3 * W:3 * W + DECAY_RANK + ICLR_RANK]
    lane_z = lax.broadcasted_iota(jnp.int32, (1, DECAY_RANK + ICLR_RANK), 1)
    z = jnp.where(lane_z < DECAY_RANK, jnp.tanh(z), z)
    lin = _dot(z, w2a_ref[...], precision=lax.Precision.HIGHEST)
    u = -(w0_ref[...] + lin[:, :W])
    softplus = jnp.maximum(u, 0.0) + jnp.log1p(jnp.exp(-jnp.abs(u)))
    w = -softplus - 0.5
    ld = -jnp.exp(w)
    a = jax.nn.sigmoid(a0_ref[...] + lin[:, W:])
    kkr = k * kk_ref[...]
    nrm = jnp.sqrt(_head_sum(kkr * kkr, ones_b))
    kk = kkr / jnp.maximum(nrm, 1e-12)
    k2 = k * (1.0 + (a - 1.0) * ka_ref[...])
    cs = _dot(tril_bd, ld, precision=lax.Precision.HIGHEST)
    g_inv = jnp.exp(-cs)
    at_ref[...] = (-kk) * jnp.exp(cs - ld)
    bt_ref[...] = kk * a * g_inv
    kt_ref[...] = k2 * g_inv
    rt_ref[...] = r * jnp.exp(cs)
    v_ref[...] = v
    gl_ref[...] = jnp.exp(cs)
    bonus = _head_sum(r * k2 * rk_ref[...], ones_b) * v

    def chunk_step(c, carry):
        r0 = pl.multiple_of(c * CHUNK, CHUNK)
        rows = pl.ds(r0, CHUNK)
        for g in range(n_groups):
            cols = slice(g * GROUP_W, (g + 1) * GROUP_W)
            at = at_ref[rows, cols]
            bt = bt_ref[rows, cols]
            kt = kt_ref[rows, cols]
            rt = rt_ref[rows, cols]
            vv = v_ref[rows, cols]
            gl = gl_ref[pl.ds(r0 + CHUNK - 1, 1), cols]
            lhs = jnp.concatenate([at, rt], axis=0).astype(BF16)
            rhs = jnp.concatenate([mstack(bt), mstack(kt)], axis=0)
            x = _dot(lhs, rhs, _NT)
            a_ab = jnp.where(strict_w, x[:CHUNK, :GROUP_W], 0.0)
            a_ak = jnp.where(strict_w, x[:CHUNK, GROUP_W:], 0.0)
            m_rb = jnp.where(incl_w, x[CHUNK:, :GROUP_W], 0.0)
            m_rk = jnp.where(incl_w, x[CHUNK:, GROUP_W:], 0.0)
            pw = jnp.where(same_head, jnp.concatenate([a_ab] * GROUP_HEADS, axis=0), 0.0)
            tw = a_ab + eye_w
            n_sq = CHUNK.bit_length() - 2
            for it in range(n_sq):
                pb = pw.astype(BF16)
                pw = _dot(pb, pb)
                tw = tw + _dot(tw.astype(BF16), pw.astype(BF16))
            twb = tw.astype(BF16)
            wt = _dot(twb, mstack(at))
            u0 = _dot(twb, mstack(_dot(a_ak.astype(BF16), mstack(vv))))
            m_rb_b = m_rb.astype(BF16)
            q = rt + _dot(m_rb_b, mstack(wt))
            y0 = _dot(m_rb_b, mstack(u0)) + _dot(m_rk.astype(BF16), mstack(vv))
            bh = (bt * gl).astype(BF16)
            kh = (kt * gl).astype(BF16)
            gm = jnp.where(eye_g, gl, 0.0) + jnp.where(same_head, _dot(bh, wt.astype(BF16), _TN), 0.0)
            hm = jnp.where(
                same_head,
                _dot(jnp.concatenate([bh, kh], axis=0),
                     jnp.concatenate([u0, vv], axis=0).astype(BF16), _TN),
                0.0)
            s_b = s_ref[g].astype(BF16)
            y_ref[rows, cols] = _dot(q.astype(BF16), s_b) + y0
            s_ref[g] = _dot(gm.astype(BF16), s_b) + hm
        return carry

    lax.fori_loop(0, n_chunks, chunk_step, 0)

    y = y_ref[...]
    mean = _head_sum(y, ones_b) * (1.0 / HEAD_DIM)
    yc = y - mean
    var = _head_sum(yc * yc, ones_b) * (1.0 / HEAD_DIM)
    yn = yc * lax.rsqrt(var + GN_EPS) * lw_ref[...] + lb_ref[...] + bonus
    o_ref[...] = (yn * gs_ref[...].astype(F32)).astype(BF16)


def _rwkv(pr, gs, mu, w0, w2a, a0, k_k, k_a, r_k, lnx_w, lnx_b, batch, seq):
    m = pr.shape[0]
    per_b = seq // TOK_TILE
    W = RWKV_WIDTH
    tile = lambda b, t: (b * per_b + t, 0)
    const = lambda b, t: (0, 0)
    vec = pl.BlockSpec((1, W), const)
    tile_scratch = pltpu.VMEM((TOK_TILE, W), F32)
    return pl.pallas_call(
        _rwkv_body,
        grid=(batch, per_b),
        in_specs=[
            pl.BlockSpec((TOK_TILE, N_RWKV_COLS), tile),
            pl.BlockSpec((TOK_TILE, W), tile),
            pl.BlockSpec((1, N_RWKV_COLS), const),
            vec,
            pl.BlockSpec(w2a.shape, const),
            vec, vec, vec, vec, vec, vec,
        ],
        out_specs=pl.BlockSpec((TOK_TILE, W), tile),
        out_shape=jax.ShapeDtypeStruct((m, W), BF16),
        scratch_shapes=[
            pltpu.VMEM((W // GROUP_W, GROUP_W, GROUP_W), F32),
            pltpu.VMEM((1, N_RWKV_COLS), F32),
            tile_scratch, tile_scratch, tile_scratch, tile_scratch, tile_scratch, tile_scratch, tile_scratch,
        ],
        compiler_params=pltpu.CompilerParams(
            dimension_semantics=("arbitrary", "arbitrary"), vmem_limit_bytes=VMEM_LIMIT),
        name="rwkv7",
    )(pr, gs, mu, w0, w2a, a0, k_k, k_a, r_k, lnx_w, lnx_b)


def _moba_body(slopes_ref, q_ref, k_ref, vt_ref, km_ref, gs_ref, o_ref, sel_ref):
    BS = MOBA_BLOCK
    pair = pl.program_id(1)
    i = pl.program_id(2)
    heads = LANES // HEAD_DIM
    q = q_ref[...]
    km = km_ref[0]
    nb = km.shape[0]
    lane_h = lax.broadcasted_iota(jnp.int32, (1, LANES), 1) // HEAD_DIM
    n_iota = lax.broadcasted_iota(jnp.int32, (nb, BS), 0)
    s_loc = lax.broadcasted_iota(jnp.int32, (BS, BS), 0)
    t_loc = lax.broadcasted_iota(jnp.int32, (BS, BS), 1)
    d0 = (s_loc - t_loc).astype(F32)
    causal = s_loc <= t_loc
    own = pl.ds(pl.multiple_of(i * BS, BS), BS)

    qb, slope, m0, l0, acc0 = [], [], [], [], []
    for h in range(heads):
        qh = jnp.where(lane_h == h, q, 0.0)
        qb.append(qh.astype(BF16))
        slope.append(slopes_ref[pair * heads + h])
        gate = _dot(km, qh, _NT, precision=lax.Precision.HIGHEST)
        gate = jnp.where(n_iota < i, gate, NEG)
        sel = jnp.zeros((nb, BS), F32)
        for rnk in range(MOBA_TOPK):
            mx = jnp.max(gate, axis=0, keepdims=True)
            idx = jnp.min(jnp.where(gate == mx, n_iota, nb), axis=0, keepdims=True)
            pick = n_iota == idx
            sel = jnp.where(pick, jnp.where(rnk < i, 1.0, 0.0), sel)
            gate = jnp.where(pick, -jnp.inf, gate)
        sel_ref[h] = sel
        st = _dot(k_ref[own, :], qb[h], _NT) + slope[h] * d0
        st = jnp.where(causal, st, NEG)
        m = jnp.max(st, axis=0, keepdims=True)
        pexp = jnp.exp(st - m)
        m0.append(m)
        l0.append(jnp.sum(pexp, axis=0, keepdims=True))
        acc0.append(_dot(vt_ref[0, h * HEAD_DIM:(h + 1) * HEAD_DIM, own], pexp.astype(BF16)))

    def kv_step(j, carry):
        ms, ls, accs = carry
        blk = pl.ds(pl.multiple_of(j * BS, BS), BS)
        kj = k_ref[blk, :]
        off = ((j - i) * BS).astype(F32)
        ms2, ls2, accs2 = [], [], []
        for h in range(heads):
            st = _dot(kj, qb[h], _NT) + slope[h] * (d0 + off)
            st = jnp.where(sel_ref[h, pl.ds(j, 1), :] > 0.0, st, NEG)
            m_new = jnp.maximum(ms[h], jnp.max(st, axis=0, keepdims=True))
            alpha = jnp.exp(ms[h] - m_new)
            pexp = jnp.exp(st - m_new)
            ls2.append(alpha * ls[h] + jnp.sum(pexp, axis=0, keepdims=True))
            accs2.append(alpha * accs[h]
                         + _dot(vt_ref[0, h * HEAD_DIM:(h + 1) * HEAD_DIM, blk], pexp.astype(BF16)))
            ms2.append(m_new)
        return tuple(ms2), tuple(ls2), tuple(accs2)

    ms, ls, accs = lax.fori_loop(0, i, kv_step, (tuple(m0), tuple(l0), tuple(acc0)))
    ot = jnp.concatenate([accs[h] / ls[h] for h in range(heads)], axis=0)
    o_ref[...] = (ot.T * gs_ref[...].astype(F32)).astype(BF16)


def _moba(slopes, q, k, vt, kmean, gs, batch, seq):
    m = q.shape[0]
    BS = MOBA_BLOCK
    nb = seq // BS
    n_pairs = ATTN_WIDTH // LANES
    gate_col0 = RWKV_WIDTH // LANES
    return pl.pallas_call(
        _moba_body,
        grid_spec=pltpu.PrefetchScalarGridSpec(
            num_scalar_prefetch=1,
            grid=(batch, n_pairs, nb),
            in_specs=[
                pl.BlockSpec((BS, LANES), lambda b, p, i, s: (b * nb + i, p)),
                pl.BlockSpec((seq, LANES), lambda b, p, i, s: (b, p)),
                pl.BlockSpec((1, LANES, seq), lambda b, p, i, s: (b, p, 0)),
                pl.BlockSpec((1, nb, LANES), lambda b, p, i, s: (b, 0, p)),
                pl.BlockSpec((BS, LANES), lambda b, p, i, s: (b * nb + i, gate_col0 + p)),
            ],
            out_specs=pl.BlockSpec((BS, LANES), lambda b, p, i, s: (b * nb + i, p)),
            scratch_shapes=[pltpu.VMEM((LANES // HEAD_DIM, nb, BS), F32)],
        ),
        out_shape=jax.ShapeDtypeStruct((m, ATTN_WIDTH), BF16),
        compiler_params=pltpu.CompilerParams(
            dimension_semantics=("arbitrary", "arbitrary", "arbitrary"), vmem_limit_bytes=VMEM_LIMIT),
        name="moba",
    )(slopes, q, k, vt, kmean, gs)


def _outproj_body(x_ref, yr_ref, ya_ref, wr_ref, wa_ref, g_ref, o_ref):
    o = _dot(yr_ref[...], wr_ref[...]) + _dot(ya_ref[...], wa_ref[...])
    ms = jnp.mean(o * o, axis=-1, keepdims=True)
    o_ref[...] = x_ref[...] + o * lax.rsqrt(ms + RMS_EPS) * g_ref[...]


def _out_proj(x2, yr, ya, w_or, w_oa, g_post):
    m, d = x2.shape
    tm = 2 * TOK_TILE
    const = lambda i: (0, 0)
    return pl.pallas_call(
        _outproj_body,
        grid=(m // tm,),
        in_specs=[
            pl.BlockSpec((tm, d), lambda i: (i, 0)),
            pl.BlockSpec((tm, RWKV_WIDTH), lambda i: (i, 0)),
            pl.BlockSpec((tm, ATTN_WIDTH), lambda i: (i, 0)),
            pl.BlockSpec(w_or.shape, const),
            pl.BlockSpec(w_oa.shape, const),
            pl.BlockSpec((1, d), const),
        ],
        out_specs=pl.BlockSpec((tm, d), lambda i: (i, 0)),
        out_shape=jax.ShapeDtypeStruct((m, d), F32),
        compiler_params=pltpu.CompilerParams(
            dimension_semantics=("arbitrary",), vmem_limit_bytes=VMEM_LIMIT),
        name="out_proj",
    )(x2, yr, ya, w_or, w_oa, g_post)


def kernel(x, g_pre, w_in, tshift_mu, w0, w2, a0, a2, k_k, k_a, r_k, lnx_w, lnx_b, w_out, g_post):
    batch, seq, d = x.shape
    depth = w_in.shape[0]
    assert seq % TOK_TILE == 0 and TOK_TILE == MOBA_BLOCK and d % LANES == 0
    x2 = x.reshape(batch * seq, d)
    slopes = 2.0 ** (-8.0 * jnp.arange(1, ATTN_HEADS + 1, dtype=F32) / ATTN_HEADS)
    row = lambda z: z.reshape(1, -1).astype(F32)
    for l in range(depth):
        wb = w_in[l].astype(BF16)
        w_r = wb[:, :N_RWKV_COLS]
        w_a = wb[:, N_RWKV_COLS:N_RWKV_COLS + N_ATTN_COLS]
        w_g = wb[:, N_RWKV_COLS + N_ATTN_COLS:]
        pr, q, k, vt, kmean, gs = _in_proj(x2, row(g_pre[l]), w_r, w_a, w_g, batch, seq)
        zeros = jnp.zeros((DECAY_RANK, RWKV_WIDTH), F32)
        w2a = jnp.concatenate([jnp.concatenate([w2[l], zeros], axis=1),
                               jnp.concatenate([zeros, a2[l]], axis=1)], axis=0)
        yr = _rwkv(pr, gs, row(tshift_mu[l]), row(w0[l]), w2a, row(a0[l]), row(k_k[l]), row(k_a[l]),
                   row(r_k[l]), row(lnx_w[l]), row(lnx_b[l]), batch, seq)
        ya = _moba(slopes, q, k, vt, kmean.reshape(batch, seq // MOBA_BLOCK, ATTN_WIDTH), gs, batch, seq)
        wo = w_out[l].astype(BF16)
        x2 = _out_proj(x2, yr, ya, wo[:RWKV_WIDTH], wo[RWKV_WIDTH:], row(g_post[l]))
    return x2.reshape(batch, seq, d)
```

```python
import functools

import jax
import jax.numpy as jnp
from jax import lax
from jax.experimental import pallas as pl
from jax.experimental.pallas import tpu as pltpu

F32 = jnp.float32
BF16 = jnp.bfloat16

HEAD_DIM = 64
RWKV_HEADS = 8
ATTN_HEADS = 8
RWKV_WIDTH = RWKV_HEADS * HEAD_DIM
ATTN_WIDTH = ATTN_HEADS * HEAD_DIM
D_MIX = RWKV_WIDTH + ATTN_WIDTH
DECAY_RANK = 64
ICLR_RANK = 64
MOBA_BLOCK = 256
MOBA_TOPK = 3
RMS_EPS = 1e-6
GN_EPS = 64e-5
NEG = -1e30
N_RWKV_COLS = 3 * RWKV_WIDTH + DECAY_RANK + ICLR_RANK
N_ATTN_COLS = 3 * ATTN_WIDTH

LANES = 128
MXU_DIM = 256
CHUNK = 64
GROUP_HEADS = MXU_DIM // HEAD_DIM
GROUP_W = GROUP_HEADS * HEAD_DIM
TOK_TILE = 256
VMEM_LIMIT = 48 * 1024 * 1024

_NT = (((1,), (1,)), ((), ()))
_TN = (((0,), (0,)), ((), ()))


def _dot(a, b, dims=None, precision=None):
    if dims is None:
        return jnp.dot(a, b, preferred_element_type=F32, precision=precision)
    return lax.dot_general(a, b, dims, preferred_element_type=F32, precision=precision)


def _inproj_body(x_ref, g_ref, wr_ref, wa_ref, wg_ref, pr_ref, q_ref, k_ref, vt_ref, km_ref, gs_ref):
    x = x_ref[...]
    ms = jnp.mean(x * x, axis=-1, keepdims=True)
    h = (x * lax.rsqrt(ms + RMS_EPS) * g_ref[...]).astype(BF16)
    pr_ref[...] = _dot(h, wr_ref[...])
    qkv = _dot(h, wa_ref[...])
    q_ref[...] = qkv[:, :ATTN_WIDTH] * (HEAD_DIM ** -0.5)
    kf = qkv[:, ATTN_WIDTH:2 * ATTN_WIDTH]
    k_ref[...] = kf.astype(BF16)
    km_ref[0] = jnp.mean(kf, axis=0, keepdims=True)
    vt_ref[0] = qkv[:, 2 * ATTN_WIDTH:].T.astype(BF16)
    g = _dot(h, wg_ref[...])
    gs_ref[...] = (g * jax.nn.sigmoid(g)).astype(BF16)


def _in_proj(x2, g_pre, w_r, w_a, w_g, batch, seq):
    m, d = x2.shape
    nblk = m // TOK_TILE
    per_b = seq // TOK_TILE
    const = lambda i: (0, 0)
    return pl.pallas_call(
        _inproj_body,
        grid=(nblk,),
        in_specs=[
            pl.BlockSpec((TOK_TILE, d), lambda i: (i, 0)),
            pl.BlockSpec((1, d), const),
            pl.BlockSpec(w_r.shape, const),
            pl.BlockSpec(w_a.shape, const),
            pl.BlockSpec(w_g.shape, const),
        ],
        out_specs=[
            pl.BlockSpec((TOK_TILE, N_RWKV_COLS), lambda i: (i, 0)),
            pl.BlockSpec((TOK_TILE, ATTN_WIDTH), lambda i: (i, 0)),
            pl.BlockSpec((TOK_TILE, ATTN_WIDTH), lambda i: (i, 0)),
            pl.BlockSpec((1, ATTN_WIDTH, TOK_TILE), lambda i: (i // per_b, 0, i % per_b)),
            pl.BlockSpec((1, 1, ATTN_WIDTH), lambda i: (i, 0, 0)),
            pl.BlockSpec((TOK_TILE, D_MIX), lambda i: (i, 0)),
        ],
        out_shape=[
            jax.ShapeDtypeStruct((m, N_RWKV_COLS), F32),
            jax.ShapeDtypeStruct((m, ATTN_WIDTH), F32),
            jax.ShapeDtypeStruct((m, ATTN_WIDTH), BF16),
            jax.ShapeDtypeStruct((batch, ATTN_WIDTH, seq), BF16),
            jax.ShapeDtypeStruct((nblk, 1, ATTN_WIDTH), F32),
            jax.ShapeDtypeStruct((m, D_MIX), BF16),
        ],
        compiler_params=pltpu.CompilerParams(
            dimension_semantics=("arbitrary",), vmem_limit_bytes=VMEM_LIMIT),
        name="in_proj",
    )(x2, g_pre, w_r, w_a, w_g)


def _split_dot(x, ones_b):
    hi = x.astype(BF16)
    lo = (x - hi.astype(F32)).astype(BF16)
    return _dot(hi, ones_b) + _dot(lo, ones_b)


def _head_sum(x, ones_b):
    parts = [_split_dot(x[:, g * GROUP_W:(g + 1) * GROUP_W], ones_b)
             for g in range(x.shape[1] // GROUP_W)]
    return jnp.concatenate(parts, axis=1)


def _rwkv_body(pr_ref, gs_ref, mu_ref, w0_ref, w2a_ref, a0_ref, kk_ref, ka_ref, rk_ref, lw_ref, lb_ref,
               o_ref, s_ref, prev_ref, at_ref, bt_ref, kt_ref, rt_ref, v_ref, gl_ref, y_ref):
    W = RWKV_WIDTH
    n_groups = W // GROUP_W
    n_chunks = TOK_TILE // CHUNK

    @pl.when(pl.program_id(1) == 0)
    def _():
        s_ref[...] = jnp.zeros_like(s_ref)
        prev_ref[...] = jnp.zeros_like(prev_ref)

    rr = lax.broadcasted_iota(jnp.int32, (GROUP_W, GROUP_W), 0)
    cc = lax.broadcasted_iota(jnp.int32, (GROUP_W, GROUP_W), 1)
    same_head = (rr // HEAD_DIM) == (cc // HEAD_DIM)
    ones_b = same_head.astype(BF16)
    eye_g = rr == cc
    tril_bd = (same_head & (cc <= rr)).astype(F32)
    wr = lax.broadcasted_iota(jnp.int32, (CHUNK, GROUP_W), 0)
    wc = lax.broadcasted_iota(jnp.int32, (CHUNK, GROUP_W), 1) % CHUNK
    strict_w = wc < wr
    incl_w = wc <= wr
    eye_w = (wc == wr).astype(F32)
    lane_g = lax.broadcasted_iota(jnp.int32, (1, GROUP_W), 1) // HEAD_DIM
    head_masks = [lane_g == h for h in range(GROUP_HEADS)]

    def mstack(x):
        return jnp.concatenate([jnp.where(hm, x, 0.0) for hm in head_masks], axis=0).astype(BF16)

    p = pr_ref[...]
    rolled = pltpu.roll(p, 1, axis=0)
    row0 = lax.broadcasted_iota(jnp.int32, (TOK_TILE, 1), 0) == 0
    p_prev = jnp.where(row0, prev_ref[...], rolled)
    prev_ref[...] = p[TOK_TILE - 1:TOK_TILE, :]
    p = p + (p_prev - p) * mu_ref[...]
    r = p[:, 0:W]
    k = p[:, W:2 * W]
    v = p[:, 2 * W:3 * W]
    z = p[:, 3 * W:3 * W + DECAY_RANK + ICLR_RANK]
    lane_z = lax.broadcasted_iota(jnp.int32, (1, DECAY_RANK + ICLR_RANK), 1)
    z = jnp.where(lane_z < DECAY_RANK, jnp.tanh(z), z)
    lin = _dot(z, w2a_ref[...], precision=lax.Precision.HIGHEST)
    u = -(w0_ref[...] + lin[:, :W])
    softplus = jnp.maximum(u, 0.0) + jnp.log1p(jnp.exp(-jnp.abs(u)))
    w = -softplus - 0.5
    ld = -jnp.exp(w)
    a = jax.nn.sigmoid(a0_ref[...] + lin[:, W:])
    kkr = k * kk_ref[...]
    nrm = jnp.sqrt(_head_sum(kkr * kkr, ones_b))
    kk = kkr / jnp.maximum(nrm, 1e-12)
    k2 = k * (1.0 + (a - 1.0) * ka_ref[...])
    cs = _dot(tril_bd, ld, precision=lax.Precision.HIGHEST)
    g_inv = jnp.exp(-cs)
    at_ref[...] = (-kk) * jnp.exp(cs - ld)
    bt_ref[...] = kk * a * g_inv
    kt_ref[...] = k2 * g_inv
    rt_ref[...] = r * jnp.exp(cs)
    v_ref[...] = v
    gl_ref[...] = jnp.exp(cs)
    bonus = _head_sum(r * k2 * rk_ref[...], ones_b) * v

    def chunk_step(c, carry):
        r0 = pl.multiple_of(c * CHUNK, CHUNK)
        rows = pl.ds(r0, CHUNK)
        for g in range(n_groups):
            cols = slice(g * GROUP_W, (g + 1) * GROUP_W)
            at = at_ref[rows, cols]
            bt = bt_ref[rows, cols]
            kt = kt_ref[rows, cols]
            rt = rt_ref[rows, cols]
            vv = v_ref[rows, cols]
            gl = gl_ref[pl.ds(r0 + CHUNK - 1, 1), cols]
            lhs = jnp.concatenate([at, rt], axis=0).astype(BF16)
            rhs = jnp.concatenate([mstack(bt), mstack(kt)], axis=0)
            x = _dot(lhs, rhs, _NT)
            a_ab = jnp.where(strict_w, x[:CHUNK, :GROUP_W], 0.0)
            a_ak = jnp.where(strict_w, x[:CHUNK, GROUP_W:], 0.0)
            m_rb = jnp.where(incl_w, x[CHUNK:, :GROUP_W], 0.0)
            m_rk = jnp.where(incl_w, x[CHUNK:, GROUP_W:], 0.0)
            pw = jnp.where(same_head, jnp.concatenate([a_ab] * GROUP_HEADS, axis=0), 0.0)
            tw = a_ab + eye_w
            n_sq = CHUNK.bit_length() - 2
            for it in range(n_sq):
                pb = pw.astype(BF16)
                pw = _dot(pb, pb)
                tw = tw + _dot(tw.astype(BF16), pw.astype(BF16))
            twb = tw.astype(BF16)
            wt = _dot(twb, mstack(at))
            u0 = _dot(twb, mstack(_dot(a_ak.astype(BF16), mstack(vv))))
            m_rb_b = m_rb.astype(BF16)
            q = rt + _dot(m_rb_b, mstack(wt))
            y0 = _dot(m_rb_b, mstack(u0)) + _dot(m_rk.astype(BF16), mstack(vv))
            bh = (bt * gl).astype(BF16)
            kh = (kt * gl).astype(BF16)
            gm = jnp.where(eye_g, gl, 0.0) + jnp.where(same_head, _dot(bh, wt.astype(BF16), _TN), 0.0)
            hm = jnp.where(
                same_head,
                _dot(jnp.concatenate([bh, kh], axis=0),
                     jnp.concatenate([u0, vv], axis=0).astype(BF16), _TN),
                0.0)
            s_b = s_ref[g].astype(BF16)
            y_ref[rows, cols] = _dot(q.astype(BF16), s_b) + y0
            s_ref[g] = _dot(gm.astype(BF16), s_b) + hm
        return carry

    lax.fori_loop(0, n_chunks, chunk_step, 0)

    y = y_ref[...]
    mean = _head_sum(y, ones_b) * (1.0 / HEAD_DIM)
    yc = y - mean
    var = _head_sum(yc * yc, ones_b) * (1.0 / HEAD_DIM)
    yn = yc * lax.rsqrt(var + GN_EPS) * lw_ref[...] + lb_ref[...] + bonus
    o_ref[...] = (yn * gs_ref[...].astype(F32)).astype(BF16)


def _rwkv(pr, gs, mu, w0, w2a, a0, k_k, k_a, r_k, lnx_w, lnx_b, batch, seq):
    m = pr.shape[0]
    per_b = seq // TOK_TILE
    W = RWKV_WIDTH
    tile = lambda b, t: (b * per_b + t, 0)
    const = lambda b, t: (0, 0)
    vec = pl.BlockSpec((1, W), const)
    tile_scratch = pltpu.VMEM((TOK_TILE, W), F32)
    return pl.pallas_call(
        _rwkv_body,
        grid=(batch, per_b),
        in_specs=[
            pl.BlockSpec((TOK_TILE, N_RWKV_COLS), tile),
            pl.BlockSpec((TOK_TILE, W), tile),
            pl.BlockSpec((1, N_RWKV_COLS), const),
            vec,
            pl.BlockSpec(w2a.shape, const),
            vec, vec, vec, vec, vec, vec,
        ],
        out_specs=pl.BlockSpec((TOK_TILE, W), tile),
        out_shape=jax.ShapeDtypeStruct((m, W), BF16),
        scratch_shapes=[
            pltpu.VMEM((W // GROUP_W, GROUP_W, GROUP_W), F32),
            pltpu.VMEM((1, N_RWKV_COLS), F32),
            tile_scratch, tile_scratch, tile_scratch, tile_scratch, tile_scratch, tile_scratch, tile_scratch,
        ],
        compiler_params=pltpu.CompilerParams(
            dimension_semantics=("arbitrary", "arbitrary"), vmem_limit_bytes=VMEM_LIMIT),
        name="rwkv7",
    )(pr, gs, mu, w0, w2a, a0, k_k, k_a, r_k, lnx_w, lnx_b)


LOG2E = 1.4426950408889634
AUX_S, AUX_T, AUX_BLK, AUX_SEL = 0, 3, 6, 16
PV_ONES_ROWS = 16
MOBA_STEP_HEADS = 4


def _moba_key_aux(seq):
    pos = jnp.arange(seq, dtype=jnp.int32)[:, None]
    lane = jnp.arange(LANES, dtype=jnp.int32)[None, :]
    blk = pos // MOBA_BLOCK
    tab = jnp.where(lane < AUX_T, pos % MOBA_BLOCK,
                    jnp.where(lane < AUX_BLK, 1,
                              jnp.where(lane < AUX_BLK + 3, blk * MOBA_BLOCK,
                                        jnp.where(lane == AUX_SEL + blk, 1, 0))))
    return tab.astype(BF16)


def _split3(x):
    x1 = x.astype(BF16).astype(F32)
    r1 = x - x1
    x2 = r1.astype(BF16).astype(F32)
    x3 = (r1 - x2).astype(BF16).astype(F32)
    return x1, x2, x3


def _moba_body(slopes_ref, q_ref, k_ref, ak_ref, vt_ref, km_ref, gs_ref, o_ref, st_ref, p_ref):
    BS = MOBA_BLOCK
    grp = pl.program_id(1)
    i = pl.program_id(2)
    heads = MOBA_STEP_HEADS
    per_pair = LANES // HEAD_DIM
    nb = km_ref.shape[1]
    lane_h = lax.broadcasted_iota(jnp.int32, (1, LANES), 1) // HEAD_DIM
    n_iota = lax.broadcasted_iota(jnp.int32, (nb, BS), 0)
    causal = (lax.broadcasted_iota(jnp.int32, (BS, BS), 0)
              <= lax.broadcasted_iota(jnp.int32, (BS, BS), 1))
    t_glob = (lax.broadcasted_iota(jnp.int32, (nb, BS), 1) + i * BS).astype(F32)
    piece = n_iota % 3
    ones_rows = jnp.ones((PV_ONES_ROWS, BS), BF16)

    def key_rows(n):
        b = jnp.where(n == 0, i, n - 1)
        return pl.ds(pl.multiple_of(b * BS, BS), BS)

    def pair_lanes(h):
        p = h // per_pair
        return slice(p * LANES, (p + 1) * LANES)

    def scores(h, blk, q_aug_t):
        k_aug = jnp.concatenate([k_ref[blk, pair_lanes(h)], ak_ref[blk, :]], axis=1)
        return _dot(k_aug, q_aug_t)

    def pv(h, blk):
        vt_aug = jnp.concatenate([vt_ref[0, h * HEAD_DIM:(h + 1) * HEAD_DIM, blk], ones_rows], axis=0)
        return _dot(vt_aug, p_ref[h])

    q = q_ref[...]
    km = km_ref[0]
    lane_head = lax.broadcasted_iota(jnp.int32, (1, heads * HEAD_DIM), 1) // HEAD_DIM
    km_stack = jnp.concatenate([jnp.where(lane_head == h, km, 0.0) for h in range(heads)], axis=0)
    gates = _dot(km_stack, q, _NT, precision=lax.Precision.HIGHEST)
    gate = [jnp.where(n_iota < i, gates[h * nb:(h + 1) * nb], NEG) for h in range(heads)]
    sel = [jnp.zeros((nb, BS), F32) for _ in range(heads)]
    for rnk in range(MOBA_TOPK):
        rank_ok = jnp.where(rnk < i, 1.0, 0.0)
        for h in range(heads):
            mx = jnp.max(gate[h], axis=0, keepdims=True)
            idx = jnp.min(jnp.where(gate[h] == mx, n_iota, nb), axis=0, keepdims=True)
            pick = n_iota == idx
            sel[h] = jnp.where(pick, rank_ok, sel[h])
            gate[h] = jnp.where(pick, -jnp.inf, gate[h])
    q_t = (q * LOG2E).T
    row_h = lax.broadcasted_iota(jnp.int32, (LANES, 1), 0) // HEAD_DIM
    q_augs = []
    for h in range(heads):
        keep = (sel[h] > 0.0) | (n_iota == i)
        cvec = jnp.full((nb, BS), slopes_ref[grp * heads + h] * LOG2E, F32)
        c1, c2, c3 = _split3(cvec)
        e1, e2, e3 = _split3(-cvec * t_glob)
        cp = jnp.where(piece == 0, c1, jnp.where(piece == 1, c2, c3))
        ep = jnp.where(piece == 0, e1, jnp.where(piece == 1, e2, e3))
        top = jnp.where((n_iota >= AUX_T) & (n_iota < AUX_BLK), ep, cp)
        top = jnp.where(n_iota < AUX_BLK + 3, top, 0.0)
        aux_t = jnp.concatenate(
            [top, jnp.where(keep, 0.0, NEG), jnp.zeros((LANES - AUX_SEL - nb, BS), F32)], axis=0)
        qh_t = jnp.where(row_h == h % per_pair, q_t[pair_lanes(h), :], 0.0)
        q_augs.append(jnp.concatenate([qh_t, aux_t], axis=0).astype(BF16))
    for h in range(heads):
        st_ref[h] = jnp.where(causal, scores(h, key_rows(0), q_augs[h]), NEG)
        p_ref[h] = jnp.zeros((BS, BS), BF16)

    def kv_step(n, carry):
        ms, accs, alphas = carry
        prev_rows = key_rows(jnp.maximum(n - 1, 0))
        next_rows = key_rows(n + 1)
        ms2, accs2, alphas2 = [], [], []
        for h in range(heads):
            accs2.append(alphas[h] * accs[h] + pv(h, prev_rows))
            st = st_ref[h]
            m_new = jnp.maximum(ms[h], jnp.max(st, axis=0, keepdims=True))
            alphas2.append(jnp.exp2(ms[h] - m_new))
            p_ref[h] = jnp.exp2(st - m_new).astype(BF16)
            st_ref[h] = scores(h, next_rows, q_augs[h])
            ms2.append(m_new)
        return tuple(ms2), tuple(accs2), tuple(alphas2)

    m0 = tuple(jnp.full((1, BS), NEG, F32) for _ in range(heads))
    acc0 = tuple(jnp.zeros((HEAD_DIM + PV_ONES_ROWS, BS), F32) for _ in range(heads))
    alpha0 = tuple(jnp.ones((1, BS), F32) for _ in range(heads))
    ms, accs, alphas = lax.fori_loop(0, i + 1, kv_step, (m0, acc0, alpha0))
    last_rows = key_rows(i)
    outs = []
    for h in range(heads):
        acc = alphas[h] * accs[h] + pv(h, last_rows)
        outs.append(acc[:HEAD_DIM] / acc[HEAD_DIM:HEAD_DIM + 1])
    ot = jnp.concatenate(outs, axis=0)
    o_ref[...] = (ot.T * gs_ref[...].astype(F32)).astype(BF16)


def _moba(slopes, q, k, key_aux, vt, kmean, gs, batch, seq):
    m = q.shape[0]
    BS = MOBA_BLOCK
    nb = seq // BS
    assert nb == AUX_SEL and AUX_SEL + nb <= LANES and AUX_BLK + 3 <= AUX_SEL
    W = MOBA_STEP_HEADS * HEAD_DIM
    n_groups = ATTN_WIDTH // W
    gate_col0 = RWKV_WIDTH // W
    return pl.pallas_call(
        _moba_body,
        grid_spec=pltpu.PrefetchScalarGridSpec(
            num_scalar_prefetch=1,
            grid=(batch, n_groups, nb),
            in_specs=[
                pl.BlockSpec((BS, W), lambda b, g, i, s: (b * nb + i, g)),
                pl.BlockSpec((seq, W), lambda b, g, i, s: (b, g)),
                pl.BlockSpec((seq, LANES), lambda b, g, i, s: (0, 0)),
                pl.BlockSpec((1, W, seq), lambda b, g, i, s: (b, g, 0)),
                pl.BlockSpec((1, nb, W), lambda b, g, i, s: (b, 0, g)),
                pl.BlockSpec((BS, W), lambda b, g, i, s: (b * nb + i, gate_col0 + g)),
            ],
            out_specs=pl.BlockSpec((BS, W), lambda b, g, i, s: (b * nb + i, g)),
            scratch_shapes=[
                pltpu.VMEM((MOBA_STEP_HEADS, BS, BS), F32),
                pltpu.VMEM((MOBA_STEP_HEADS, BS, BS), BF16),
            ],
        ),
        out_shape=jax.ShapeDtypeStruct((m, ATTN_WIDTH), BF16),
        compiler_params=pltpu.CompilerParams(
            dimension_semantics=("arbitrary", "arbitrary", "arbitrary"), vmem_limit_bytes=VMEM_LIMIT),
        name="moba",
    )(slopes, q, k, key_aux, vt, kmean, gs)


def _outproj_body(x_ref, yr_ref, ya_ref, wr_ref, wa_ref, g_ref, o_ref):
    o = _dot(yr_ref[...], wr_ref[...]) + _dot(ya_ref[...], wa_ref[...])
    ms = jnp.mean(o * o, axis=-1, keepdims=True)
    o_ref[...] = x_ref[...] + o * lax.rsqrt(ms + RMS_EPS) * g_ref[...]


def _out_proj(x2, yr, ya, w_or, w_oa, g_post):
    m, d = x2.shape
    tm = 2 * TOK_TILE
    const = lambda i: (0, 0)
    return pl.pallas_call(
        _outproj_body,
        grid=(m // tm,),
        in_specs=[
            pl.BlockSpec((tm, d), lambda i: (i, 0)),
            pl.BlockSpec((tm, RWKV_WIDTH), lambda i: (i, 0)),
            pl.BlockSpec((tm, ATTN_WIDTH), lambda i: (i, 0)),
            pl.BlockSpec(w_or.shape, const),
            pl.BlockSpec(w_oa.shape, const),
            pl.BlockSpec((1, d), const),
        ],
        out_specs=pl.BlockSpec((tm, d), lambda i: (i, 0)),
        out_shape=jax.ShapeDtypeStruct((m, d), F32),
        compiler_params=pltpu.CompilerParams(
            dimension_semantics=("arbitrary",), vmem_limit_bytes=VMEM_LIMIT),
        name="out_proj",
    )(x2, yr, ya, w_or, w_oa, g_post)


def kernel(x, g_pre, w_in, tshift_mu, w0, w2, a0, a2, k_k, k_a, r_k, lnx_w, lnx_b, w_out, g_post):
    batch, seq, d = x.shape
    depth = w_in.shape[0]
    assert seq % TOK_TILE == 0 and TOK_TILE == MOBA_BLOCK and d % LANES == 0
    x2 = x.reshape(batch * seq, d)
    slopes = 2.0 ** (-8.0 * jnp.arange(1, ATTN_HEADS + 1, dtype=F32) / ATTN_HEADS)
    row = lambda z: z.reshape(1, -1).astype(F32)
    for l in range(depth):
        wb = w_in[l].astype(BF16)
        w_r = wb[:, :N_RWKV_COLS]
        w_a = wb[:, N_RWKV_COLS:N_RWKV_COLS + N_ATTN_COLS]
        w_g = wb[:, N_RWKV_COLS + N_ATTN_COLS:]
        pr, q, k, vt, kmean, gs = _in_proj(x2, row(g_pre[l]), w_r, w_a, w_g, batch, seq)
        zeros = jnp.zeros((DECAY_RANK, RWKV_WIDTH), F32)
        w2a = jnp.concatenate([jnp.concatenate([w2[l], zeros], axis=1),
                               jnp.concatenate([zeros, a2[l]], axis=1)], axis=0)
        yr = _rwkv(pr, gs, row(tshift_mu[l]), row(w0[l]), w2a, row(a0[l]), row(k_k[l]), row(k_a[l]),
                   row(r_k[l]), row(lnx_w[l]), row(lnx_b[l]), batch, seq)
        ya = _moba(slopes, q, k, _moba_key_aux(seq), vt,
                   kmean.reshape(batch, seq // MOBA_BLOCK, ATTN_WIDTH), gs, batch, seq)
        wo = w_out[l].astype(BF16)
        x2 = _out_proj(x2, yr, ya, wo[:RWKV_WIDTH], wo[RWKV_WIDTH:], row(g_post[l]))
    return x2.reshape(batch, seq, d)
```

```python
import functools

import jax
import jax.numpy as jnp
from jax import lax
from jax.experimental import pallas as pl
from jax.experimental.pallas import tpu as pltpu

F32 = jnp.float32
BF16 = jnp.bfloat16

HEAD_DIM = 64
RWKV_HEADS = 8
ATTN_HEADS = 8
RWKV_WIDTH = RWKV_HEADS * HEAD_DIM
ATTN_WIDTH = ATTN_HEADS * HEAD_DIM
D_MIX = RWKV_WIDTH + ATTN_WIDTH
DECAY_RANK = 64
ICLR_RANK = 64
MOBA_BLOCK = 256
MOBA_TOPK = 3
RMS_EPS = 1e-6
GN_EPS = 64e-5
NEG = -1e30
N_RWKV_COLS = 3 * RWKV_WIDTH + DECAY_RANK + ICLR_RANK
N_ATTN_COLS = 3 * ATTN_WIDTH

LANES = 128
MXU_DIM = 256
CHUNK = 64
GROUP_HEADS = MXU_DIM // HEAD_DIM
GROUP_W = GROUP_HEADS * HEAD_DIM
TOK_TILE = 256
RWKV_LOCKSTEP = 8
VMEM_LIMIT = 48 * 1024 * 1024

_NT = (((1,), (1,)), ((), ()))
_TN = (((0,), (0,)), ((), ()))


def _dot(a, b, dims=None, precision=None):
    if dims is None:
        return jnp.dot(a, b, preferred_element_type=F32, precision=precision)
    return lax.dot_general(a, b, dims, preferred_element_type=F32, precision=precision)


def _inproj_body(x_ref, g_ref, wr_ref, wa_ref, wg_ref, pr_ref, q_ref, k_ref, vt_ref, km_ref, gs_ref):
    x = x_ref[...]
    ms = jnp.mean(x * x, axis=-1, keepdims=True)
    h = (x * lax.rsqrt(ms + RMS_EPS) * g_ref[...]).astype(BF16)
    pr_ref[...] = _dot(h, wr_ref[...])
    qkv = _dot(h, wa_ref[...])
    q_ref[...] = qkv[:, :ATTN_WIDTH] * (HEAD_DIM ** -0.5)
    kf = qkv[:, ATTN_WIDTH:2 * ATTN_WIDTH]
    k_ref[...] = kf.astype(BF16)
    km_ref[0] = jnp.mean(kf, axis=0, keepdims=True)
    vt_ref[0] = qkv[:, 2 * ATTN_WIDTH:].T.astype(BF16)
    g = _dot(h, wg_ref[...])
    gs_ref[...] = (g * jax.nn.sigmoid(g)).astype(BF16)


def _in_proj(x2, g_pre, w_r, w_a, w_g, batch, seq):
    m, d = x2.shape
    nblk = m // TOK_TILE
    per_b = seq // TOK_TILE
    const = lambda i: (0, 0)
    return pl.pallas_call(
        _inproj_body,
        grid=(nblk,),
        in_specs=[
            pl.BlockSpec((TOK_TILE, d), lambda i: (i, 0)),
            pl.BlockSpec((1, d), const),
            pl.BlockSpec(w_r.shape, const),
            pl.BlockSpec(w_a.shape, const),
            pl.BlockSpec(w_g.shape, const),
        ],
        out_specs=[
            pl.BlockSpec((TOK_TILE, N_RWKV_COLS), lambda i: (i, 0)),
            pl.BlockSpec((TOK_TILE, ATTN_WIDTH), lambda i: (i, 0)),
            pl.BlockSpec((TOK_TILE, ATTN_WIDTH), lambda i: (i, 0)),
            pl.BlockSpec((1, ATTN_WIDTH, TOK_TILE), lambda i: (i // per_b, 0, i % per_b)),
            pl.BlockSpec((1, 1, ATTN_WIDTH), lambda i: (i, 0, 0)),
            pl.BlockSpec((TOK_TILE, D_MIX), lambda i: (i, 0)),
        ],
        out_shape=[
            jax.ShapeDtypeStruct((m, N_RWKV_COLS), F32),
            jax.ShapeDtypeStruct((m, ATTN_WIDTH), F32),
            jax.ShapeDtypeStruct((m, ATTN_WIDTH), BF16),
            jax.ShapeDtypeStruct((batch, ATTN_WIDTH, seq), BF16),
            jax.ShapeDtypeStruct((nblk, 1, ATTN_WIDTH), F32),
            jax.ShapeDtypeStruct((m, D_MIX), BF16),
        ],
        compiler_params=pltpu.CompilerParams(
            dimension_semantics=("arbitrary",), vmem_limit_bytes=VMEM_LIMIT),
        name="in_proj",
    )(x2, g_pre, w_r, w_a, w_g)


def _split2(x):
    hi = x.astype(BF16)
    lo = (x - hi.astype(F32)).astype(BF16)
    return hi, lo


def _split_dot(x, ones_b):
    hi, lo = _split2(x)
    return _dot(hi, ones_b) + _dot(lo, ones_b)


def _head_sum(x, ones_b, split=True):
    one = _split_dot if split else (lambda xs, o: _dot(xs.astype(BF16), o))
    parts = [one(x[:, g * GROUP_W:(g + 1) * GROUP_W], ones_b) for g in range(x.shape[1] // GROUP_W)]
    return jnp.concatenate(parts, axis=1)


def _rwkv_body(pr_ref, gs_ref, mu_ref, w0_ref, w2a_hi_ref, w2a_lo_ref, a0_ref, kk_ref, ka_ref, rk_ref, lw_ref,
               lb_ref, o_ref, s_ref, prev_ref, at_ref, bt_ref, kt_ref, rt_ref, v_ref, gl_ref, y_ref,
               q_ref, g_ref, h_ref):
    W = RWKV_WIDTH
    n_groups = W // GROUP_W
    n_chunks = TOK_TILE // CHUNK

    @pl.when(pl.program_id(1) == 0)
    def _():
        s_ref[...] = jnp.zeros_like(s_ref)
        prev_ref[...] = jnp.zeros_like(prev_ref)

    rr = lax.broadcasted_iota(jnp.int32, (GROUP_W, GROUP_W), 0)
    cc = lax.broadcasted_iota(jnp.int32, (GROUP_W, GROUP_W), 1)
    same_head = (rr // HEAD_DIM) == (cc // HEAD_DIM)
    ones_b = same_head.astype(BF16)
    eye_g = rr == cc
    pr_ = lax.broadcasted_iota(jnp.int32, (2 * CHUNK, 4 * CHUNK), 0)
    pc_ = lax.broadcasted_iota(jnp.int32, (2 * CHUNK, 4 * CHUNK), 1)
    tril_pair = ((pr_ // CHUNK == pc_ // (2 * CHUNK)) & (pc_ % CHUNK <= pr_ % CHUNK)).astype(BF16)
    wr = lax.broadcasted_iota(jnp.int32, (CHUNK, GROUP_W), 0)
    wc = lax.broadcasted_iota(jnp.int32, (CHUNK, GROUP_W), 1) % CHUNK
    strict_w = wc < wr
    incl_w = wc <= wr
    eye_w = (wc == wr).astype(F32)
    lane_g = lax.broadcasted_iota(jnp.int32, (1, GROUP_W), 1) // HEAD_DIM
    head_masks = [lane_g == h for h in range(GROUP_HEADS)]

    def mstack(x):
        xb = x.astype(BF16)
        zero = jnp.zeros_like(xb)
        return jnp.concatenate([jnp.where(hm, xb, zero) for hm in head_masks], axis=0)

    def block_diag(xw):
        return jnp.where(same_head, jnp.concatenate([xw] * GROUP_HEADS, axis=0), jnp.zeros((), BF16))

    p = pr_ref[...]
    rolled = pltpu.roll(p, 1, axis=0)
    row0 = lax.broadcasted_iota(jnp.int32, (TOK_TILE, 1), 0) == 0
    p_prev = jnp.where(row0, prev_ref[...], rolled)
    prev_ref[...] = p[TOK_TILE - 1:TOK_TILE, :]
    p = p + (p_prev - p) * mu_ref[...]
    r = p[:, 0:W]
    k = p[:, W:2 * W]
    v = p[:, 2 * W:3 * W]
    z = p[:, 3 * W:3 * W + DECAY_RANK + ICLR_RANK]
    lane_z = lax.broadcasted_iota(jnp.int32, (1, DECAY_RANK + ICLR_RANK), 1)
    z = jnp.where(lane_z < DECAY_RANK, jnp.tanh(z), z)
    z_hi, z_lo = _split2(z)
    lin = _dot(jnp.concatenate([z_hi, z_lo], axis=1), w2a_hi_ref[...]) + _dot(z_hi, w2a_lo_ref[...])
    u = -(w0_ref[...] + lin[:, :W])
    softplus = jnp.maximum(u, 0.0) + jnp.log1p(jnp.exp(-jnp.abs(u)))
    w = -softplus - 0.5
    ld = -jnp.exp(w)
    a = jax.nn.sigmoid(a0_ref[...] + lin[:, W:])
    kkr = k * kk_ref[...]
    nrm = jnp.sqrt(_head_sum(kkr * kkr, ones_b, split=False))
    kk = kkr / jnp.maximum(nrm, 1e-12)
    k2 = k * (1.0 + (a - 1.0) * ka_ref[...])
    ld_hi, ld_lo = _split2(ld)
    cs_parts = []
    for pair in range(n_chunks // 2):
        pieces = []
        for c in (2 * pair, 2 * pair + 1):
            pieces += [ld_hi[c * CHUNK:(c + 1) * CHUNK], ld_lo[c * CHUNK:(c + 1) * CHUNK]]
        cs_parts.append(_dot(tril_pair, jnp.concatenate(pieces, axis=0)))
    cs = jnp.concatenate(cs_parts, axis=0)
    g_incl = jnp.exp(cs)
    g_inv = jnp.exp(-cs)
    at_ref[...] = (-kk) * jnp.exp(cs - ld)
    bt_ref[...] = kk * a * g_inv
    kt_ref[...] = k2 * g_inv
    rt_ref[...] = r * g_incl
    v_ref[...] = v
    gl_ref[...] = g_incl
    bonus = _head_sum(r * k2 * rk_ref[...], ones_b) * v

    chains = [(c, g) for c in range(n_chunks) for g in range(n_groups)]
    for first in range(0, len(chains), RWKV_LOCKSTEP):
        batch_chains = chains[first:first + RWKV_LOCKSTEP]
        nc = range(len(batch_chains))
        rows = [slice(c * CHUNK, (c + 1) * CHUNK) for c, _ in batch_chains]
        cols = [slice(g * GROUP_W, (g + 1) * GROUP_W) for _, g in batch_chains]
        x = []
        for j in nc:
            lhs = jnp.concatenate([at_ref[rows[j], cols[j]], rt_ref[rows[j], cols[j]]], axis=0).astype(BF16)
            rhs = jnp.concatenate([mstack(bt_ref[rows[j], cols[j]]), mstack(kt_ref[rows[j], cols[j]])], axis=0)
            x.append(_dot(lhs, rhs, _NT))
        a_ab = [jnp.where(strict_w, x[j][:CHUNK, :GROUP_W], 0.0) for j in nc]
        a_ak = [jnp.where(strict_w, x[j][:CHUNK, GROUP_W:], 0.0).astype(BF16) for j in nc]
        m_rb = [jnp.where(incl_w, x[j][CHUNK:, :GROUP_W], 0.0).astype(BF16) for j in nc]
        m_rk = [jnp.where(incl_w, x[j][CHUNK:, GROUP_W:], 0.0).astype(BF16) for j in nc]
        tw = [a_ab[j] + eye_w for j in nc]
        pw = [a_ab[j].astype(BF16) for j in nc]
        pw = [_dot(pw[j], block_diag(pw[j])).astype(BF16) for j in nc]
        for it in range(CHUNK.bit_length() - 3):
            both = [_dot(jnp.concatenate([pw[j], tw[j].astype(BF16)], axis=0), block_diag(pw[j])) for j in nc]
            tw = [tw[j] + both[j][CHUNK:] for j in nc]
            pw = [both[j][:CHUNK].astype(BF16) for j in nc]
        tw = [tw[j] + _dot(tw[j].astype(BF16), block_diag(pw[j])) for j in nc]
        twb = [tw[j].astype(BF16) for j in nc]
        v_m = [mstack(v_ref[rows[j], cols[j]]) for j in nc]
        wt = [_dot(twb[j], mstack(at_ref[rows[j], cols[j]])) for j in nc]
        x1 = [_dot(a_ak[j], v_m[j]) for j in nc]
        u0 = [_dot(twb[j], mstack(x1[j])) for j in nc]
        for j in nc:
            q_ref[rows[j], cols[j]] = (rt_ref[rows[j], cols[j]] + _dot(m_rb[j], mstack(wt[j]))).astype(BF16)
        for j in nc:
            y_ref[rows[j], cols[j]] = _dot(m_rb[j], mstack(u0[j])) + _dot(m_rk[j], v_m[j])
        for j in nc:
            c, g = batch_chains[j]
            gl = gl_ref[(c + 1) * CHUNK - 1:(c + 1) * CHUNK, cols[j]]
            bh = (bt_ref[rows[j], cols[j]] * gl).astype(BF16)
            kh = (kt_ref[rows[j], cols[j]] * gl).astype(BF16)
            gm = jnp.where(eye_g, gl, 0.0) + jnp.where(same_head, _dot(bh, wt[j].astype(BF16), _TN), 0.0)
            g_ref[c * n_groups + g] = gm.astype(BF16)
            h_ref[c * n_groups + g] = jnp.where(
                same_head,
                _dot(jnp.concatenate([bh, kh], axis=0),
                     jnp.concatenate([u0[j], v_ref[rows[j], cols[j]]], axis=0).astype(BF16), _TN),
                0.0)

    for c in range(n_chunks):
        rows = slice(c * CHUNK, (c + 1) * CHUNK)
        for g in range(n_groups):
            cols = slice(g * GROUP_W, (g + 1) * GROUP_W)
            s_b = s_ref[g].astype(BF16)
            y_ref[rows, cols] = y_ref[rows, cols] + _dot(q_ref[rows, cols], s_b)
            s_ref[g] = _dot(g_ref[c * n_groups + g], s_b) + h_ref[c * n_groups + g]

    y = y_ref[...]
    mean = _head_sum(y, ones_b) * (1.0 / HEAD_DIM)
    yc = y - mean
    var = _head_sum(yc * yc, ones_b) * (1.0 / HEAD_DIM)
    yn = yc * lax.rsqrt(var + GN_EPS) * lw_ref[...] + lb_ref[...] + bonus
    o_ref[...] = (yn * gs_ref[...].astype(F32)).astype(BF16)


def _rwkv(pr, gs, mu, w0, w2a, a0, k_k, k_a, r_k, lnx_w, lnx_b, batch, seq):
    m = pr.shape[0]
    per_b = seq // TOK_TILE
    W = RWKV_WIDTH
    n_cg = (TOK_TILE // CHUNK) * (W // GROUP_W)
    w2a_hi = w2a.astype(BF16)
    w2a_lo = (w2a - w2a_hi.astype(F32)).astype(BF16)
    w2a_hi = jnp.concatenate([w2a_hi, w2a_hi], axis=0)
    tile = lambda b, t: (b * per_b + t, 0)
    const = lambda b, t: (0, 0)
    vec = pl.BlockSpec((1, W), const)
    tile_scratch = pltpu.VMEM((TOK_TILE, W), F32)
    return pl.pallas_call(
        _rwkv_body,
        grid=(batch, per_b),
        in_specs=[
            pl.BlockSpec((TOK_TILE, N_RWKV_COLS), tile),
            pl.BlockSpec((TOK_TILE, W), tile),
            pl.BlockSpec((1, N_RWKV_COLS), const),
            vec,
            pl.BlockSpec(w2a_hi.shape, const),
            pl.BlockSpec(w2a_lo.shape, const),
            vec, vec, vec, vec, vec, vec,
        ],
        out_specs=pl.BlockSpec((TOK_TILE, W), tile),
        out_shape=jax.ShapeDtypeStruct((m, W), BF16),
        scratch_shapes=[
            pltpu.VMEM((W // GROUP_W, GROUP_W, GROUP_W), F32),
            pltpu.VMEM((1, N_RWKV_COLS), F32),
            tile_scratch, tile_scratch, tile_scratch, tile_scratch, tile_scratch, tile_scratch, tile_scratch,
            pltpu.VMEM((TOK_TILE, W), BF16),
            pltpu.VMEM((n_cg, GROUP_W, GROUP_W), BF16),
            pltpu.VMEM((n_cg, GROUP_W, GROUP_W), F32),
        ],
        compiler_params=pltpu.CompilerParams(
            dimension_semantics=("arbitrary", "arbitrary"), vmem_limit_bytes=VMEM_LIMIT),
        name="rwkv7",
    )(pr, gs, mu, w0, w2a_hi, w2a_lo, a0, k_k, k_a, r_k, lnx_w, lnx_b)


LOG2E = 1.4426950408889634
AUX_S, AUX_T, AUX_BLK, AUX_SEL = 0, 3, 6, 16
PV_ONES_ROWS = 16
MOBA_STEP_HEADS = 4


def _moba_key_aux(seq):
    pos = jnp.arange(seq, dtype=jnp.int32)[:, None]
    lane = jnp.arange(LANES, dtype=jnp.int32)[None, :]
    blk = pos // MOBA_BLOCK
    tab = jnp.where(lane < AUX_T, pos % MOBA_BLOCK,
                    jnp.where(lane < AUX_BLK, 1,
                              jnp.where(lane < AUX_BLK + 3, blk * MOBA_BLOCK,
                                        jnp.where(lane == AUX_SEL + blk, 1, 0))))
    return tab.astype(BF16)


def _split3(x):
    x1 = x.astype(BF16).astype(F32)
    r1 = x - x1
    x2 = r1.astype(BF16).astype(F32)
    x3 = (r1 - x2).astype(BF16).astype(F32)
    return x1, x2, x3


def _moba_body(slopes_ref, q_ref, k_ref, ak_ref, vt_ref, km_ref, gs_ref, o_ref, st_ref, p_ref):
    BS = MOBA_BLOCK
    grp = pl.program_id(1)
    i = pl.program_id(2)
    heads = MOBA_STEP_HEADS
    per_pair = LANES // HEAD_DIM
    nb = km_ref.shape[1]
    lane_h = lax.broadcasted_iota(jnp.int32, (1, LANES), 1) // HEAD_DIM
    n_iota = lax.broadcasted_iota(jnp.int32, (nb, BS), 0)
    causal = (lax.broadcasted_iota(jnp.int32, (BS, BS), 0)
              <= lax.broadcasted_iota(jnp.int32, (BS, BS), 1))
    t_glob = (lax.broadcasted_iota(jnp.int32, (nb, BS), 1) + i * BS).astype(F32)
    piece = n_iota % 3
    ones_rows = jnp.ones((PV_ONES_ROWS, BS), BF16)

    def key_rows(n):
        b = jnp.where(n == 0, i, n - 1)
        return pl.ds(pl.multiple_of(b * BS, BS), BS)

    def pair_lanes(h):
        p = h // per_pair
        return slice(p * LANES, (p + 1) * LANES)

    def scores(h, blk, q_aug_t):
        k_aug = jnp.concatenate([k_ref[blk, pair_lanes(h)], ak_ref[blk, :]], axis=1)
        return _dot(k_aug, q_aug_t)

    def pv(h, blk):
        vt_aug = jnp.concatenate([vt_ref[0, h * HEAD_DIM:(h + 1) * HEAD_DIM, blk], ones_rows], axis=0)
        return _dot(vt_aug, p_ref[h])

    q = q_ref[...]
    km = km_ref[0]
    lane_head = lax.broadcasted_iota(jnp.int32, (1, heads * HEAD_DIM), 1) // HEAD_DIM
    km_stack = jnp.concatenate([jnp.where(lane_head == h, km, 0.0) for h in range(heads)], axis=0)
    gates = _dot(km_stack, q, _NT, precision=lax.Precision.HIGHEST)
    gate = [jnp.where(n_iota < i, gates[h * nb:(h + 1) * nb], NEG) for h in range(heads)]
    sel = [jnp.zeros((nb, BS), F32) for _ in range(heads)]
    for rnk in range(MOBA_TOPK):
        rank_ok = jnp.where(rnk < i, 1.0, 0.0)
        for h in range(heads):
            mx = jnp.max(gate[h], axis=0, keepdims=True)
            idx = jnp.min(jnp.where(gate[h] == mx, n_iota, nb), axis=0, keepdims=True)
            pick = n_iota == idx
            sel[h] = jnp.where(pick, rank_ok, sel[h])
            gate[h] = jnp.where(pick, -jnp.inf, gate[h])
    q_t = (q * LOG2E).T
    row_h = lax.broadcasted_iota(jnp.int32, (LANES, 1), 0) // HEAD_DIM
    q_augs = []
    for h in range(heads):
        keep = (sel[h] > 0.0) | (n_iota == i)
        cvec = jnp.full((nb, BS), slopes_ref[grp * heads + h] * LOG2E, F32)
        c1, c2, c3 = _split3(cvec)
        e1, e2, e3 = _split3(-cvec * t_glob)
        cp = jnp.where(piece == 0, c1, jnp.where(piece == 1, c2, c3))
        ep = jnp.where(piece == 0, e1, jnp.where(piece == 1, e2, e3))
        top = jnp.where((n_iota >= AUX_T) & (n_iota < AUX_BLK), ep, cp)
        top = jnp.where(n_iota < AUX_BLK + 3, top, 0.0)
        aux_t = jnp.concatenate(
            [top, jnp.where(keep, 0.0, NEG), jnp.zeros((LANES - AUX_SEL - nb, BS), F32)], axis=0)
        qh_t = jnp.where(row_h == h % per_pair, q_t[pair_lanes(h), :], 0.0)
        q_augs.append(jnp.concatenate([qh_t, aux_t], axis=0).astype(BF16))
    for h in range(heads):
        st_ref[h] = jnp.where(causal, scores(h, key_rows(0), q_augs[h]), NEG)
        p_ref[h] = jnp.zeros((BS, BS), BF16)

    def kv_step(n, carry):
        ms, accs, alphas = carry
        prev_rows = key_rows(jnp.maximum(n - 1, 0))
        next_rows = key_rows(n + 1)
        ms2, accs2, alphas2 = [], [], []
        for h in range(heads):
            accs2.append(alphas[h] * accs[h] + pv(h, prev_rows))
            st = st_ref[h]
            m_new = jnp.maximum(ms[h], jnp.max(st, axis=0, keepdims=True))
            alphas2.append(jnp.exp2(ms[h] - m_new))
            p_ref[h] = jnp.exp2(st - m_new).astype(BF16)
            st_ref[h] = scores(h, next_rows, q_augs[h])
            ms2.append(m_new)
        return tuple(ms2), tuple(accs2), tuple(alphas2)

    m0 = tuple(jnp.full((1, BS), NEG, F32) for _ in range(heads))
    acc0 = tuple(jnp.zeros((HEAD_DIM + PV_ONES_ROWS, BS), F32) for _ in range(heads))
    alpha0 = tuple(jnp.ones((1, BS), F32) for _ in range(heads))
    ms, accs, alphas = lax.fori_loop(0, i + 1, kv_step, (m0, acc0, alpha0))
    last_rows = key_rows(i)
    outs = []
    for h in range(heads):
        acc = alphas[h] * accs[h] + pv(h, last_rows)
        outs.append(acc[:HEAD_DIM] / acc[HEAD_DIM:HEAD_DIM + 1])
    ot = jnp.concatenate(outs, axis=0)
    o_ref[...] = (ot.T * gs_ref[...].astype(F32)).astype(BF16)


def _moba(slopes, q, k, key_aux, vt, kmean, gs, batch, seq):
    m = q.shape[0]
    BS = MOBA_BLOCK
    nb = seq // BS
    assert nb == AUX_SEL and AUX_SEL + nb <= LANES and AUX_BLK + 3 <= AUX_SEL
    W = MOBA_STEP_HEADS * HEAD_DIM
    n_groups = ATTN_WIDTH // W
    gate_col0 = RWKV_WIDTH // W
    return pl.pallas_call(
        _moba_body,
        grid_spec=pltpu.PrefetchScalarGridSpec(
            num_scalar_prefetch=1,
            grid=(batch, n_groups, nb),
            in_specs=[
                pl.BlockSpec((BS, W), lambda b, g, i, s: (b * nb + i, g)),
                pl.BlockSpec((seq, W), lambda b, g, i, s: (b, g)),
                pl.BlockSpec((seq, LANES), lambda b, g, i, s: (0, 0)),
                pl.BlockSpec((1, W, seq), lambda b, g, i, s: (b, g, 0)),
                pl.BlockSpec((1, nb, W), lambda b, g, i, s: (b, 0, g)),
                pl.BlockSpec((BS, W), lambda b, g, i, s: (b * nb + i, gate_col0 + g)),
            ],
            out_specs=pl.BlockSpec((BS, W), lambda b, g, i, s: (b * nb + i, g)),
            scratch_shapes=[
                pltpu.VMEM((MOBA_STEP_HEADS, BS, BS), F32),
                pltpu.VMEM((MOBA_STEP_HEADS, BS, BS), BF16),
            ],
        ),
        out_shape=jax.ShapeDtypeStruct((m, ATTN_WIDTH), BF16),
        compiler_params=pltpu.CompilerParams(
            dimension_semantics=("arbitrary", "arbitrary", "arbitrary"), vmem_limit_bytes=VMEM_LIMIT),
        name="moba",
    )(slopes, q, k, key_aux, vt, kmean, gs)


def _outproj_body(x_ref, yr_ref, ya_ref, wr_ref, wa_ref, g_ref, o_ref):
    o = _dot(yr_ref[...], wr_ref[...]) + _dot(ya_ref[...], wa_ref[...])
    ms = jnp.mean(o * o, axis=-1, keepdims=True)
    o_ref[...] = x_ref[...] + o * lax.rsqrt(ms + RMS_EPS) * g_ref[...]


def _out_proj(x2, yr, ya, w_or, w_oa, g_post):
    m, d = x2.shape
    tm = 2 * TOK_TILE
    const = lambda i: (0, 0)
    return pl.pallas_call(
        _outproj_body,
        grid=(m // tm,),
        in_specs=[
            pl.BlockSpec((tm, d), lambda i: (i, 0)),
            pl.BlockSpec((tm, RWKV_WIDTH), lambda i: (i, 0)),
            pl.BlockSpec((tm, ATTN_WIDTH), lambda i: (i, 0)),
            pl.BlockSpec(w_or.shape, const),
            pl.BlockSpec(w_oa.shape, const),
            pl.BlockSpec((1, d), const),
        ],
        out_specs=pl.BlockSpec((tm, d), lambda i: (i, 0)),
        out_shape=jax.ShapeDtypeStruct((m, d), F32),
        compiler_params=pltpu.CompilerParams(
            dimension_semantics=("arbitrary",), vmem_limit_bytes=VMEM_LIMIT),
        name="out_proj",
    )(x2, yr, ya, w_or, w_oa, g_post)


def kernel(x, g_pre, w_in, tshift_mu, w0, w2, a0, a2, k_k, k_a, r_k, lnx_w, lnx_b, w_out, g_post):
    batch, seq, d = x.shape
    depth = w_in.shape[0]
    assert seq % TOK_TILE == 0 and TOK_TILE == MOBA_BLOCK and d % LANES == 0
    x2 = x.reshape(batch * seq, d)
    slopes = 2.0 ** (-8.0 * jnp.arange(1, ATTN_HEADS + 1, dtype=F32) / ATTN_HEADS)
    row = lambda z: z.reshape(1, -1).astype(F32)
    for l in range(depth):
        wb = w_in[l].astype(BF16)
        w_r = wb[:, :N_RWKV_COLS]
        w_a = wb[:, N_RWKV_COLS:N_RWKV_COLS + N_ATTN_COLS]
        w_g = wb[:, N_RWKV_COLS + N_ATTN_COLS:]
        pr, q, k, vt, kmean, gs = _in_proj(x2, row(g_pre[l]), w_r, w_a, w_g, batch, seq)
        zeros = jnp.zeros((DECAY_RANK, RWKV_WIDTH), F32)
        w2a = jnp.concatenate([jnp.concatenate([w2[l], zeros], axis=1),
                               jnp.concatenate([zeros, a2[l]], axis=1)], axis=0)
        yr = _rwkv(pr, gs, row(tshift_mu[l]), row(w0[l]), w2a, row(a0[l]), row(k_k[l]), row(k_a[l]),
                   row(r_k[l]), row(lnx_w[l]), row(lnx_b[l]), batch, seq)
        ya = _moba(slopes, q, k, _moba_key_aux(seq), vt,
                   kmean.reshape(batch, seq // MOBA_BLOCK, ATTN_WIDTH), gs, batch, seq)
        wo = w_out[l].astype(BF16)
        x2 = _out_proj(x2, yr, ya, wo[:RWKV_WIDTH], wo[RWKV_WIDTH:], row(g_post[l]))
    return x2.reshape(batch, seq, d)
```

```python
import functools

import jax
import jax.numpy as jnp
from jax import lax
from jax.experimental import pallas as pl
from jax.experimental.pallas import tpu as pltpu

F32 = jnp.float32
BF16 = jnp.bfloat16

HEAD_DIM = 64
RWKV_HEADS = 8
ATTN_HEADS = 8
RWKV_WIDTH = RWKV_HEADS * HEAD_DIM
ATTN_WIDTH = ATTN_HEADS * HEAD_DIM
D_MIX = RWKV_WIDTH + ATTN_WIDTH
DECAY_RANK = 64
ICLR_RANK = 64
MOBA_BLOCK = 256
MOBA_TOPK = 3
RMS_EPS = 1e-6
GN_EPS = 64e-5
NEG = -1e30
N_RWKV_COLS = 3 * RWKV_WIDTH + DECAY_RANK + ICLR_RANK
N_ATTN_COLS = 3 * ATTN_WIDTH

LANES = 128
MXU_DIM = 256
CHUNK = 64
GROUP_HEADS = MXU_DIM // HEAD_DIM
GROUP_W = GROUP_HEADS * HEAD_DIM
TOK_TILE = 256
RWKV_LOCKSTEP = 8
VMEM_LIMIT = 48 * 1024 * 1024

_NT = (((1,), (1,)), ((), ()))
_TN = (((0,), (0,)), ((), ()))


def _dot(a, b, dims=None, precision=None):
    if dims is None:
        return jnp.dot(a, b, preferred_element_type=F32, precision=precision)
    return lax.dot_general(a, b, dims, preferred_element_type=F32, precision=precision)


def _inproj_body(x_ref, g_ref, wr_ref, wa_ref, wg_ref, pr_ref, q_ref, k_ref, vt_ref, km_ref, gs_ref):
    x = x_ref[...]
    ms = jnp.mean(x * x, axis=-1, keepdims=True)
    h = (x * lax.rsqrt(ms + RMS_EPS) * g_ref[...]).astype(BF16)
    pr_ref[...] = _dot(h, wr_ref[...])
    qkv = _dot(h, wa_ref[...])
    q_ref[...] = qkv[:, :ATTN_WIDTH] * (HEAD_DIM ** -0.5)
    kf = qkv[:, ATTN_WIDTH:2 * ATTN_WIDTH]
    k_ref[...] = kf.astype(BF16)
    km_ref[0] = jnp.mean(kf, axis=0, keepdims=True)
    vt_ref[0] = qkv[:, 2 * ATTN_WIDTH:].T.astype(BF16)
    g = _dot(h, wg_ref[...])
    gs_ref[...] = (g * jax.nn.sigmoid(g)).astype(BF16)


def _in_proj(x2, g_pre, w_r, w_a, w_g, batch, seq):
    m, d = x2.shape
    nblk = m // TOK_TILE
    per_b = seq // TOK_TILE
    const = lambda i: (0, 0)
    return pl.pallas_call(
        _inproj_body,
        grid=(nblk,),
        in_specs=[
            pl.BlockSpec((TOK_TILE, d), lambda i: (i, 0)),
            pl.BlockSpec((1, d), const),
            pl.BlockSpec(w_r.shape, const),
            pl.BlockSpec(w_a.shape, const),
            pl.BlockSpec(w_g.shape, const),
        ],
        out_specs=[
            pl.BlockSpec((TOK_TILE, N_RWKV_COLS), lambda i: (i, 0)),
            pl.BlockSpec((TOK_TILE, ATTN_WIDTH), lambda i: (i, 0)),
            pl.BlockSpec((TOK_TILE, ATTN_WIDTH), lambda i: (i, 0)),
            pl.BlockSpec((1, ATTN_WIDTH, TOK_TILE), lambda i: (i // per_b, 0, i % per_b)),
            pl.BlockSpec((1, 1, ATTN_WIDTH), lambda i: (i, 0, 0)),
            pl.BlockSpec((TOK_TILE, D_MIX), lambda i: (i, 0)),
        ],
        out_shape=[
            jax.ShapeDtypeStruct((m, N_RWKV_COLS), F32),
            jax.ShapeDtypeStruct((m, ATTN_WIDTH), F32),
            jax.ShapeDtypeStruct((m, ATTN_WIDTH), BF16),
            jax.ShapeDtypeStruct((batch, ATTN_WIDTH, seq), BF16),
            jax.ShapeDtypeStruct((nblk, 1, ATTN_WIDTH), F32),
            jax.ShapeDtypeStruct((m, D_MIX), BF16),
        ],
        compiler_params=pltpu.CompilerParams(
            dimension_semantics=("arbitrary",), vmem_limit_bytes=VMEM_LIMIT),
        name="in_proj",
    )(x2, g_pre, w_r, w_a, w_g)


def _split2(x):
    hi = x.astype(BF16)
    lo = (x - hi.astype(F32)).astype(BF16)
    return hi, lo


def _split_dot(x, ones_b):
    hi, lo = _split2(x)
    return _dot(hi, ones_b) + _dot(lo, ones_b)


def _head_sum(x, ones_b, split=True):
    one = _split_dot if split else (lambda xs, o: _dot(xs.astype(BF16), o))
    parts = [one(x[:, g * GROUP_W:(g + 1) * GROUP_W], ones_b) for g in range(x.shape[1] // GROUP_W)]
    return jnp.concatenate(parts, axis=1)


def _rwkv_body(pr_ref, gs_ref, mu_ref, w0_ref, w2a_hi_ref, w2a_lo_ref, a0_ref, kk_ref, ka_ref, rk_ref, lw_ref,
               lb_ref, o_ref, s_ref, prev_ref, at_ref, bt_ref, kt_ref, rt_ref, v_ref, gl_ref, y_ref,
               q_ref, g_ref, h_ref):
    W = RWKV_WIDTH
    n_groups = W // GROUP_W
    n_chunks = TOK_TILE // CHUNK

    @pl.when(pl.program_id(1) == 0)
    def _():
        s_ref[...] = jnp.zeros_like(s_ref)
        prev_ref[...] = jnp.zeros_like(prev_ref)

    rr = lax.broadcasted_iota(jnp.int32, (GROUP_W, GROUP_W), 0)
    cc = lax.broadcasted_iota(jnp.int32, (GROUP_W, GROUP_W), 1)
    same_head = (rr // HEAD_DIM) == (cc // HEAD_DIM)
    ones_b = same_head.astype(BF16)
    eye_g = rr == cc
    pr_ = lax.broadcasted_iota(jnp.int32, (2 * CHUNK, 4 * CHUNK), 0)
    pc_ = lax.broadcasted_iota(jnp.int32, (2 * CHUNK, 4 * CHUNK), 1)
    tril_pair = ((pr_ // CHUNK == pc_ // (2 * CHUNK)) & (pc_ % CHUNK <= pr_ % CHUNK)).astype(BF16)
    wr = lax.broadcasted_iota(jnp.int32, (CHUNK, GROUP_W), 0)
    wc = lax.broadcasted_iota(jnp.int32, (CHUNK, GROUP_W), 1) % CHUNK
    strict_w = wc < wr
    incl_w = wc <= wr
    eye_w = (wc == wr).astype(F32)
    lane_g = lax.broadcasted_iota(jnp.int32, (1, GROUP_W), 1) // HEAD_DIM
    head_masks = [lane_g == h for h in range(GROUP_HEADS)]

    def mstack(x):
        xb = x.astype(BF16)
        zero = jnp.zeros_like(xb)
        return jnp.concatenate([jnp.where(hm, xb, zero) for hm in head_masks], axis=0)

    def block_diag(xw):
        return jnp.where(same_head, jnp.concatenate([xw] * GROUP_HEADS, axis=0), jnp.zeros((), BF16))

    p = pr_ref[...]
    rolled = pltpu.roll(p, 1, axis=0)
    row0 = lax.broadcasted_iota(jnp.int32, (TOK_TILE, 1), 0) == 0
    p_prev = jnp.where(row0, prev_ref[...], rolled)
    prev_ref[...] = p[TOK_TILE - 1:TOK_TILE, :]
    p = p + (p_prev - p) * mu_ref[...]
    r = p[:, 0:W]
    k = p[:, W:2 * W]
    v = p[:, 2 * W:3 * W]
    z = p[:, 3 * W:3 * W + DECAY_RANK + ICLR_RANK]
    lane_z = lax.broadcasted_iota(jnp.int32, (1, DECAY_RANK + ICLR_RANK), 1)
    z = jnp.where(lane_z < DECAY_RANK, jnp.tanh(z), z)
    z_hi, z_lo = _split2(z)
    lin = _dot(jnp.concatenate([z_hi, z_lo], axis=1), w2a_hi_ref[...]) + _dot(z_hi, w2a_lo_ref[...])
    u = -(w0_ref[...] + lin[:, :W])
    softplus = jnp.maximum(u, 0.0) + jnp.log1p(jnp.exp(-jnp.abs(u)))
    w = -softplus - 0.5
    ld = -jnp.exp(w)
    a = jax.nn.sigmoid(a0_ref[...] + lin[:, W:])
    kkr = k * kk_ref[...]
    nrm = jnp.sqrt(_head_sum(kkr * kkr, ones_b, split=False))
    kk = kkr / jnp.maximum(nrm, 1e-12)
    k2 = k * (1.0 + (a - 1.0) * ka_ref[...])
    ld_hi, ld_lo = _split2(ld)
    cs_parts = []
    for pair in range(n_chunks // 2):
        pieces = []
        for c in (2 * pair, 2 * pair + 1):
            pieces += [ld_hi[c * CHUNK:(c + 1) * CHUNK], ld_lo[c * CHUNK:(c + 1) * CHUNK]]
        cs_parts.append(_dot(tril_pair, jnp.concatenate(pieces, axis=0)))
    cs = jnp.concatenate(cs_parts, axis=0)
    g_incl = jnp.exp(cs)
    g_inv = jnp.exp(-cs)
    at_ref[...] = (-kk) * jnp.exp(cs - ld)
    bt_ref[...] = kk * a * g_inv
    kt_ref[...] = k2 * g_inv
    rt_ref[...] = r * g_incl
    v_ref[...] = v
    gl_ref[...] = g_incl
    bonus = _head_sum(r * k2 * rk_ref[...], ones_b) * v

    chains = [(c, g) for c in range(n_chunks) for g in range(n_groups)]
    for first in range(0, len(chains), RWKV_LOCKSTEP):
        batch_chains = chains[first:first + RWKV_LOCKSTEP]
        nc = range(len(batch_chains))
        rows = [slice(c * CHUNK, (c + 1) * CHUNK) for c, _ in batch_chains]
        cols = [slice(g * GROUP_W, (g + 1) * GROUP_W) for _, g in batch_chains]
        x = []
        for j in nc:
            lhs = jnp.concatenate([at_ref[rows[j], cols[j]], rt_ref[rows[j], cols[j]]], axis=0).astype(BF16)
            rhs = jnp.concatenate([mstack(bt_ref[rows[j], cols[j]]), mstack(kt_ref[rows[j], cols[j]])], axis=0)
            x.append(_dot(lhs, rhs, _NT))
        a_ab = [jnp.where(strict_w, x[j][:CHUNK, :GROUP_W], 0.0) for j in nc]
        a_ak = [jnp.where(strict_w, x[j][:CHUNK, GROUP_W:], 0.0).astype(BF16) for j in nc]
        m_rb = [jnp.where(incl_w, x[j][CHUNK:, :GROUP_W], 0.0).astype(BF16) for j in nc]
        m_rk = [jnp.where(incl_w, x[j][CHUNK:, GROUP_W:], 0.0).astype(BF16) for j in nc]
        tw = [a_ab[j] + eye_w for j in nc]
        pw = [a_ab[j].astype(BF16) for j in nc]
        pw = [_dot(pw[j], block_diag(pw[j])).astype(BF16) for j in nc]
        for it in range(CHUNK.bit_length() - 3):
            both = [_dot(jnp.concatenate([pw[j], tw[j].astype(BF16)], axis=0), block_diag(pw[j])) for j in nc]
            tw = [tw[j] + both[j][CHUNK:] for j in nc]
            pw = [both[j][:CHUNK].astype(BF16) for j in nc]
        tw = [tw[j] + _dot(tw[j].astype(BF16), block_diag(pw[j])) for j in nc]
        twb = [tw[j].astype(BF16) for j in nc]
        v_m = [mstack(v_ref[rows[j], cols[j]]) for j in nc]
        wt = [_dot(twb[j], mstack(at_ref[rows[j], cols[j]])) for j in nc]
        x1 = [_dot(a_ak[j], v_m[j]) for j in nc]
        u0 = [_dot(twb[j], mstack(x1[j])) for j in nc]
        for j in nc:
            q_ref[rows[j], cols[j]] = (rt_ref[rows[j], cols[j]] + _dot(m_rb[j], mstack(wt[j]))).astype(BF16)
        for j in nc:
            y_ref[rows[j], cols[j]] = _dot(m_rb[j], mstack(u0[j])) + _dot(m_rk[j], v_m[j])
        for j in nc:
            c, g = batch_chains[j]
            gl = gl_ref[(c + 1) * CHUNK - 1:(c + 1) * CHUNK, cols[j]]
            bh = (bt_ref[rows[j], cols[j]] * gl).astype(BF16)
            kh = (kt_ref[rows[j], cols[j]] * gl).astype(BF16)
            gm = jnp.where(eye_g, gl, 0.0) + jnp.where(same_head, _dot(bh, wt[j].astype(BF16), _TN), 0.0)
            g_ref[c * n_groups + g] = gm.astype(BF16)
            h_ref[c * n_groups + g] = jnp.where(
                same_head,
                _dot(jnp.concatenate([bh, kh], axis=0),
                     jnp.concatenate([u0[j], v_ref[rows[j], cols[j]]], axis=0).astype(BF16), _TN),
                0.0)

    for c in range(n_chunks):
        rows = slice(c * CHUNK, (c + 1) * CHUNK)
        for g in range(n_groups):
            cols = slice(g * GROUP_W, (g + 1) * GROUP_W)
            s_b = s_ref[g].astype(BF16)
            y_ref[rows, cols] = y_ref[rows, cols] + _dot(q_ref[rows, cols], s_b)
            s_ref[g] = _dot(g_ref[c * n_groups + g], s_b) + h_ref[c * n_groups + g]

    y = y_ref[...]
    mean = _head_sum(y, ones_b) * (1.0 / HEAD_DIM)
    yc = y - mean
    var = _head_sum(yc * yc, ones_b) * (1.0 / HEAD_DIM)
    yn = yc * lax.rsqrt(var + GN_EPS) * lw_ref[...] + lb_ref[...] + bonus
    o_ref[...] = (yn * gs_ref[...].astype(F32)).astype(BF16)


def _rwkv(pr, gs, mu, w0, w2a, a0, k_k, k_a, r_k, lnx_w, lnx_b, batch, seq):
    m = pr.shape[0]
    per_b = seq // TOK_TILE
    W = RWKV_WIDTH
    n_cg = (TOK_TILE // CHUNK) * (W // GROUP_W)
    w2a_hi = w2a.astype(BF16)
    w2a_lo = (w2a - w2a_hi.astype(F32)).astype(BF16)
    w2a_hi = jnp.concatenate([w2a_hi, w2a_hi], axis=0)
    tile = lambda b, t: (b * per_b + t, 0)
    const = lambda b, t: (0, 0)
    vec = pl.BlockSpec((1, W), const)
    tile_scratch = pltpu.VMEM((TOK_TILE, W), F32)
    return pl.pallas_call(
        _rwkv_body,
        grid=(batch, per_b),
        in_specs=[
            pl.BlockSpec((TOK_TILE, N_RWKV_COLS), tile),
            pl.BlockSpec((TOK_TILE, W), tile),
            pl.BlockSpec((1, N_RWKV_COLS), const),
            vec,
            pl.BlockSpec(w2a_hi.shape, const),
            pl.BlockSpec(w2a_lo.shape, const),
            vec, vec, vec, vec, vec, vec,
        ],
        out_specs=pl.BlockSpec((TOK_TILE, W), tile),
        out_shape=jax.ShapeDtypeStruct((m, W), BF16),
        scratch_shapes=[
            pltpu.VMEM((W // GROUP_W, GROUP_W, GROUP_W), F32),
            pltpu.VMEM((1, N_RWKV_COLS), F32),
            tile_scratch, tile_scratch, tile_scratch, tile_scratch, tile_scratch, tile_scratch, tile_scratch,
            pltpu.VMEM((TOK_TILE, W), BF16),
            pltpu.VMEM((n_cg, GROUP_W, GROUP_W), BF16),
            pltpu.VMEM((n_cg, GROUP_W, GROUP_W), F32),
        ],
        compiler_params=pltpu.CompilerParams(
            dimension_semantics=("arbitrary", "arbitrary"), vmem_limit_bytes=VMEM_LIMIT),
        name="rwkv7",
    )(pr, gs, mu, w0, w2a_hi, w2a_lo, a0, k_k, k_a, r_k, lnx_w, lnx_b)


LOG2E = 1.4426950408889634
AUX_S, AUX_T, AUX_BLK, AUX_SEL = 0, 3, 6, 16
PV_ONES_ROWS = 16
MOBA_STEP_HEADS = 4


def _moba_key_aux(seq):
    pos = jnp.arange(seq, dtype=jnp.int32)[:, None]
    lane = jnp.arange(LANES, dtype=jnp.int32)[None, :]
    blk = pos // MOBA_BLOCK
    tab = jnp.where(lane < AUX_T, pos % MOBA_BLOCK,
                    jnp.where(lane < AUX_BLK, 1,
                              jnp.where(lane < AUX_BLK + 3, blk * MOBA_BLOCK,
                                        jnp.where(lane == AUX_SEL + blk, 1, 0))))
    return tab.astype(BF16)


def _split3(x):
    x1 = x.astype(BF16).astype(F32)
    r1 = x - x1
    x2 = r1.astype(BF16).astype(F32)
    x3 = (r1 - x2).astype(BF16).astype(F32)
    return x1, x2, x3


def _moba_body(slopes_ref, q_ref, k_ref, ak_ref, vt_ref, km_ref, gs_ref, o_ref, st_ref, p_ref):
    BS = MOBA_BLOCK
    grp = pl.program_id(1)
    i = pl.program_id(2)
    heads = MOBA_STEP_HEADS
    per_pair = LANES // HEAD_DIM
    nb = km_ref.shape[1]
    lane_h = lax.broadcasted_iota(jnp.int32, (1, LANES), 1) // HEAD_DIM
    n_iota = lax.broadcasted_iota(jnp.int32, (nb, BS), 0)
    causal = (lax.broadcasted_iota(jnp.int32, (BS, BS), 0)
              <= lax.broadcasted_iota(jnp.int32, (BS, BS), 1))
    t_glob = (lax.broadcasted_iota(jnp.int32, (nb, BS), 1) + i * BS).astype(F32)
    piece = n_iota % 3
    ones_rows = jnp.ones((PV_ONES_ROWS, BS), BF16)

    def key_rows(n):
        b = jnp.where(n == 0, i, n - 1)
        return pl.ds(pl.multiple_of(b * BS, BS), BS)

    def pair_lanes(h):
        p = h // per_pair
        return slice(p * LANES, (p + 1) * LANES)

    def scores(h, blk, q_aug_t):
        k_aug = jnp.concatenate([k_ref[blk, pair_lanes(h)], ak_ref[blk, :]], axis=1)
        return _dot(k_aug, q_aug_t)

    def pv(h, slot, blk):
        vt_aug = jnp.concatenate([vt_ref[0, h * HEAD_DIM:(h + 1) * HEAD_DIM, blk], ones_rows], axis=0)
        return _dot(vt_aug, p_ref[slot, h])

    q = q_ref[...]
    km = km_ref[0]
    lane_head = lax.broadcasted_iota(jnp.int32, (1, heads * HEAD_DIM), 1) // HEAD_DIM
    km_stack = jnp.concatenate([jnp.where(lane_head == h, km, 0.0) for h in range(heads)], axis=0)
    q_t = (q * LOG2E).T
    q_hi, q_lo = _split2(q_t)
    km_hi, km_lo = _split2(km_stack)
    gates = _dot(jnp.concatenate([km_hi, km_lo, km_hi], axis=1), jnp.concatenate([q_hi, q_hi, q_lo], axis=0))
    gate = [jnp.where(n_iota < i, gates[h * nb:(h + 1) * nb], NEG) for h in range(heads)]
    row_h = lax.broadcasted_iota(jnp.int32, (LANES, 1), 0) // HEAD_DIM
    zero_b = jnp.zeros((), BF16)
    qh_t, aux_top = [], []
    for h in range(heads):
        cvec = jnp.full((nb, BS), slopes_ref[grp * heads + h] * LOG2E, F32)
        c1, c2, c3 = _split3(cvec)
        e1, e2, e3 = _split3(-cvec * t_glob)
        cp = jnp.where(piece == 0, c1, jnp.where(piece == 1, c2, c3))
        ep = jnp.where(piece == 0, e1, jnp.where(piece == 1, e2, e3))
        top = jnp.where((n_iota >= AUX_T) & (n_iota < AUX_BLK), ep, cp)
        aux_top.append(jnp.where(n_iota < AUX_BLK + 3, top, 0.0).astype(BF16))
        qh_t.append(jnp.where(row_h == h % per_pair, q_hi[pair_lanes(h), :], zero_b))
    aux_pad = jnp.zeros((LANES - AUX_SEL - nb, BS), BF16)

    def q_aug(h, sel_rows):
        return jnp.concatenate([qh_t[h], aux_top[h], sel_rows, aux_pad], axis=0)

    no_mask = jnp.zeros((nb, BS), BF16)
    for h in range(heads):
        st_ref[0, h] = jnp.where(causal, scores(h, key_rows(0), q_aug(h, no_mask)), NEG)
        p_ref[0, h] = jnp.zeros((BS, BS), BF16)
    sel = [jnp.zeros((nb, BS), F32) for _ in range(heads)]
    for rnk in range(MOBA_TOPK):
        rank_ok = jnp.where(rnk < i, 1.0, 0.0)
        for h in range(heads):
            mx = jnp.max(gate[h], axis=0, keepdims=True)
            idx = jnp.min(jnp.where(gate[h] == mx, n_iota, nb), axis=0, keepdims=True)
            pick = n_iota == idx
            sel[h] = jnp.where(pick, rank_ok, sel[h])
            gate[h] = jnp.where(pick, -jnp.inf, gate[h])
    q_augs = [q_aug(h, jnp.where(sel[h] > 0.0, 0.0, NEG).astype(BF16)) for h in range(heads)]

    def stage(n, carry, src, dst):
        ms, accs, alphas = carry
        prev_rows = key_rows(jnp.maximum(n - 1, 0))
        next_rows = key_rows(n + 1)
        pvs = [pv(h, src, prev_rows) for h in range(heads)]
        ms2, alphas2 = [], []
        for h in range(heads):
            st = st_ref[src, h]
            m_new = jnp.maximum(ms[h], jnp.max(st, axis=0, keepdims=True))
            alphas2.append(jnp.exp2(ms[h] - m_new))
            p_ref[dst, h] = jnp.exp2(st - m_new).astype(BF16)
            ms2.append(m_new)
        for h in range(heads):
            st_ref[dst, h] = scores(h, next_rows, q_augs[h])
        accs2 = [alphas[h] * accs[h] + pvs[h] for h in range(heads)]
        return tuple(ms2), tuple(accs2), tuple(alphas2)

    def pair_step(t, carry):
        return stage(2 * t + 1, stage(2 * t, carry, 0, 1), 1, 0)

    def single_step(t, carry):
        return stage(i, carry, 0, 0)

    m0 = tuple(jnp.full((1, BS), NEG, F32) for _ in range(heads))
    acc0 = tuple(jnp.zeros((HEAD_DIM + PV_ONES_ROWS, BS), F32) for _ in range(heads))
    alpha0 = tuple(jnp.ones((1, BS), F32) for _ in range(heads))
    n_blocks = i + 1
    carry = lax.fori_loop(0, n_blocks // 2, pair_step, (m0, acc0, alpha0))
    ms, accs, alphas = lax.fori_loop(0, n_blocks % 2, single_step, carry)
    last_rows = key_rows(i)
    outs = []
    for h in range(heads):
        acc = alphas[h] * accs[h] + pv(h, 0, last_rows)
        outs.append(acc[:HEAD_DIM] / acc[HEAD_DIM:HEAD_DIM + 1])
    ot = jnp.concatenate(outs, axis=0)
    o_ref[...] = (ot.T * gs_ref[...].astype(F32)).astype(BF16)


def _moba(slopes, q, k, key_aux, vt, kmean, gs, batch, seq):
    m = q.shape[0]
    BS = MOBA_BLOCK
    nb = seq // BS
    assert nb == AUX_SEL and AUX_SEL + nb <= LANES and AUX_BLK + 3 <= AUX_SEL
    W = MOBA_STEP_HEADS * HEAD_DIM
    n_groups = ATTN_WIDTH // W
    gate_col0 = RWKV_WIDTH // W
    return pl.pallas_call(
        _moba_body,
        grid_spec=pltpu.PrefetchScalarGridSpec(
            num_scalar_prefetch=1,
            grid=(batch, n_groups, nb),
            in_specs=[
                pl.BlockSpec((BS, W), lambda b, g, i, s: (b * nb + i, g)),
                pl.BlockSpec((seq, W), lambda b, g, i, s: (b, g)),
                pl.BlockSpec((seq, LANES), lambda b, g, i, s: (0, 0)),
                pl.BlockSpec((1, W, seq), lambda b, g, i, s: (b, g, 0)),
                pl.BlockSpec((1, nb, W), lambda b, g, i, s: (b, 0, g)),
                pl.BlockSpec((BS, W), lambda b, g, i, s: (b * nb + i, gate_col0 + g)),
            ],
            out_specs=pl.BlockSpec((BS, W), lambda b, g, i, s: (b * nb + i, g)),
            scratch_shapes=[
                pltpu.VMEM((2, MOBA_STEP_HEADS, BS, BS), F32),
                pltpu.VMEM((2, MOBA_STEP_HEADS, BS, BS), BF16),
            ],
        ),
        out_shape=jax.ShapeDtypeStruct((m, ATTN_WIDTH), BF16),
        compiler_params=pltpu.CompilerParams(
            dimension_semantics=("arbitrary", "arbitrary", "arbitrary"), vmem_limit_bytes=VMEM_LIMIT),
        name="moba",
    )(slopes, q, k, key_aux, vt, kmean, gs)


def _outproj_body(x_ref, yr_ref, ya_ref, wr_ref, wa_ref, g_ref, o_ref):
    o = _dot(yr_ref[...], wr_ref[...]) + _dot(ya_ref[...], wa_ref[...])
    ms = jnp.mean(o * o, axis=-1, keepdims=True)
    o_ref[...] = x_ref[...] + o * lax.rsqrt(ms + RMS_EPS) * g_ref[...]


def _out_proj(x2, yr, ya, w_or, w_oa, g_post):
    m, d = x2.shape
    tm = 2 * TOK_TILE
    const = lambda i: (0, 0)
    return pl.pallas_call(
        _outproj_body,
        grid=(m // tm,),
        in_specs=[
            pl.BlockSpec((tm, d), lambda i: (i, 0)),
            pl.BlockSpec((tm, RWKV_WIDTH), lambda i: (i, 0)),
            pl.BlockSpec((tm, ATTN_WIDTH), lambda i: (i, 0)),
            pl.BlockSpec(w_or.shape, const),
            pl.BlockSpec(w_oa.shape, const),
            pl.BlockSpec((1, d), const),
        ],
        out_specs=pl.BlockSpec((tm, d), lambda i: (i, 0)),
        out_shape=jax.ShapeDtypeStruct((m, d), F32),
        compiler_params=pltpu.CompilerParams(
            dimension_semantics=("arbitrary",), vmem_limit_bytes=VMEM_LIMIT),
        name="out_proj",
    )(x2, yr, ya, w_or, w_oa, g_post)


def kernel(x, g_pre, w_in, tshift_mu, w0, w2, a0, a2, k_k, k_a, r_k, lnx_w, lnx_b, w_out, g_post):
    batch, seq, d = x.shape
    depth = w_in.shape[0]
    assert seq % TOK_TILE == 0 and TOK_TILE == MOBA_BLOCK and d % LANES == 0
    x2 = x.reshape(batch * seq, d)
    slopes = 2.0 ** (-8.0 * jnp.arange(1, ATTN_HEADS + 1, dtype=F32) / ATTN_HEADS)
    row = lambda z: z.reshape(1, -1).astype(F32)
    for l in range(depth):
        wb = w_in[l].astype(BF16)
        w_r = wb[:, :N_RWKV_COLS]
        w_a = wb[:, N_RWKV_COLS:N_RWKV_COLS + N_ATTN_COLS]
        w_g = wb[:, N_RWKV_COLS + N_ATTN_COLS:]
        pr, q, k, vt, kmean, gs = _in_proj(x2, row(g_pre[l]), w_r, w_a, w_g, batch, seq)
        zeros = jnp.zeros((DECAY_RANK, RWKV_WIDTH), F32)
        w2a = jnp.concatenate([jnp.concatenate([w2[l], zeros], axis=1),
                               jnp.concatenate([zeros, a2[l]], axis=1)], axis=0)
        yr = _rwkv(pr, gs, row(tshift_mu[l]), row(w0[l]), w2a, row(a0[l]), row(k_k[l]), row(k_a[l]),
                   row(r_k[l]), row(lnx_w[l]), row(lnx_b[l]), batch, seq)
        ya = _moba(slopes, q, k, _moba_key_aux(seq), vt,
                   kmean.reshape(batch, seq // MOBA_BLOCK, ATTN_WIDTH), gs, batch, seq)
        wo = w_out[l].astype(BF16)
        x2 = _out_proj(x2, yr, ya, wo[:RWKV_WIDTH], wo[RWKV_WIDTH:], row(g_post[l]))
    return x2.reshape(batch, seq, d)
```

```python
import functools

import jax
import jax.numpy as jnp
from jax import lax
from jax.experimental import pallas as pl
from jax.experimental.pallas import tpu as pltpu

F32 = jnp.float32
BF16 = jnp.bfloat16

HEAD_DIM = 64
RWKV_HEADS = 8
ATTN_HEADS = 8
RWKV_WIDTH = RWKV_HEADS * HEAD_DIM
ATTN_WIDTH = ATTN_HEADS * HEAD_DIM
D_MIX = RWKV_WIDTH + ATTN_WIDTH
DECAY_RANK = 64
ICLR_RANK = 64
MOBA_BLOCK = 256
MOBA_TOPK = 3
RMS_EPS = 1e-6
GN_EPS = 64e-5
EXP_M_HALF = 0.6065306597126334
NEG = -1e30
N_RWKV_COLS = 3 * RWKV_WIDTH + DECAY_RANK + ICLR_RANK
N_ATTN_COLS = 3 * ATTN_WIDTH

LANES = 128
MXU_DIM = 256
CHUNK = 64
GROUP_HEADS = MXU_DIM // HEAD_DIM
GROUP_W = GROUP_HEADS * HEAD_DIM
TOK_TILE = 256
RWKV_LOCKSTEP = 8
VMEM_LIMIT = 48 * 1024 * 1024

_NT = (((1,), (1,)), ((), ()))
_TN = (((0,), (0,)), ((), ()))


def _dot(a, b, dims=None, precision=None):
    if dims is None:
        return jnp.dot(a, b, preferred_element_type=F32, precision=precision)
    return lax.dot_general(a, b, dims, preferred_element_type=F32, precision=precision)


def _inproj_body(per_b, x_ref, g_ref, mu_ref, wr_ref, wa_ref, wg_ref,
                 pr_ref, q_ref, k_ref, vt_ref, km_ref, gs_ref, prev_ref):
    x = x_ref[...]
    ms = jnp.mean(x * x, axis=-1, keepdims=True)
    h = (x * lax.rsqrt(ms + RMS_EPS) * g_ref[...]).astype(BF16)
    @pl.when(pl.program_id(0) % per_b == 0)
    def _():
        prev_ref[...] = jnp.zeros_like(prev_ref)

    row0 = lax.broadcasted_iota(jnp.int32, (TOK_TILE, 1), 0) == 0
    for c0 in range(0, N_RWKV_COLS, MXU_DIM):
        cs = slice(c0, min(c0 + MXU_DIM, N_RWKV_COLS))
        p = _dot(h, wr_ref[:, cs])
        p_prev = jnp.where(row0, prev_ref[:, cs], pltpu.roll(p, 1, axis=0))
        prev_ref[:, cs] = p[TOK_TILE - 1:TOK_TILE, :]
        pr_ref[:, cs] = p + (p_prev - p) * mu_ref[:, cs]
    qkv = _dot(h, wa_ref[...])
    q_ref[...] = qkv[:, :ATTN_WIDTH] * (HEAD_DIM ** -0.5)
    kf = qkv[:, ATTN_WIDTH:2 * ATTN_WIDTH]
    k_ref[...] = kf.astype(BF16)
    km_ref[0] = jnp.mean(kf, axis=0, keepdims=True)
    vt_ref[0] = qkv[:, 2 * ATTN_WIDTH:].T.astype(BF16)
    g = _dot(h, wg_ref[...])
    gs_ref[...] = (g * jax.nn.sigmoid(g)).astype(BF16)


def _in_proj(x2, g_pre, mu, w_r, w_a, w_g, batch, seq):
    m, d = x2.shape
    nblk = m // TOK_TILE
    per_b = seq // TOK_TILE
    const = lambda i: (0, 0)
    return pl.pallas_call(
        functools.partial(_inproj_body, per_b),
        grid=(nblk,),
        in_specs=[
            pl.BlockSpec((TOK_TILE, d), lambda i: (i, 0)),
            pl.BlockSpec((1, d), const),
            pl.BlockSpec((1, N_RWKV_COLS), const),
            pl.BlockSpec(w_r.shape, const),
            pl.BlockSpec(w_a.shape, const),
            pl.BlockSpec(w_g.shape, const),
        ],
        out_specs=[
            pl.BlockSpec((TOK_TILE, N_RWKV_COLS), lambda i: (i, 0)),
            pl.BlockSpec((TOK_TILE, ATTN_WIDTH), lambda i: (i, 0)),
            pl.BlockSpec((TOK_TILE, ATTN_WIDTH), lambda i: (i, 0)),
            pl.BlockSpec((1, ATTN_WIDTH, TOK_TILE), lambda i: (i // per_b, 0, i % per_b)),
            pl.BlockSpec((1, 1, ATTN_WIDTH), lambda i: (i, 0, 0)),
            pl.BlockSpec((TOK_TILE, D_MIX), lambda i: (i, 0)),
        ],
        out_shape=[
            jax.ShapeDtypeStruct((m, N_RWKV_COLS), F32),
            jax.ShapeDtypeStruct((m, ATTN_WIDTH), F32),
            jax.ShapeDtypeStruct((m, ATTN_WIDTH), BF16),
            jax.ShapeDtypeStruct((batch, ATTN_WIDTH, seq), BF16),
            jax.ShapeDtypeStruct((nblk, 1, ATTN_WIDTH), F32),
            jax.ShapeDtypeStruct((m, D_MIX), BF16),
        ],
        scratch_shapes=[pltpu.VMEM((1, N_RWKV_COLS), F32)],
        compiler_params=pltpu.CompilerParams(
            dimension_semantics=("arbitrary",), vmem_limit_bytes=VMEM_LIMIT),
        name="in_proj",
    )(x2, g_pre, mu, w_r, w_a, w_g)


def _split2(x):
    hi = x.astype(BF16)
    lo = (x - hi.astype(F32)).astype(BF16)
    return hi, lo


def _split_dot(x, ones_b):
    hi, lo = _split2(x)
    return _dot(hi, ones_b) + _dot(lo, ones_b)


def _head_sum(x, ones_b, split=True):
    one = _split_dot if split else (lambda xs, o: _dot(xs.astype(BF16), o))
    parts = [one(x[:, g * GROUP_W:(g + 1) * GROUP_W], ones_b) for g in range(x.shape[1] // GROUP_W)]
    return jnp.concatenate(parts, axis=1)


def _rwkv_body(pr_ref, gs_ref, w0_ref, w2a_hi_ref, w2a_lo_ref, a0_ref, kk_ref, ka_ref, rk_ref, lw_ref,
               lb_ref, o_ref, s_ref, at_ref, bt_ref, kt_ref, rt_ref, v_ref, gl_ref, y_ref,
               q_ref, g_ref, h_ref):
    W = RWKV_WIDTH
    n_groups = W // GROUP_W
    n_chunks = TOK_TILE // CHUNK

    @pl.when(pl.program_id(1) == 0)
    def _():
        s_ref[...] = jnp.zeros_like(s_ref)

    rr = lax.broadcasted_iota(jnp.int32, (GROUP_W, GROUP_W), 0)
    cc = lax.broadcasted_iota(jnp.int32, (GROUP_W, GROUP_W), 1)
    same_head = (rr // HEAD_DIM) == (cc // HEAD_DIM)
    ones_b = same_head.astype(BF16)
    eye_g = rr == cc
    pr_ = lax.broadcasted_iota(jnp.int32, (2 * CHUNK, 4 * CHUNK), 0)
    pc_ = lax.broadcasted_iota(jnp.int32, (2 * CHUNK, 4 * CHUNK), 1)
    tril_pair = ((pr_ // CHUNK == pc_ // (2 * CHUNK)) & (pc_ % CHUNK <= pr_ % CHUNK)).astype(BF16)
    wr = lax.broadcasted_iota(jnp.int32, (CHUNK, GROUP_W), 0)
    wc = lax.broadcasted_iota(jnp.int32, (CHUNK, GROUP_W), 1) % CHUNK
    strict_w = wc < wr
    incl_w = wc <= wr
    eye_w = (wc == wr).astype(F32)
    lane_g = lax.broadcasted_iota(jnp.int32, (1, GROUP_W), 1) // HEAD_DIM
    head_masks = [lane_g == h for h in range(GROUP_HEADS)]

    def mstack(x):
        xb = x.astype(BF16)
        zero = jnp.zeros_like(xb)
        return jnp.concatenate([jnp.where(hm, xb, zero) for hm in head_masks], axis=0)

    def block_diag(xw):
        return jnp.where(same_head, jnp.concatenate([xw] * GROUP_HEADS, axis=0), jnp.zeros((), BF16))

    p = pr_ref[...]
    r = p[:, 0:W]
    k = p[:, W:2 * W]
    v = p[:, 2 * W:3 * W]
    z = p[:, 3 * W:3 * W + DECAY_RANK + ICLR_RANK]
    lane_z = lax.broadcasted_iota(jnp.int32, (1, DECAY_RANK + ICLR_RANK), 1)
    z = jnp.where(lane_z < DECAY_RANK, jnp.tanh(z), z)
    z_hi, z_lo = _split2(z)
    lin = _dot(jnp.concatenate([z_hi, z_lo], axis=1), w2a_hi_ref[...]) + _dot(z_hi, w2a_lo_ref[...])
    ld = -EXP_M_HALF * jax.nn.sigmoid(w0_ref[...] + lin[:, :W])
    a = jax.nn.sigmoid(a0_ref[...] + lin[:, W:])
    kkr = k * kk_ref[...]
    kk = kkr * jnp.minimum(lax.rsqrt(_head_sum(kkr * kkr, ones_b, split=False)), 1e12)
    k2 = k * (1.0 + (a - 1.0) * ka_ref[...])
    ld_hi, ld_lo = _split2(ld)
    cs_parts = []
    for pair in range(n_chunks // 2):
        pieces = []
        for c in (2 * pair, 2 * pair + 1):
            pieces += [ld_hi[c * CHUNK:(c + 1) * CHUNK], ld_lo[c * CHUNK:(c + 1) * CHUNK]]
        cs_parts.append(_dot(tril_pair, jnp.concatenate(pieces, axis=0)))
    cs = jnp.concatenate(cs_parts, axis=0)
    g_incl = jnp.exp(cs)
    g_inv = jnp.exp(-cs)
    at_ref[...] = (-kk) * jnp.exp(cs - ld)
    bt_ref[...] = kk * a * g_inv
    kt_ref[...] = k2 * g_inv
    rt_ref[...] = r * g_incl
    v_ref[...] = v
    gl_ref[...] = g_incl
    bonus = _head_sum(r * k2 * rk_ref[...], ones_b) * v

    chains = [(c, g) for c in range(n_chunks) for g in range(n_groups)]
    for first in range(0, len(chains), RWKV_LOCKSTEP):
        batch_chains = chains[first:first + RWKV_LOCKSTEP]
        nc = range(len(batch_chains))
        rows = [slice(c * CHUNK, (c + 1) * CHUNK) for c, _ in batch_chains]
        cols = [slice(g * GROUP_W, (g + 1) * GROUP_W) for _, g in batch_chains]
        x = []
        for j in nc:
            lhs = jnp.concatenate([at_ref[rows[j], cols[j]], rt_ref[rows[j], cols[j]]], axis=0).astype(BF16)
            rhs = jnp.concatenate([mstack(bt_ref[rows[j], cols[j]]), mstack(kt_ref[rows[j], cols[j]])], axis=0)
            x.append(_dot(lhs, rhs, _NT))
        a_ab = [jnp.where(strict_w, x[j][:CHUNK, :GROUP_W], 0.0) for j in nc]
        a_ak = [jnp.where(strict_w, x[j][:CHUNK, GROUP_W:], 0.0).astype(BF16) for j in nc]
        m_rb = [jnp.where(incl_w, x[j][CHUNK:, :GROUP_W], 0.0).astype(BF16) for j in nc]
        m_rk = [jnp.where(incl_w, x[j][CHUNK:, GROUP_W:], 0.0).astype(BF16) for j in nc]
        tw = [a_ab[j] + eye_w for j in nc]
        pw = [a_ab[j].astype(BF16) for j in nc]
        pw = [_dot(pw[j], block_diag(pw[j])).astype(BF16) for j in nc]
        for it in range(CHUNK.bit_length() - 3):
            both = [_dot(jnp.concatenate([pw[j], tw[j].astype(BF16)], axis=0), block_diag(pw[j])) for j in nc]
            tw = [tw[j] + both[j][CHUNK:] for j in nc]
            pw = [both[j][:CHUNK].astype(BF16) for j in nc]
        tw = [tw[j] + _dot(tw[j].astype(BF16), block_diag(pw[j])) for j in nc]
        twb = [tw[j].astype(BF16) for j in nc]
        v_m = [mstack(v_ref[rows[j], cols[j]]) for j in nc]
        wt = [_dot(twb[j], mstack(at_ref[rows[j], cols[j]])) for j in nc]
        x1 = [_dot(a_ak[j], v_m[j]) for j in nc]
        u0 = [_dot(twb[j], mstack(x1[j])) for j in nc]
        for j in nc:
            q_ref[rows[j], cols[j]] = (rt_ref[rows[j], cols[j]] + _dot(m_rb[j], mstack(wt[j]))).astype(BF16)
        for j in nc:
            y_ref[rows[j], cols[j]] = _dot(m_rb[j], mstack(u0[j])) + _dot(m_rk[j], v_m[j])
        for j in nc:
            c, g = batch_chains[j]
            gl = gl_ref[(c + 1) * CHUNK - 1:(c + 1) * CHUNK, cols[j]]
            bh = (bt_ref[rows[j], cols[j]] * gl).astype(BF16)
            kh = (kt_ref[rows[j], cols[j]] * gl).astype(BF16)
            gm = jnp.where(eye_g, gl, 0.0) + jnp.where(same_head, _dot(bh, wt[j].astype(BF16), _TN), 0.0)
            g_ref[c * n_groups + g] = gm.astype(BF16)
            h_ref[c * n_groups + g] = jnp.where(
                same_head,
                _dot(jnp.concatenate([bh, kh], axis=0),
                     jnp.concatenate([u0[j], v_ref[rows[j], cols[j]]], axis=0).astype(BF16), _TN),
                0.0)

    for c in range(n_chunks):
        rows = slice(c * CHUNK, (c + 1) * CHUNK)
        for g in range(n_groups):
            cols = slice(g * GROUP_W, (g + 1) * GROUP_W)
            s_b = s_ref[g].astype(BF16)
            y_ref[rows, cols] = y_ref[rows, cols] + _dot(q_ref[rows, cols], s_b)
            s_ref[g] = _dot(g_ref[c * n_groups + g], s_b) + h_ref[c * n_groups + g]

    y = y_ref[...]
    mean = _head_sum(y, ones_b) * (1.0 / HEAD_DIM)
    yc = y - mean
    var = _head_sum(yc * yc, ones_b) * (1.0 / HEAD_DIM)
    yn = yc * lax.rsqrt(var + GN_EPS) * lw_ref[...] + lb_ref[...] + bonus
    o_ref[...] = (yn * gs_ref[...].astype(F32)).astype(BF16)


def _rwkv(pr, gs, w0, w2a, a0, k_k, k_a, r_k, lnx_w, lnx_b, batch, seq):
    m = pr.shape[0]
    per_b = seq // TOK_TILE
    W = RWKV_WIDTH
    n_cg = (TOK_TILE // CHUNK) * (W // GROUP_W)
    w2a_hi = w2a.astype(BF16)
    w2a_lo = (w2a - w2a_hi.astype(F32)).astype(BF16)
    w2a_hi = jnp.concatenate([w2a_hi, w2a_hi], axis=0)
    tile = lambda b, t: (b * per_b + t, 0)
    const = lambda b, t: (0, 0)
    vec = pl.BlockSpec((1, W), const)
    tile_scratch = pltpu.VMEM((TOK_TILE, W), F32)
    return pl.pallas_call(
        _rwkv_body,
        grid=(batch, per_b),
        in_specs=[
            pl.BlockSpec((TOK_TILE, N_RWKV_COLS), tile),
            pl.BlockSpec((TOK_TILE, W), tile),
            vec,
            pl.BlockSpec(w2a_hi.shape, const),
            pl.BlockSpec(w2a_lo.shape, const),
            vec, vec, vec, vec, vec, vec,
        ],
        out_specs=pl.BlockSpec((TOK_TILE, W), tile),
        out_shape=jax.ShapeDtypeStruct((m, W), BF16),
        scratch_shapes=[
            pltpu.VMEM((W // GROUP_W, GROUP_W, GROUP_W), F32),
            tile_scratch, tile_scratch, tile_scratch, tile_scratch, tile_scratch, tile_scratch, tile_scratch,
            pltpu.VMEM((TOK_TILE, W), BF16),
            pltpu.VMEM((n_cg, GROUP_W, GROUP_W), BF16),
            pltpu.VMEM((n_cg, GROUP_W, GROUP_W), F32),
        ],
        compiler_params=pltpu.CompilerParams(
            dimension_semantics=("arbitrary", "arbitrary"), vmem_limit_bytes=VMEM_LIMIT),
        name="rwkv7",
    )(pr, gs, w0, w2a_hi, w2a_lo, a0, k_k, k_a, r_k, lnx_w, lnx_b)


LOG2E = 1.4426950408889634
AUX_S, AUX_T, AUX_BLK, AUX_SEL = 0, 3, 6, 16
PV_ONES_ROWS = 16
MOBA_STEP_HEADS = 4


def _moba_key_aux(seq):
    pos = jnp.arange(seq, dtype=jnp.int32)[:, None]
    lane = jnp.arange(LANES, dtype=jnp.int32)[None, :]
    blk = pos // MOBA_BLOCK
    tab = jnp.where(lane < AUX_T, pos % MOBA_BLOCK,
                    jnp.where(lane < AUX_BLK, 1,
                              jnp.where(lane < AUX_BLK + 3, blk * MOBA_BLOCK,
                                        jnp.where(lane == AUX_SEL + blk, 1, 0))))
    return tab.astype(BF16)


def _split3(x):
    x1 = x.astype(BF16).astype(F32)
    r1 = x - x1
    x2 = r1.astype(BF16).astype(F32)
    x3 = (r1 - x2).astype(BF16).astype(F32)
    return x1, x2, x3


def _moba_body(slopes_ref, q_ref, k_ref, ak_ref, vt_ref, km_ref, gs_ref, o_ref, st_ref, p_ref):
    BS = MOBA_BLOCK
    grp = pl.program_id(1)
    i = pl.program_id(2)
    heads = MOBA_STEP_HEADS
    per_pair = LANES // HEAD_DIM
    nb = km_ref.shape[1]
    lane_h = lax.broadcasted_iota(jnp.int32, (1, LANES), 1) // HEAD_DIM
    n_iota = lax.broadcasted_iota(jnp.int32, (nb, BS), 0)
    causal = (lax.broadcasted_iota(jnp.int32, (BS, BS), 0)
              <= lax.broadcasted_iota(jnp.int32, (BS, BS), 1))
    t_glob = (lax.broadcasted_iota(jnp.int32, (nb, BS), 1) + i * BS).astype(F32)
    piece = n_iota % 3
    ones_rows = jnp.ones((PV_ONES_ROWS, BS), BF16)

    def key_rows(n):
        b = jnp.where(n == 0, i, n - 1)
        return pl.ds(pl.multiple_of(b * BS, BS), BS)

    def pair_lanes(h):
        p = h // per_pair
        return slice(p * LANES, (p + 1) * LANES)

    def scores(h, blk, q_aug_t):
        k_aug = jnp.concatenate([k_ref[blk, pair_lanes(h)], ak_ref[blk, :]], axis=1)
        return _dot(k_aug, q_aug_t)

    def pv(h, slot, blk):
        vt_aug = jnp.concatenate([vt_ref[0, h * HEAD_DIM:(h + 1) * HEAD_DIM, blk], ones_rows], axis=0)
        return _dot(vt_aug, p_ref[slot, h])

    q = q_ref[...]
    km = km_ref[0]
    lane_head = lax.broadcasted_iota(jnp.int32, (1, heads * HEAD_DIM), 1) // HEAD_DIM
    km_stack = jnp.concatenate([jnp.where(lane_head == h, km, 0.0) for h in range(heads)], axis=0)
    q_t = (q * LOG2E).T
    q_hi, q_lo = _split2(q_t)
    km_hi, km_lo = _split2(km_stack)
    gates = _dot(jnp.concatenate([km_hi, km_lo, km_hi], axis=1), jnp.concatenate([q_hi, q_hi, q_lo], axis=0))
    gate = [jnp.where(n_iota < i, gates[h * nb:(h + 1) * nb], NEG) for h in range(heads)]
    row_h = lax.broadcasted_iota(jnp.int32, (LANES, 1), 0) // HEAD_DIM
    zero_b = jnp.zeros((), BF16)
    qh_t, aux_top = [], []
    for h in range(heads):
        cvec = jnp.full((nb, BS), slopes_ref[grp * heads + h] * LOG2E, F32)
        c1, c2, c3 = _split3(cvec)
        e1, e2, e3 = _split3(-cvec * t_glob)
        cp = jnp.where(piece == 0, c1, jnp.where(piece == 1, c2, c3))
        ep = jnp.where(piece == 0, e1, jnp.where(piece == 1, e2, e3))
        top = jnp.where((n_iota >= AUX_T) & (n_iota < AUX_BLK), ep, cp)
        aux_top.append(jnp.where(n_iota < AUX_BLK + 3, top, 0.0).astype(BF16))
        qh_t.append(jnp.where(row_h == h % per_pair, q_hi[pair_lanes(h), :], zero_b))
    aux_pad = jnp.zeros((LANES - AUX_SEL - nb, BS), BF16)

    def q_aug(h, sel_rows):
        return jnp.concatenate([qh_t[h], aux_top[h], sel_rows, aux_pad], axis=0)

    no_mask = jnp.zeros((nb, BS), BF16)
    for h in range(heads):
        st_ref[0, h] = jnp.where(causal, scores(h, key_rows(0), q_aug(h, no_mask)), NEG)
        p_ref[0, h] = jnp.zeros((BS, BS), BF16)
    sel = [jnp.zeros((nb, BS), F32) for _ in range(heads)]
    for rnk in range(MOBA_TOPK):
        rank_ok = jnp.where(rnk < i, 1.0, 0.0)
        for h in range(heads):
            mx = jnp.max(gate[h], axis=0, keepdims=True)
            idx = jnp.min(jnp.where(gate[h] == mx, n_iota, nb), axis=0, keepdims=True)
            pick = n_iota == idx
            sel[h] = jnp.where(pick, rank_ok, sel[h])
            gate[h] = jnp.where(pick, -jnp.inf, gate[h])
    q_augs = [q_aug(h, jnp.where(sel[h] > 0.0, 0.0, NEG).astype(BF16)) for h in range(heads)]

    def stage(n, carry, src, dst):
        ms, accs, alphas = carry
        prev_rows = key_rows(jnp.maximum(n - 1, 0))
        next_rows = key_rows(n + 1)
        pvs = [pv(h, src, prev_rows) for h in range(heads)]
        ms2, alphas2 = [], []
        for h in range(heads):
            st = st_ref[src, h]
            m_new = jnp.maximum(ms[h], jnp.max(st, axis=0, keepdims=True))
            alphas2.append(jnp.exp2(ms[h] - m_new))
            p_ref[dst, h] = jnp.exp2(st - m_new).astype(BF16)
            ms2.append(m_new)
        for h in range(heads):
            st_ref[dst, h] = scores(h, next_rows, q_augs[h])
        accs2 = [alphas[h] * accs[h] + pvs[h] for h in range(heads)]
        return tuple(ms2), tuple(accs2), tuple(alphas2)

    def pair_step(t, carry):
        return stage(2 * t + 1, stage(2 * t, carry, 0, 1), 1, 0)

    def single_step(t, carry):
        return stage(i, carry, 0, 0)

    m0 = tuple(jnp.full((1, BS), NEG, F32) for _ in range(heads))
    acc0 = tuple(jnp.zeros((HEAD_DIM + PV_ONES_ROWS, BS), F32) for _ in range(heads))
    alpha0 = tuple(jnp.ones((1, BS), F32) for _ in range(heads))
    n_blocks = i + 1
    carry = lax.fori_loop(0, n_blocks // 2, pair_step, (m0, acc0, alpha0))
    ms, accs, alphas = lax.fori_loop(0, n_blocks % 2, single_step, carry)
    last_rows = key_rows(i)
    outs = []
    for h in range(heads):
        acc = alphas[h] * accs[h] + pv(h, 0, last_rows)
        outs.append(acc[:HEAD_DIM] / acc[HEAD_DIM:HEAD_DIM + 1])
    ot = jnp.concatenate(outs, axis=0)
    o_ref[...] = (ot.T * gs_ref[...].astype(F32)).astype(BF16)


def _moba(slopes, q, k, key_aux, vt, kmean, gs, batch, seq):
    m = q.shape[0]
    BS = MOBA_BLOCK
    nb = seq // BS
    assert nb == AUX_SEL and AUX_SEL + nb <= LANES and AUX_BLK + 3 <= AUX_SEL
    W = MOBA_STEP_HEADS * HEAD_DIM
    n_groups = ATTN_WIDTH // W
    gate_col0 = RWKV_WIDTH // W
    return pl.pallas_call(
        _moba_body,
        grid_spec=pltpu.PrefetchScalarGridSpec(
            num_scalar_prefetch=1,
            grid=(batch, n_groups, nb),
            in_specs=[
                pl.BlockSpec((BS, W), lambda b, g, i, s: (b * nb + i, g)),
                pl.BlockSpec((seq, W), lambda b, g, i, s: (b, g)),
                pl.BlockSpec((seq, LANES), lambda b, g, i, s: (0, 0)),
                pl.BlockSpec((1, W, seq), lambda b, g, i, s: (b, g, 0)),
                pl.BlockSpec((1, nb, W), lambda b, g, i, s: (b, 0, g)),
                pl.BlockSpec((BS, W), lambda b, g, i, s: (b * nb + i, gate_col0 + g)),
            ],
            out_specs=pl.BlockSpec((BS, W), lambda b, g, i, s: (b * nb + i, g)),
            scratch_shapes=[
                pltpu.VMEM((2, MOBA_STEP_HEADS, BS, BS), F32),
                pltpu.VMEM((2, MOBA_STEP_HEADS, BS, BS), BF16),
            ],
        ),
        out_shape=jax.ShapeDtypeStruct((m, ATTN_WIDTH), BF16),
        compiler_params=pltpu.CompilerParams(
            dimension_semantics=("arbitrary", "arbitrary", "arbitrary"), vmem_limit_bytes=VMEM_LIMIT),
        name="moba",
    )(slopes, q, k, key_aux, vt, kmean, gs)


def _outproj_body(x_ref, yr_ref, ya_ref, wr_ref, wa_ref, g_ref, o_ref):
    o = _dot(yr_ref[...], wr_ref[...]) + _dot(ya_ref[...], wa_ref[...])
    ms = jnp.mean(o * o, axis=-1, keepdims=True)
    o_ref[...] = x_ref[...] + o * lax.rsqrt(ms + RMS_EPS) * g_ref[...]


def _out_proj(x2, yr, ya, w_or, w_oa, g_post):
    m, d = x2.shape
    tm = 2 * TOK_TILE
    const = lambda i: (0, 0)
    return pl.pallas_call(
        _outproj_body,
        grid=(m // tm,),
        in_specs=[
            pl.BlockSpec((tm, d), lambda i: (i, 0)),
            pl.BlockSpec((tm, RWKV_WIDTH), lambda i: (i, 0)),
            pl.BlockSpec((tm, ATTN_WIDTH), lambda i: (i, 0)),
            pl.BlockSpec(w_or.shape, const),
            pl.BlockSpec(w_oa.shape, const),
            pl.BlockSpec((1, d), const),
        ],
        out_specs=pl.BlockSpec((tm, d), lambda i: (i, 0)),
        out_shape=jax.ShapeDtypeStruct((m, d), F32),
        compiler_params=pltpu.CompilerParams(
            dimension_semantics=("arbitrary",), vmem_limit_bytes=VMEM_LIMIT),
        name="out_proj",
    )(x2, yr, ya, w_or, w_oa, g_post)


def kernel(x, g_pre, w_in, tshift_mu, w0, w2, a0, a2, k_k, k_a, r_k, lnx_w, lnx_b, w_out, g_post):
    batch, seq, d = x.shape
    depth = w_in.shape[0]
    assert seq % TOK_TILE == 0 and TOK_TILE == MOBA_BLOCK and d % LANES == 0
    x2 = x.reshape(batch * seq, d)
    slopes = 2.0 ** (-8.0 * jnp.arange(1, ATTN_HEADS + 1, dtype=F32) / ATTN_HEADS)
    row = lambda z: z.reshape(1, -1).astype(F32)
    for l in range(depth):
        wb = w_in[l].astype(BF16)
        w_r = wb[:, :N_RWKV_COLS]
        w_a = wb[:, N_RWKV_COLS:N_RWKV_COLS + N_ATTN_COLS]
        w_g = wb[:, N_RWKV_COLS + N_ATTN_COLS:]
        pr, q, k, vt, kmean, gs = _in_proj(x2, row(g_pre[l]), row(tshift_mu[l]), w_r, w_a, w_g, batch, seq)
        zeros = jnp.zeros((DECAY_RANK, RWKV_WIDTH), F32)
        w2a = jnp.concatenate([jnp.concatenate([w2[l], zeros], axis=1),
                               jnp.concatenate([zeros, a2[l]], axis=1)], axis=0)
        yr = _rwkv(pr, gs, row(w0[l]), w2a, row(a0[l]), row(k_k[l]), row(k_a[l]),
                   row(r_k[l]), row(lnx_w[l]), row(lnx_b[l]), batch, seq)
        ya = _moba(slopes, q, k, _moba_key_aux(seq), vt,
                   kmean.reshape(batch, seq // MOBA_BLOCK, ATTN_WIDTH), gs, batch, seq)
        wo = w_out[l].astype(BF16)
        x2 = _out_proj(x2, yr, ya, wo[:RWKV_WIDTH], wo[RWKV_WIDTH:], row(g_post[l]))
    return x2.reshape(batch, seq, d)
```

```python
import functools

import jax
import jax.numpy as jnp
from jax import lax
from jax.experimental import pallas as pl
from jax.experimental.pallas import tpu as pltpu

F32 = jnp.float32
BF16 = jnp.bfloat16

HEAD_DIM = 64
RWKV_HEADS = 8
ATTN_HEADS = 8
RWKV_WIDTH = RWKV_HEADS * HEAD_DIM
ATTN_WIDTH = ATTN_HEADS * HEAD_DIM
D_MIX = RWKV_WIDTH + ATTN_WIDTH
DECAY_RANK = 64
ICLR_RANK = 64
MOBA_BLOCK = 256
MOBA_TOPK = 3
RMS_EPS = 1e-6
GN_EPS = 64e-5
EXP_M_HALF = 0.6065306597126334
NEG = -1e30
N_RWKV_COLS = 3 * RWKV_WIDTH + DECAY_RANK + ICLR_RANK
N_ATTN_COLS = 3 * ATTN_WIDTH

LANES = 128
MXU_DIM = 256
CHUNK = 64
GROUP_HEADS = MXU_DIM // HEAD_DIM
GROUP_W = GROUP_HEADS * HEAD_DIM
TOK_TILE = 256
RWKV_TILE = 512
RWKV_RING = 512
VMEM_LIMIT = 48 * 1024 * 1024

_NT = (((1,), (1,)), ((), ()))
_TN = (((0,), (0,)), ((), ()))


def _dot(a, b, dims=None, precision=None):
    if dims is None:
        return jnp.dot(a, b, preferred_element_type=F32, precision=precision)
    return lax.dot_general(a, b, dims, preferred_element_type=F32, precision=precision)


def _inproj_body(per_b, x_ref, g_ref, mu_ref, wr_ref, wa_ref, wg_ref,
                 pr_ref, q_ref, k_ref, vt_ref, km_ref, gs_ref, prev_ref):
    x = x_ref[...]
    ms = jnp.mean(x * x, axis=-1, keepdims=True)
    h = (x * lax.rsqrt(ms + RMS_EPS) * g_ref[...]).astype(BF16)
    @pl.when(pl.program_id(0) % per_b == 0)
    def _():
        prev_ref[...] = jnp.zeros_like(prev_ref)

    row0 = lax.broadcasted_iota(jnp.int32, (TOK_TILE, 1), 0) == 0
    for c0 in range(0, N_RWKV_COLS, MXU_DIM):
        cs = slice(c0, min(c0 + MXU_DIM, N_RWKV_COLS))
        p = _dot(h, wr_ref[:, cs])
        p_prev = jnp.where(row0, prev_ref[:, cs], pltpu.roll(p, 1, axis=0))
        prev_ref[:, cs] = p[TOK_TILE - 1:TOK_TILE, :]
        pr_ref[:, cs] = p + (p_prev - p) * mu_ref[:, cs]
    qkv = _dot(h, wa_ref[...])
    q_ref[...] = qkv[:, :ATTN_WIDTH] * (HEAD_DIM ** -0.5)
    kf = qkv[:, ATTN_WIDTH:2 * ATTN_WIDTH]
    k_ref[...] = kf.astype(BF16)
    km_ref[0] = jnp.mean(kf, axis=0, keepdims=True)
    vt_ref[0] = qkv[:, 2 * ATTN_WIDTH:].T.astype(BF16)
    g = _dot(h, wg_ref[...])
    gs_ref[...] = (g * jax.nn.sigmoid(g)).astype(BF16)


def _in_proj(x2, g_pre, mu, w_r, w_a, w_g, batch, seq):
    m, d = x2.shape
    nblk = m // TOK_TILE
    per_b = seq // TOK_TILE
    const = lambda i: (0, 0)
    return pl.pallas_call(
        functools.partial(_inproj_body, per_b),
        grid=(nblk,),
        in_specs=[
            pl.BlockSpec((TOK_TILE, d), lambda i: (i, 0)),
            pl.BlockSpec((1, d), const),
            pl.BlockSpec((1, N_RWKV_COLS), const),
            pl.BlockSpec(w_r.shape, const),
            pl.BlockSpec(w_a.shape, const),
            pl.BlockSpec(w_g.shape, const),
        ],
        out_specs=[
            pl.BlockSpec((TOK_TILE, N_RWKV_COLS), lambda i: (i, 0)),
            pl.BlockSpec((TOK_TILE, ATTN_WIDTH), lambda i: (i, 0)),
            pl.BlockSpec((TOK_TILE, ATTN_WIDTH), lambda i: (i, 0)),
            pl.BlockSpec((1, ATTN_WIDTH, TOK_TILE), lambda i: (i // per_b, 0, i % per_b)),
            pl.BlockSpec((1, 1, ATTN_WIDTH), lambda i: (i, 0, 0)),
            pl.BlockSpec((TOK_TILE, D_MIX), lambda i: (i, 0)),
        ],
        out_shape=[
            jax.ShapeDtypeStruct((m, N_RWKV_COLS), F32),
            jax.ShapeDtypeStruct((m, ATTN_WIDTH), F32),
            jax.ShapeDtypeStruct((m, ATTN_WIDTH), BF16),
            jax.ShapeDtypeStruct((batch, ATTN_WIDTH, seq), BF16),
            jax.ShapeDtypeStruct((nblk, 1, ATTN_WIDTH), F32),
            jax.ShapeDtypeStruct((m, D_MIX), BF16),
        ],
        scratch_shapes=[pltpu.VMEM((1, N_RWKV_COLS), F32)],
        compiler_params=pltpu.CompilerParams(
            dimension_semantics=("arbitrary",), vmem_limit_bytes=VMEM_LIMIT),
        name="in_proj",
    )(x2, g_pre, mu, w_r, w_a, w_g)


def _split2(x):
    hi = x.astype(BF16)
    lo = (x - hi.astype(F32)).astype(BF16)
    return hi, lo


def _split_dot(x, ones_b):
    hi, lo = _split2(x)
    return _dot(hi, ones_b) + _dot(lo, ones_b)


def _head_sum(x, ones_b, split=True):
    one = _split_dot if split else (lambda xs, o: _dot(xs.astype(BF16), o))
    parts = [one(x[:, g * GROUP_W:(g + 1) * GROUP_W], ones_b) for g in range(x.shape[1] // GROUP_W)]
    return jnp.concatenate(parts, axis=1)


def _rwkv_body(pr_ref, gs_ref, w0_ref, w2a_hi_ref, w2a_lo_ref, a0_ref, kk_ref, ka_ref, rk_ref, lw_ref,
               lb_ref, o_ref, s_ref, at_ref, bt_ref, kt_ref, rt_ref, v_ref, gl_ref, y_ref, bonus_ref,
               q_ref, g_ref, h_ref):
    W = RWKV_WIDTH
    n_groups = W // GROUP_W
    n_chunks = RWKV_TILE // CHUNK

    @pl.when(pl.program_id(1) == 0)
    def _():
        s_ref[...] = jnp.zeros_like(s_ref)

    rr = lax.broadcasted_iota(jnp.int32, (GROUP_W, GROUP_W), 0)
    cc = lax.broadcasted_iota(jnp.int32, (GROUP_W, GROUP_W), 1)
    same_head = (rr // HEAD_DIM) == (cc // HEAD_DIM)
    ones_b = same_head.astype(BF16)
    eye_g = rr == cc
    pr_ = lax.broadcasted_iota(jnp.int32, (2 * CHUNK, 4 * CHUNK), 0)
    pc_ = lax.broadcasted_iota(jnp.int32, (2 * CHUNK, 4 * CHUNK), 1)
    tril_pair = ((pr_ // CHUNK == pc_ // (2 * CHUNK)) & (pc_ % CHUNK <= pr_ % CHUNK)).astype(BF16)
    wr = lax.broadcasted_iota(jnp.int32, (CHUNK, GROUP_W), 0)
    wc = lax.broadcasted_iota(jnp.int32, (CHUNK, GROUP_W), 1) % CHUNK
    strict_w = wc < wr
    incl_w = wc <= wr
    eye_w = (wc == wr).astype(F32)
    lane_g = lax.broadcasted_iota(jnp.int32, (1, GROUP_W), 1) // HEAD_DIM
    head_masks = [lane_g == h for h in range(GROUP_HEADS)]

    def mstack(x):
        xb = x.astype(BF16)
        zero = jnp.zeros_like(xb)
        return jnp.concatenate([jnp.where(hm, xb, zero) for hm in head_masks], axis=0)

    def block_diag(xw):
        return jnp.where(same_head, jnp.concatenate([xw] * GROUP_HEADS, axis=0), jnp.zeros((), BF16))

    HALF = 2 * CHUNK
    SUB_CHUNKS = 4
    RING_CHUNKS = RWKV_RING // CHUNK

    def ring_rows(c, n=1):
        r0 = (c % RING_CHUNKS) * CHUNK
        return slice(r0, r0 + n * CHUNK)

    def ring_slot(c, g):
        return (c % RING_CHUNKS) * n_groups + g

    lane_z = lax.broadcasted_iota(jnp.int32, (1, DECAY_RANK + ICLR_RANK), 1)

    def run(main, side=None, side_start=0):
        gens = [main] + ([side] if side is not None else [])
        stage_no = 0
        while gens:
            for gen in list(gens):
                if gen is side and stage_no < side_start and main in gens:
                    continue
                try:
                    next(gen)
                except StopIteration:
                    gens.remove(gen)
            stage_no += 1

    def seq(*gens):
        for gen in gens:
            yield from gen

    def prep(half):
        hr = slice(half * HALF, (half + 1) * HALF)
        rr_ = ring_rows(2 * half, 2)
        z = pr_ref[hr, 3 * W:3 * W + DECAY_RANK + ICLR_RANK]
        z = jnp.where(lane_z < DECAY_RANK, jnp.tanh(z), z)
        z_hi, z_lo = _split2(z)
        lin = _dot(jnp.concatenate([z_hi, z_lo], axis=1), w2a_hi_ref[...]) + _dot(z_hi, w2a_lo_ref[...])
        k = pr_ref[hr, W:2 * W]
        kkr = k * kk_ref[...]
        ss = _head_sum(kkr * kkr, ones_b, split=False)
        yield
        ld = -EXP_M_HALF * jax.nn.sigmoid(w0_ref[...] + lin[:, :W])
        ld_hi, ld_lo = _split2(ld)
        cs = _dot(tril_pair, jnp.concatenate([ld_hi[:CHUNK], ld_lo[:CHUNK], ld_hi[CHUNK:], ld_lo[CHUNK:]], axis=0))
        a = jax.nn.sigmoid(a0_ref[...] + lin[:, W:])
        k2 = k * (1.0 + (a - 1.0) * ka_ref[...])
        r = pr_ref[hr, 0:W]
        bsum = _head_sum(r * k2 * rk_ref[...], ones_b)
        yield
        kk = kkr * jnp.minimum(lax.rsqrt(ss), 1e12)
        g_inv = jnp.exp(-cs)
        at_ref[rr_, :] = (-kk) * jnp.exp(cs - ld)
        bt_ref[rr_, :] = kk * a * g_inv
        kt_ref[rr_, :] = k2 * g_inv
        yield
        g_incl = jnp.exp(cs)
        v = pr_ref[hr, 2 * W:3 * W]
        rt_ref[rr_, :] = r * g_incl
        v_ref[rr_, :] = v
        gl_ref[rr_, :] = g_incl
        bonus_ref[hr, :] = bsum * v
        yield

    def chain_factors(batch_chains):
        nc = range(len(batch_chains))
        rows = [ring_rows(c) for c, _ in batch_chains]
        yrows = [slice(c * CHUNK, (c + 1) * CHUNK) for c, _ in batch_chains]
        cols = [slice(g * GROUP_W, (g + 1) * GROUP_W) for _, g in batch_chains]
        x = []
        for j in nc:
            lhs = jnp.concatenate([at_ref[rows[j], cols[j]], rt_ref[rows[j], cols[j]]], axis=0).astype(BF16)
            rhs = jnp.concatenate([mstack(bt_ref[rows[j], cols[j]]), mstack(kt_ref[rows[j], cols[j]])], axis=0)
            x.append(_dot(lhs, rhs, _NT))
        yield
        a_ab = [jnp.where(strict_w, x[j][:CHUNK, :GROUP_W], 0.0) for j in nc]
        a_ak = [jnp.where(strict_w, x[j][:CHUNK, GROUP_W:], 0.0).astype(BF16) for j in nc]
        m_rb = [jnp.where(incl_w, x[j][CHUNK:, :GROUP_W], 0.0).astype(BF16) for j in nc]
        m_rk = [jnp.where(incl_w, x[j][CHUNK:, GROUP_W:], 0.0).astype(BF16) for j in nc]
        tw = [a_ab[j] + eye_w for j in nc]
        pw = [a_ab[j].astype(BF16) for j in nc]
        pw = [_dot(pw[j], block_diag(pw[j])).astype(BF16) for j in nc]
        yield
        for it in range(CHUNK.bit_length() - 3):
            both = [_dot(jnp.concatenate([pw[j], tw[j].astype(BF16)], axis=0), block_diag(pw[j])) for j in nc]
            tw = [tw[j] + both[j][CHUNK:] for j in nc]
            pw = [both[j][:CHUNK].astype(BF16) for j in nc]
            yield
        tw = [tw[j] + _dot(tw[j].astype(BF16), block_diag(pw[j])) for j in nc]
        twb = [tw[j].astype(BF16) for j in nc]
        yield
        v_m = [mstack(v_ref[rows[j], cols[j]]) for j in nc]
        wt = [_dot(twb[j], mstack(at_ref[rows[j], cols[j]])) for j in nc]
        x1 = [_dot(a_ak[j], v_m[j]) for j in nc]
        yield
        u0 = [_dot(twb[j], mstack(x1[j])) for j in nc]
        for j in nc:
            q_ref[rows[j], cols[j]] = (rt_ref[rows[j], cols[j]] + _dot(m_rb[j], mstack(wt[j]))).astype(BF16)
        yield
        for j in nc:
            y_ref[yrows[j], cols[j]] = _dot(m_rb[j], mstack(u0[j])) + _dot(m_rk[j], v_m[j])
        yield
        for j in nc:
            c, g = batch_chains[j]
            gl = gl_ref[rows[j].stop - 1:rows[j].stop, cols[j]]
            bh = (bt_ref[rows[j], cols[j]] * gl).astype(BF16)
            kh = (kt_ref[rows[j], cols[j]] * gl).astype(BF16)
            gm = jnp.where(eye_g, gl, 0.0) + jnp.where(same_head, _dot(bh, wt[j].astype(BF16), _TN), 0.0)
            g_ref[ring_slot(c, g)] = gm.astype(BF16)
            h_ref[ring_slot(c, g)] = jnp.where(
                same_head,
                _dot(jnp.concatenate([bh, kh], axis=0),
                     jnp.concatenate([u0[j], v_ref[rows[j], cols[j]]], axis=0).astype(BF16), _TN),
                0.0)
        yield

    def recurrence(c):
        rows = slice(c * CHUNK, (c + 1) * CHUNK)
        for g in range(n_groups):
            cols = slice(g * GROUP_W, (g + 1) * GROUP_W)
            s_b = s_ref[g].astype(BF16)
            y_ref[rows, cols] = y_ref[rows, cols] + _dot(q_ref[ring_rows(c), cols], s_b)
            s_ref[g] = _dot(g_ref[ring_slot(c, g)], s_b) + h_ref[ring_slot(c, g)]
        yield

    def finish(half):
        hr = slice(half * HALF, (half + 1) * HALF)
        y = y_ref[hr, :]
        mean = _head_sum(y, ones_b) * (1.0 / HEAD_DIM)
        yield
        yc = y - mean
        var = _head_sum(yc * yc, ones_b) * (1.0 / HEAD_DIM)
        yield
        yn = yc * lax.rsqrt(var + GN_EPS) * lw_ref[...] + lb_ref[...] + bonus_ref[hr, :]
        o_ref[hr, :] = (yn * gs_ref[hr, :].astype(F32)).astype(BF16)
        yield

    n_sub = n_chunks // SUB_CHUNKS
    halves = lambda sub: (2 * sub, 2 * sub + 1)
    chunks = lambda sub: range(SUB_CHUNKS * sub, SUB_CHUNKS * (sub + 1))
    tail = lambda sub: [recurrence(c) for c in chunks(sub)] + [finish(hf) for hf in halves(sub)]
    run(seq(*[prep(hf) for hf in halves(0)]))
    for sub in range(n_sub):
        side = [prep(hf) for hf in halves(sub + 1)] if sub + 1 < n_sub else []
        side += tail(sub - 1) if sub > 0 else []
        run(chain_factors([(c, g) for c in chunks(sub) for g in range(n_groups)]), seq(*side), side_start=2)
    run(seq(*tail(n_sub - 1)))


def _rwkv(pr, gs, w0, w2a, a0, k_k, k_a, r_k, lnx_w, lnx_b, batch, seq):
    m = pr.shape[0]
    per_b = seq // RWKV_TILE
    W = RWKV_WIDTH
    n_cg = (RWKV_RING // CHUNK) * (W // GROUP_W)
    w2a_hi = w2a.astype(BF16)
    w2a_lo = (w2a - w2a_hi.astype(F32)).astype(BF16)
    w2a_hi = jnp.concatenate([w2a_hi, w2a_hi], axis=0)
    tile = lambda b, t: (b * per_b + t, 0)
    const = lambda b, t: (0, 0)
    vec = pl.BlockSpec((1, W), const)
    ring_scratch = pltpu.VMEM((RWKV_RING, W), F32)
    tile_scratch = pltpu.VMEM((RWKV_TILE, W), F32)
    return pl.pallas_call(
        _rwkv_body,
        grid=(batch, per_b),
        in_specs=[
            pl.BlockSpec((RWKV_TILE, N_RWKV_COLS), tile),
            pl.BlockSpec((RWKV_TILE, W), tile),
            vec,
            pl.BlockSpec(w2a_hi.shape, const),
            pl.BlockSpec(w2a_lo.shape, const),
            vec, vec, vec, vec, vec, vec,
        ],
        out_specs=pl.BlockSpec((RWKV_TILE, W), tile),
        out_shape=jax.ShapeDtypeStruct((m, W), BF16),
        scratch_shapes=[
            pltpu.VMEM((W // GROUP_W, GROUP_W, GROUP_W), F32),
            ring_scratch, ring_scratch, ring_scratch, ring_scratch, ring_scratch, ring_scratch,
            tile_scratch, tile_scratch,
            pltpu.VMEM((RWKV_RING, W), BF16),
            pltpu.VMEM((n_cg, GROUP_W, GROUP_W), BF16),
            pltpu.VMEM((n_cg, GROUP_W, GROUP_W), F32),
        ],
        compiler_params=pltpu.CompilerParams(
            dimension_semantics=("arbitrary", "arbitrary"), vmem_limit_bytes=VMEM_LIMIT),
        name="rwkv7",
    )(pr, gs, w0, w2a_hi, w2a_lo, a0, k_k, k_a, r_k, lnx_w, lnx_b)


LOG2E = 1.4426950408889634
AUX_S, AUX_T, AUX_BLK, AUX_SEL = 0, 3, 6, 16
PV_ONES_ROWS = 16
MOBA_STEP_HEADS = 4


def _moba_key_aux(seq):
    pos = jnp.arange(seq, dtype=jnp.int32)[:, None]
    lane = jnp.arange(LANES, dtype=jnp.int32)[None, :]
    blk = pos // MOBA_BLOCK
    tab = jnp.where(lane < AUX_T, pos % MOBA_BLOCK,
                    jnp.where(lane < AUX_BLK, 1,
                              jnp.where(lane < AUX_BLK + 3, blk * MOBA_BLOCK,
                                        jnp.where(lane == AUX_SEL + blk, 1, 0))))
    return tab.astype(BF16)


def _split3(x):
    x1 = x.astype(BF16).astype(F32)
    r1 = x - x1
    x2 = r1.astype(BF16).astype(F32)
    x3 = (r1 - x2).astype(BF16).astype(F32)
    return x1, x2, x3


def _moba_body(slopes_ref, q_ref, k_ref, ak_ref, vt_ref, km_ref, gs_ref, o_ref, st_ref, p_ref):
    BS = MOBA_BLOCK
    grp = pl.program_id(1)
    i = pl.program_id(2)
    heads = MOBA_STEP_HEADS
    per_pair = LANES // HEAD_DIM
    nb = km_ref.shape[1]
    lane_h = lax.broadcasted_iota(jnp.int32, (1, LANES), 1) // HEAD_DIM
    n_iota = lax.broadcasted_iota(jnp.int32, (nb, BS), 0)
    causal = (lax.broadcasted_iota(jnp.int32, (BS, BS), 0)
              <= lax.broadcasted_iota(jnp.int32, (BS, BS), 1))
    t_glob = (lax.broadcasted_iota(jnp.int32, (nb, BS), 1) + i * BS).astype(F32)
    piece = n_iota % 3
    ones_rows = jnp.ones((PV_ONES_ROWS, BS), BF16)

    def key_rows(n):
        b = jnp.where(n == 0, i, jnp.minimum(n - 1, i))
        return pl.ds(pl.multiple_of(b * BS, BS), BS)

    def pair_lanes(h):
        p = h // per_pair
        return slice(p * LANES, (p + 1) * LANES)

    def scores(h, blk, q_aug_t):
        return _dot(k_aug(h, blk), q_aug_t)

    def k_aug(h, blk):
        return jnp.concatenate([k_ref[blk, pair_lanes(h)], ak_ref[blk, :]], axis=1)

    def pv(h, slot, blk):
        vt_aug = jnp.concatenate([vt_ref[0, h * HEAD_DIM:(h + 1) * HEAD_DIM, blk], ones_rows], axis=0)
        return _dot(vt_aug, p_ref[slot, h])

    q = q_ref[...]
    km = km_ref[0]
    lane_head = lax.broadcasted_iota(jnp.int32, (1, heads * HEAD_DIM), 1) // HEAD_DIM
    km_stack = jnp.concatenate([jnp.where(lane_head == h, km, 0.0) for h in range(heads)], axis=0)
    q_t = (q * LOG2E).T
    q_hi, q_lo = _split2(q_t)
    km_hi, km_lo = _split2(km_stack)
    gates = _dot(jnp.concatenate([km_hi, km_lo, km_hi], axis=1), jnp.concatenate([q_hi, q_hi, q_lo], axis=0))
    gate = [jnp.where(n_iota < i, gates[h * nb:(h + 1) * nb], NEG) for h in range(heads)]
    row_h = lax.broadcasted_iota(jnp.int32, (LANES, 1), 0) // HEAD_DIM
    zero_b = jnp.zeros((), BF16)
    qh_t, aux_top = [], []
    for h in range(heads):
        cvec = jnp.full((nb, BS), slopes_ref[grp * heads + h] * LOG2E, F32)
        c1, c2, c3 = _split3(cvec)
        e1, e2, e3 = _split3(-cvec * t_glob)
        cp = jnp.where(piece == 0, c1, jnp.where(piece == 1, c2, c3))
        ep = jnp.where(piece == 0, e1, jnp.where(piece == 1, e2, e3))
        top = jnp.where((n_iota >= AUX_T) & (n_iota < AUX_BLK), ep, cp)
        aux_top.append(jnp.where(n_iota < AUX_BLK + 3, top, 0.0).astype(BF16))
        qh_t.append(jnp.where(row_h == h % per_pair, q_hi[pair_lanes(h), :], zero_b))
    aux_pad = jnp.zeros((LANES - AUX_SEL - nb, BS), BF16)

    def q_aug(h, sel_rows):
        return jnp.concatenate([qh_t[h], aux_top[h], sel_rows, aux_pad], axis=0)

    no_mask = jnp.zeros((nb, BS), BF16)
    for h in range(heads):
        st_ref[0, h] = jnp.where(causal, scores(h, key_rows(0), q_aug(h, no_mask)), NEG)
        p_ref[0, h] = jnp.zeros((BS, BS), BF16)
    sel = [jnp.zeros((nb, BS), F32) for _ in range(heads)]
    for rnk in range(MOBA_TOPK):
        rank_ok = jnp.where(rnk < i, 1.0, 0.0)
        for h in range(heads):
            mx = jnp.max(gate[h], axis=0, keepdims=True)
            idx = jnp.min(jnp.where(gate[h] == mx, n_iota, nb), axis=0, keepdims=True)
            pick = n_iota == idx
            sel[h] = jnp.where(pick, rank_ok, sel[h])
            gate[h] = jnp.where(pick, -jnp.inf, gate[h])
    q_augs = [q_aug(h, jnp.where(sel[h] > 0.0, 0.0, NEG).astype(BF16)) for h in range(heads)]

    def stage(n, carry, src, dst):
        ms, accs, alphas = carry
        prev_rows = key_rows(jnp.maximum(n - 1, 0))
        next_rows = key_rows(n + 1)
        pvs = [pv(h, src, prev_rows) for h in range(heads)]
        ms2, alphas2 = [], []
        for h in range(heads):
            st = st_ref[src, h]
            m_new = jnp.maximum(ms[h], jnp.max(st, axis=0, keepdims=True))
            alphas2.append(jnp.exp2(ms[h] - m_new))
            p_ref[dst, h] = jnp.exp2(st - m_new).astype(BF16)
            ms2.append(m_new)
        for h in range(heads):
            st_ref[dst, h] = scores(h, next_rows, q_augs[h])
        accs2 = [alphas[h] * accs[h] + pvs[h] for h in range(heads)]
        return tuple(ms2), tuple(accs2), tuple(alphas2)

    def pair_step(t, carry):
        return stage(2 * t + 1, stage(2 * t, carry, 0, 1), 1, 0)

    def single_step(t, carry):
        return stage(i, carry, 0, 0)

    m0 = tuple(jnp.full((1, BS), NEG, F32) for _ in range(heads))
    acc0 = tuple(jnp.zeros((HEAD_DIM + PV_ONES_ROWS, BS), F32) for _ in range(heads))
    alpha0 = tuple(jnp.ones((1, BS), F32) for _ in range(heads))
    n_blocks = i + 1
    carry = lax.fori_loop(0, n_blocks // 2, pair_step, (m0, acc0, alpha0))
    ms, accs, alphas = lax.fori_loop(0, n_blocks % 2, single_step, carry)
    last_rows = key_rows(i)
    outs = []
    for h in range(heads):
        acc = alphas[h] * accs[h] + pv(h, 0, last_rows)
        outs.append(acc[:HEAD_DIM] / acc[HEAD_DIM:HEAD_DIM + 1])
    ot = jnp.concatenate(outs, axis=0)
    o_ref[...] = (ot.T * gs_ref[...].astype(F32)).astype(BF16)


def _moba(slopes, q, k, key_aux, vt, kmean, gs, batch, seq):
    m = q.shape[0]
    BS = MOBA_BLOCK
    nb = seq // BS
    assert nb == AUX_SEL and AUX_SEL + nb <= LANES and AUX_BLK + 3 <= AUX_SEL
    W = MOBA_STEP_HEADS * HEAD_DIM
    n_groups = ATTN_WIDTH // W
    gate_col0 = RWKV_WIDTH // W
    return pl.pallas_call(
        _moba_body,
        grid_spec=pltpu.PrefetchScalarGridSpec(
            num_scalar_prefetch=1,
            grid=(batch, n_groups, nb),
            in_specs=[
                pl.BlockSpec((BS, W), lambda b, g, i, s: (b * nb + i, g)),
                pl.BlockSpec((seq, W), lambda b, g, i, s: (b, g)),
                pl.BlockSpec((seq, LANES), lambda b, g, i, s: (0, 0)),
                pl.BlockSpec((1, W, seq), lambda b, g, i, s: (b, g, 0)),
                pl.BlockSpec((1, nb, W), lambda b, g, i, s: (b, 0, g)),
                pl.BlockSpec((BS, W), lambda b, g, i, s: (b * nb + i, gate_col0 + g)),
            ],
            out_specs=pl.BlockSpec((BS, W), lambda b, g, i, s: (b * nb + i, g)),
            scratch_shapes=[
                pltpu.VMEM((2, MOBA_STEP_HEADS, BS, BS), F32),
                pltpu.VMEM((2, MOBA_STEP_HEADS, BS, BS), BF16),
            ],
        ),
        out_shape=jax.ShapeDtypeStruct((m, ATTN_WIDTH), BF16),
        compiler_params=pltpu.CompilerParams(
            dimension_semantics=("arbitrary", "arbitrary", "arbitrary"), vmem_limit_bytes=VMEM_LIMIT),
        name="moba",
    )(slopes, q, k, key_aux, vt, kmean, gs)


def _outproj_body(x_ref, yr_ref, ya_ref, wr_ref, wa_ref, g_ref, o_ref):
    o = _dot(yr_ref[...], wr_ref[...]) + _dot(ya_ref[...], wa_ref[...])
    ms = jnp.mean(o * o, axis=-1, keepdims=True)
    o_ref[...] = x_ref[...] + o * lax.rsqrt(ms + RMS_EPS) * g_ref[...]


def _out_proj(x2, yr, ya, w_or, w_oa, g_post):
    m, d = x2.shape
    tm = 2 * TOK_TILE
    const = lambda i: (0, 0)
    return pl.pallas_call(
        _outproj_body,
        grid=(m // tm,),
        in_specs=[
            pl.BlockSpec((tm, d), lambda i: (i, 0)),
            pl.BlockSpec((tm, RWKV_WIDTH), lambda i: (i, 0)),
            pl.BlockSpec((tm, ATTN_WIDTH), lambda i: (i, 0)),
            pl.BlockSpec(w_or.shape, const),
            pl.BlockSpec(w_oa.shape, const),
            pl.BlockSpec((1, d), const),
        ],
        out_specs=pl.BlockSpec((tm, d), lambda i: (i, 0)),
        out_shape=jax.ShapeDtypeStruct((m, d), F32),
        compiler_params=pltpu.CompilerParams(
            dimension_semantics=("arbitrary",), vmem_limit_bytes=VMEM_LIMIT),
        name="out_proj",
    )(x2, yr, ya, w_or, w_oa, g_post)


def kernel(x, g_pre, w_in, tshift_mu, w0, w2, a0, a2, k_k, k_a, r_k, lnx_w, lnx_b, w_out, g_post):
    batch, seq, d = x.shape
    depth = w_in.shape[0]
    assert seq % RWKV_TILE == 0 and RWKV_TILE % TOK_TILE == 0 and TOK_TILE == MOBA_BLOCK and d % LANES == 0
    x2 = x.reshape(batch * seq, d)
    slopes = 2.0 ** (-8.0 * jnp.arange(1, ATTN_HEADS + 1, dtype=F32) / ATTN_HEADS)
    row = lambda z: z.reshape(1, -1).astype(F32)
    for l in range(depth):
        wb = w_in[l].astype(BF16)
        w_r = wb[:, :N_RWKV_COLS]
        w_a = wb[:, N_RWKV_COLS:N_RWKV_COLS + N_ATTN_COLS]
        w_g = wb[:, N_RWKV_COLS + N_ATTN_COLS:]
        pr, q, k, vt, kmean, gs = _in_proj(x2, row(g_pre[l]), row(tshift_mu[l]), w_r, w_a, w_g, batch, seq)
        zeros = jnp.zeros((DECAY_RANK, RWKV_WIDTH), F32)
        w2a = jnp.concatenate([jnp.concatenate([w2[l], zeros], axis=1),
                               jnp.concatenate([zeros, a2[l]], axis=1)], axis=0)
        yr = _rwkv(pr, gs, row(w0[l]), w2a, row(a0[l]), row(k_k[l]), row(k_a[l]),
                   row(r_k[l]), row(lnx_w[l]), row(lnx_b[l]), batch, seq)
        ya = _moba(slopes, q, k, _moba_key_aux(seq), vt,
                   kmean.reshape(batch, seq // MOBA_BLOCK, ATTN_WIDTH), gs, batch, seq)
        wo = w_out[l].astype(BF16)
        x2 = _out_proj(x2, yr, ya, wo[:RWKV_WIDTH], wo[RWKV_WIDTH:], row(g_post[l]))
    return x2.reshape(batch, seq, d)
```

```python
import functools

import jax
import jax.numpy as jnp
from jax import lax
from jax.experimental import pallas as pl
from jax.experimental.pallas import tpu as pltpu

F32 = jnp.float32
BF16 = jnp.bfloat16

HEAD_DIM = 64
RWKV_HEADS = 8
ATTN_HEADS = 8
RWKV_WIDTH = RWKV_HEADS * HEAD_DIM
ATTN_WIDTH = ATTN_HEADS * HEAD_DIM
D_MIX = RWKV_WIDTH + ATTN_WIDTH
DECAY_RANK = 64
ICLR_RANK = 64
MOBA_BLOCK = 256
MOBA_TOPK = 3
RMS_EPS = 1e-6
GN_EPS = 64e-5
EXP_M_HALF = 0.6065306597126334
NEG = -1e30
N_RWKV_COLS = 3 * RWKV_WIDTH + DECAY_RANK + ICLR_RANK
N_ATTN_COLS = 3 * ATTN_WIDTH

LANES = 128
MXU_DIM = 256
CHUNK = 64
GROUP_HEADS = MXU_DIM // HEAD_DIM
GROUP_W = GROUP_HEADS * HEAD_DIM
TOK_TILE = 256
RWKV_TILE = 512
RWKV_RING = 512
VMEM_LIMIT = 48 * 1024 * 1024

_NT = (((1,), (1,)), ((), ()))
_TN = (((0,), (0,)), ((), ()))


def _dot(a, b, dims=None, precision=None):
    if dims is None:
        return jnp.dot(a, b, preferred_element_type=F32, precision=precision)
    return lax.dot_general(a, b, dims, preferred_element_type=F32, precision=precision)


def _inproj_body(per_b, x_ref, g_ref, mu_ref, wr_ref, wa_ref, wg_ref,
                 pr_ref, q_ref, k_ref, vt_ref, km_ref, gs_ref, prev_ref):
    x = x_ref[...]
    ms = jnp.mean(x * x, axis=-1, keepdims=True)
    h = (x * lax.rsqrt(ms + RMS_EPS) * g_ref[...]).astype(BF16)
    @pl.when(pl.program_id(0) % per_b == 0)
    def _():
        prev_ref[...] = jnp.zeros_like(prev_ref)

    row0 = lax.broadcasted_iota(jnp.int32, (TOK_TILE, 1), 0) == 0
    for c0 in range(0, N_RWKV_COLS, MXU_DIM):
        cs = slice(c0, min(c0 + MXU_DIM, N_RWKV_COLS))
        p = _dot(h, wr_ref[:, cs])
        p_prev = jnp.where(row0, prev_ref[:, cs], pltpu.roll(p, 1, axis=0))
        prev_ref[:, cs] = p[TOK_TILE - 1:TOK_TILE, :]
        pr_ref[:, cs] = p + (p_prev - p) * mu_ref[:, cs]
    qkv = _dot(h, wa_ref[...])
    q_ref[...] = qkv[:, :ATTN_WIDTH] * (HEAD_DIM ** -0.5)
    kf = qkv[:, ATTN_WIDTH:2 * ATTN_WIDTH]
    k_ref[...] = kf.astype(BF16)
    km_ref[0] = jnp.mean(kf, axis=0, keepdims=True)
    vt_ref[0] = qkv[:, 2 * ATTN_WIDTH:].T.astype(BF16)
    g = _dot(h, wg_ref[...])
    gs_ref[...] = (g * jax.nn.sigmoid(g)).astype(BF16)


def _in_proj(x2, g_pre, mu, w_r, w_a, w_g, batch, seq):
    m, d = x2.shape
    nblk = m // TOK_TILE
    per_b = seq // TOK_TILE
    const = lambda i: (0, 0)
    return pl.pallas_call(
        functools.partial(_inproj_body, per_b),
        grid=(nblk,),
        in_specs=[
            pl.BlockSpec((TOK_TILE, d), lambda i: (i, 0)),
            pl.BlockSpec((1, d), const),
            pl.BlockSpec((1, N_RWKV_COLS), const),
            pl.BlockSpec(w_r.shape, const),
            pl.BlockSpec(w_a.shape, const),
            pl.BlockSpec(w_g.shape, const),
        ],
        out_specs=[
            pl.BlockSpec((TOK_TILE, N_RWKV_COLS), lambda i: (i, 0)),
            pl.BlockSpec((TOK_TILE, ATTN_WIDTH), lambda i: (i, 0)),
            pl.BlockSpec((TOK_TILE, ATTN_WIDTH), lambda i: (i, 0)),
            pl.BlockSpec((1, ATTN_WIDTH, TOK_TILE), lambda i: (i // per_b, 0, i % per_b)),
            pl.BlockSpec((1, 1, ATTN_WIDTH), lambda i: (i, 0, 0)),
            pl.BlockSpec((TOK_TILE, D_MIX), lambda i: (i, 0)),
        ],
        out_shape=[
            jax.ShapeDtypeStruct((m, N_RWKV_COLS), F32),
            jax.ShapeDtypeStruct((m, ATTN_WIDTH), F32),
            jax.ShapeDtypeStruct((m, ATTN_WIDTH), BF16),
            jax.ShapeDtypeStruct((batch, ATTN_WIDTH, seq), BF16),
            jax.ShapeDtypeStruct((nblk, 1, ATTN_WIDTH), F32),
            jax.ShapeDtypeStruct((m, D_MIX), BF16),
        ],
        scratch_shapes=[pltpu.VMEM((1, N_RWKV_COLS), F32)],
        compiler_params=pltpu.CompilerParams(
            dimension_semantics=("arbitrary",), vmem_limit_bytes=VMEM_LIMIT),
        name="in_proj",
    )(x2, g_pre, mu, w_r, w_a, w_g)


LOG2E = 1.4426950408889634
AUX_S, AUX_T, AUX_BLK, AUX_SEL = 0, 3, 6, 16
PV_ONES_ROWS = 16
MOBA_STEP_HEADS = 4


def _moba_key_aux(seq):
    pos = jnp.arange(seq, dtype=jnp.int32)[:, None]
    lane = jnp.arange(LANES, dtype=jnp.int32)[None, :]
    blk = pos // MOBA_BLOCK
    tab = jnp.where(lane < AUX_T, pos % MOBA_BLOCK,
                    jnp.where(lane < AUX_BLK, 1,
                              jnp.where(lane < AUX_BLK + 3, blk * MOBA_BLOCK,
                                        jnp.where(lane == AUX_SEL + blk, 1, 0))))
    return tab.astype(BF16)


def _split2(x):
    hi = x.astype(BF16)
    lo = (x - hi.astype(F32)).astype(BF16)
    return hi, lo


def _split3(x):
    x1 = x.astype(BF16).astype(F32)
    r1 = x - x1
    x2 = r1.astype(BF16).astype(F32)
    x3 = (r1 - x2).astype(BF16).astype(F32)
    return x1, x2, x3


def _moba_body(slopes_ref, q_ref, k_ref, ak_ref, vt_ref, km_ref, gs_ref, o_ref, st_ref, p_ref):
    BS = MOBA_BLOCK
    grp = pl.program_id(1)
    i = pl.program_id(2)
    heads = MOBA_STEP_HEADS
    per_pair = LANES // HEAD_DIM
    nb = km_ref.shape[1]
    n_iota = lax.broadcasted_iota(jnp.int32, (nb, BS), 0)
    causal = (lax.broadcasted_iota(jnp.int32, (BS, BS), 0)
              <= lax.broadcasted_iota(jnp.int32, (BS, BS), 1))
    t_glob = (lax.broadcasted_iota(jnp.int32, (nb, BS), 1) + i * BS).astype(F32)
    piece = n_iota % 3
    ones_rows = jnp.ones((PV_ONES_ROWS, BS), BF16)

    def key_rows(n):
        b = jnp.where(n == 0, i, jnp.minimum(n - 1, i))
        return pl.ds(pl.multiple_of(b * BS, BS), BS)

    def pair_lanes(h):
        p = h // per_pair
        return slice(p * LANES, (p + 1) * LANES)

    def scores(h, blk, q_aug_t):
        k_aug = jnp.concatenate([k_ref[blk, pair_lanes(h)], ak_ref[blk, :]], axis=1)
        return _dot(k_aug, q_aug_t)

    def pv(h, slot, blk):
        vt_aug = jnp.concatenate([vt_ref[0, h * HEAD_DIM:(h + 1) * HEAD_DIM, blk], ones_rows], axis=0)
        return _dot(vt_aug, p_ref[slot, h])

    q = q_ref[...]
    km = km_ref[0]
    lane_head = lax.broadcasted_iota(jnp.int32, (1, heads * HEAD_DIM), 1) // HEAD_DIM
    km_stack = jnp.concatenate([jnp.where(lane_head == h, km, 0.0) for h in range(heads)], axis=0)
    q_t = (q * LOG2E).T
    q_hi, q_lo = _split2(q_t)
    km_hi, km_lo = _split2(km_stack)
    gates = _dot(jnp.concatenate([km_hi, km_lo, km_hi], axis=1), jnp.concatenate([q_hi, q_hi, q_lo], axis=0))
    gate = [jnp.where(n_iota < i, gates[h * nb:(h + 1) * nb], NEG) for h in range(heads)]
    row_h = lax.broadcasted_iota(jnp.int32, (LANES, 1), 0) // HEAD_DIM
    zero_b = jnp.zeros((), BF16)
    qh_t, aux_top = [], []
    for h in range(heads):
        cvec = jnp.full((nb, BS), slopes_ref[grp * heads + h] * LOG2E, F32)
        c1, c2, c3 = _split3(cvec)
        e1, e2, e3 = _split3(-cvec * t_glob)
        cp = jnp.where(piece == 0, c1, jnp.where(piece == 1, c2, c3))
        ep = jnp.where(piece == 0, e1, jnp.where(piece == 1, e2, e3))
        top = jnp.where((n_iota >= AUX_T) & (n_iota < AUX_BLK), ep, cp)
        aux_top.append(jnp.where(n_iota < AUX_BLK + 3, top, 0.0).astype(BF16))
        qh_t.append(jnp.where(row_h == h % per_pair, q_hi[pair_lanes(h), :], zero_b))
    aux_pad = jnp.zeros((LANES - AUX_SEL - nb, BS), BF16)

    def q_aug(h, sel_rows):
        return jnp.concatenate([qh_t[h], aux_top[h], sel_rows, aux_pad], axis=0)

    no_mask = jnp.zeros((nb, BS), BF16)
    for h in range(heads):
        st_ref[0, h] = jnp.where(causal, scores(h, key_rows(0), q_aug(h, no_mask)), NEG)
        p_ref[0, h] = jnp.zeros((BS, BS), BF16)
    sel = [jnp.zeros((nb, BS), F32) for _ in range(heads)]
    for rnk in range(MOBA_TOPK):
        rank_ok = jnp.where(rnk < i, 1.0, 0.0)
        for h in range(heads):
            mx = jnp.max(gate[h], axis=0, keepdims=True)
            idx = jnp.min(jnp.where(gate[h] == mx, n_iota, nb), axis=0, keepdims=True)
            pick = n_iota == idx
            sel[h] = jnp.where(pick, rank_ok, sel[h])
            gate[h] = jnp.where(pick, -jnp.inf, gate[h])
    q_augs = [q_aug(h, jnp.where(sel[h] > 0.0, 0.0, NEG).astype(BF16)) for h in range(heads)]

    def stage(n, carry, src, dst):
        ms, accs, alphas = carry
        prev_rows = key_rows(jnp.maximum(n - 1, 0))
        next_rows = key_rows(n + 1)
        pvs = [pv(h, src, prev_rows) for h in range(heads)]
        ms2, alphas2 = [], []
        for h in range(heads):
            st = st_ref[src, h]
            m_new = jnp.maximum(ms[h], jnp.max(st, axis=0, keepdims=True))
            alphas2.append(jnp.exp2(ms[h] - m_new))
            p_ref[dst, h] = jnp.exp2(st - m_new).astype(BF16)
            ms2.append(m_new)
        for h in range(heads):
            st_ref[dst, h] = scores(h, next_rows, q_augs[h])
        accs2 = [alphas[h] * accs[h] + pvs[h] for h in range(heads)]
        return tuple(ms2), tuple(accs2), tuple(alphas2)

    def pair_step(t, carry):
        return stage(2 * t + 1, stage(2 * t, carry, 0, 1), 1, 0)

    def single_step(t, carry):
        return stage(i, carry, 0, 0)

    m0 = tuple(jnp.full((1, BS), NEG, F32) for _ in range(heads))
    acc0 = tuple(jnp.zeros((HEAD_DIM + PV_ONES_ROWS, BS), F32) for _ in range(heads))
    alpha0 = tuple(jnp.ones((1, BS), F32) for _ in range(heads))
    n_blocks = i + 1
    carry = lax.fori_loop(0, n_blocks // 2, pair_step, (m0, acc0, alpha0))
    ms, accs, alphas = lax.fori_loop(0, n_blocks % 2, single_step, carry)
    last_rows = key_rows(i)
    outs = []
    for h in range(heads):
        acc = alphas[h] * accs[h] + pv(h, 0, last_rows)
        outs.append(acc[:HEAD_DIM] / acc[HEAD_DIM:HEAD_DIM + 1])
    ot = jnp.concatenate(outs, axis=0)
    o_ref[...] = (ot.T * gs_ref[...].astype(F32)).astype(BF16)


def _moba(slopes, q, k, key_aux, vt, kmean, gs, batch, seq):
    m = q.shape[0]
    BS = MOBA_BLOCK
    nb = seq // BS
    assert nb == AUX_SEL and AUX_SEL + nb <= LANES and AUX_BLK + 3 <= AUX_SEL
    W = MOBA_STEP_HEADS * HEAD_DIM
    n_groups = ATTN_WIDTH // W
    gate_col0 = RWKV_WIDTH // W
    return pl.pallas_call(
        _moba_body,
        grid_spec=pltpu.PrefetchScalarGridSpec(
            num_scalar_prefetch=1,
            grid=(batch, n_groups, nb),
            in_specs=[
                pl.BlockSpec((BS, W), lambda b, g, i, s: (b * nb + i, g)),
                pl.BlockSpec((seq, W), lambda b, g, i, s: (b, g)),
                pl.BlockSpec((seq, LANES), lambda b, g, i, s: (0, 0)),
                pl.BlockSpec((1, W, seq), lambda b, g, i, s: (b, g, 0)),
                pl.BlockSpec((1, nb, W), lambda b, g, i, s: (b, 0, g)),
                pl.BlockSpec((BS, W), lambda b, g, i, s: (b * nb + i, gate_col0 + g)),
            ],
            out_specs=pl.BlockSpec((BS, W), lambda b, g, i, s: (b * nb + i, g)),
            scratch_shapes=[
                pltpu.VMEM((2, MOBA_STEP_HEADS, BS, BS), F32),
                pltpu.VMEM((2, MOBA_STEP_HEADS, BS, BS), BF16),
            ],
        ),
        out_shape=jax.ShapeDtypeStruct((m, ATTN_WIDTH), BF16),
        compiler_params=pltpu.CompilerParams(
            dimension_semantics=("arbitrary", "arbitrary", "arbitrary"), vmem_limit_bytes=VMEM_LIMIT),
        name="moba",
    )(slopes, q, k, key_aux, vt, kmean, gs)


def _split_dot(x, ones_b):
    hi, lo = _split2(x)
    return _dot(hi, ones_b) + _dot(lo, ones_b)


def _head_sum(x, ones_b, split=True):
    one = _split_dot if split else (lambda xs, o: _dot(xs.astype(BF16), o))
    parts = [one(x[:, g * GROUP_W:(g + 1) * GROUP_W], ones_b) for g in range(x.shape[1] // GROUP_W)]
    return jnp.concatenate(parts, axis=1)


def _rwkv_body(pr_ref, gs_ref, w0_ref, w2a_hi_ref, w2a_lo_ref, a0_ref, kk_ref, ka_ref, rk_ref, lw_ref,
               lb_ref, x_ref, ya_ref, wor_ref, woa_ref, gp_ref,
               o_ref, s_ref, at_ref, bt_ref, kt_ref, rt_ref, v_ref, gl_ref, y_ref, bonus_ref,
               q_ref, g_ref, h_ref, yr_ref):
    W = RWKV_WIDTH
    n_groups = W // GROUP_W
    n_chunks = RWKV_TILE // CHUNK

    @pl.when(pl.program_id(1) == 0)
    def _():
        s_ref[...] = jnp.zeros_like(s_ref)

    rr = lax.broadcasted_iota(jnp.int32, (GROUP_W, GROUP_W), 0)
    cc = lax.broadcasted_iota(jnp.int32, (GROUP_W, GROUP_W), 1)
    same_head = (rr // HEAD_DIM) == (cc // HEAD_DIM)
    ones_b = same_head.astype(BF16)
    eye_g = rr == cc
    pr_ = lax.broadcasted_iota(jnp.int32, (2 * CHUNK, 4 * CHUNK), 0)
    pc_ = lax.broadcasted_iota(jnp.int32, (2 * CHUNK, 4 * CHUNK), 1)
    tril_pair = ((pr_ // CHUNK == pc_ // (2 * CHUNK)) & (pc_ % CHUNK <= pr_ % CHUNK)).astype(BF16)
    wr = lax.broadcasted_iota(jnp.int32, (CHUNK, GROUP_W), 0)
    wc = lax.broadcasted_iota(jnp.int32, (CHUNK, GROUP_W), 1) % CHUNK
    strict_w = wc < wr
    incl_w = wc <= wr
    eye_w = (wc == wr).astype(F32)
    lane_g = lax.broadcasted_iota(jnp.int32, (1, GROUP_W), 1) // HEAD_DIM
    head_masks = [lane_g == h for h in range(GROUP_HEADS)]

    def mstack(x):
        xb = x.astype(BF16)
        zero = jnp.zeros_like(xb)
        return jnp.concatenate([jnp.where(hm, xb, zero) for hm in head_masks], axis=0)

    def block_diag(xw):
        return jnp.where(same_head, jnp.concatenate([xw] * GROUP_HEADS, axis=0), jnp.zeros((), BF16))

    HALF = 2 * CHUNK
    SUB_CHUNKS = 4
    RING_CHUNKS = RWKV_RING // CHUNK

    def ring_rows(c, n=1):
        r0 = (c % RING_CHUNKS) * CHUNK
        return slice(r0, r0 + n * CHUNK)

    def ring_slot(c, g):
        return (c % RING_CHUNKS) * n_groups + g

    lane_z = lax.broadcasted_iota(jnp.int32, (1, DECAY_RANK + ICLR_RANK), 1)

    def run(main, side=None, side_start=0):
        gens = [main] + ([side] if side is not None else [])
        stage_no = 0
        while gens:
            for gen in list(gens):
                if gen is side and stage_no < side_start and main in gens:
                    continue
                try:
                    next(gen)
                except StopIteration:
                    gens.remove(gen)
            stage_no += 1

    def seq(*gens):
        for gen in gens:
            yield from gen

    def prep(half):
        hr = slice(half * HALF, (half + 1) * HALF)
        rr_ = ring_rows(2 * half, 2)
        z = pr_ref[hr, 3 * W:3 * W + DECAY_RANK + ICLR_RANK]
        z = jnp.where(lane_z < DECAY_RANK, jnp.tanh(z), z)
        z_hi, z_lo = _split2(z)
        lin = _dot(jnp.concatenate([z_hi, z_lo], axis=1), w2a_hi_ref[...]) + _dot(z_hi, w2a_lo_ref[...])
        k = pr_ref[hr, W:2 * W]
        kkr = k * kk_ref[...]
        ss = _head_sum(kkr * kkr, ones_b, split=False)
        yield
        ld = -EXP_M_HALF * jax.nn.sigmoid(w0_ref[...] + lin[:, :W])
        ld_hi, ld_lo = _split2(ld)
        cs = _dot(tril_pair, jnp.concatenate([ld_hi[:CHUNK], ld_lo[:CHUNK], ld_hi[CHUNK:], ld_lo[CHUNK:]], axis=0))
        a = jax.nn.sigmoid(a0_ref[...] + lin[:, W:])
        k2 = k * (1.0 + (a - 1.0) * ka_ref[...])
        r = pr_ref[hr, 0:W]
        bsum = _head_sum(r * k2 * rk_ref[...], ones_b)
        yield
        kk = kkr * jnp.minimum(lax.rsqrt(ss), 1e12)
        g_inv = jnp.exp(-cs)
        at_ref[rr_, :] = (-kk) * jnp.exp(cs - ld)
        bt_ref[rr_, :] = kk * a * g_inv
        kt_ref[rr_, :] = k2 * g_inv
        yield
        g_incl = jnp.exp(cs)
        v = pr_ref[hr, 2 * W:3 * W]
        rt_ref[rr_, :] = r * g_incl
        v_ref[rr_, :] = v
        gl_ref[rr_, :] = g_incl
        bonus_ref[hr, :] = bsum * v
        yield

    def chain_factors(batch_chains):
        nc = range(len(batch_chains))
        rows = [ring_rows(c) for c, _ in batch_chains]
        yrows = [slice(c * CHUNK, (c + 1) * CHUNK) for c, _ in batch_chains]
        cols = [slice(g * GROUP_W, (g + 1) * GROUP_W) for _, g in batch_chains]
        x = []
        for j in nc:
            lhs = jnp.concatenate([at_ref[rows[j], cols[j]], rt_ref[rows[j], cols[j]]], axis=0).astype(BF16)
            rhs = jnp.concatenate([mstack(bt_ref[rows[j], cols[j]]), mstack(kt_ref[rows[j], cols[j]])], axis=0)
            x.append(_dot(lhs, rhs, _NT))
        yield
        a_ab = [jnp.where(strict_w, x[j][:CHUNK, :GROUP_W], 0.0) for j in nc]
        a_ak = [jnp.where(strict_w, x[j][:CHUNK, GROUP_W:], 0.0).astype(BF16) for j in nc]
        m_rb = [jnp.where(incl_w, x[j][CHUNK:, :GROUP_W], 0.0).astype(BF16) for j in nc]
        m_rk = [jnp.where(incl_w, x[j][CHUNK:, GROUP_W:], 0.0).astype(BF16) for j in nc]
        tw = [a_ab[j] + eye_w for j in nc]
        pw = [a_ab[j].astype(BF16) for j in nc]
        pw = [_dot(pw[j], block_diag(pw[j])).astype(BF16) for j in nc]
        yield
        for it in range(CHUNK.bit_length() - 3):
            both = [_dot(jnp.concatenate([pw[j], tw[j].astype(BF16)], axis=0), block_diag(pw[j])) for j in nc]
            tw = [tw[j] + both[j][CHUNK:] for j in nc]
            pw = [both[j][:CHUNK].astype(BF16) for j in nc]
            yield
        tw = [tw[j] + _dot(tw[j].astype(BF16), block_diag(pw[j])) for j in nc]
        twb = [tw[j].astype(BF16) for j in nc]
        yield
        v_m = [mstack(v_ref[rows[j], cols[j]]) for j in nc]
        wt = [_dot(twb[j], mstack(at_ref[rows[j], cols[j]])) for j in nc]
        x1 = [_dot(a_ak[j], v_m[j]) for j in nc]
        yield
        u0 = [_dot(twb[j], mstack(x1[j])) for j in nc]
        for j in nc:
            q_ref[rows[j], cols[j]] = (rt_ref[rows[j], cols[j]] + _dot(m_rb[j], mstack(wt[j]))).astype(BF16)
        yield
        for j in nc:
            y_ref[yrows[j], cols[j]] = _dot(m_rb[j], mstack(u0[j])) + _dot(m_rk[j], v_m[j])
        yield
        for j in nc:
            c, g = batch_chains[j]
            gl = gl_ref[rows[j].stop - 1:rows[j].stop, cols[j]]
            bh = (bt_ref[rows[j], cols[j]] * gl).astype(BF16)
            kh = (kt_ref[rows[j], cols[j]] * gl).astype(BF16)
            gm = jnp.where(eye_g, gl, 0.0) + jnp.where(same_head, _dot(bh, wt[j].astype(BF16), _TN), 0.0)
            g_ref[ring_slot(c, g)] = gm.astype(BF16)
            h_ref[ring_slot(c, g)] = jnp.where(
                same_head,
                _dot(jnp.concatenate([bh, kh], axis=0),
                     jnp.concatenate([u0[j], v_ref[rows[j], cols[j]]], axis=0).astype(BF16), _TN),
                0.0)
        yield

    def recurrence(c):
        rows = slice(c * CHUNK, (c + 1) * CHUNK)
        for g in range(n_groups):
            cols = slice(g * GROUP_W, (g + 1) * GROUP_W)
            s_b = s_ref[g].astype(BF16)
            y_ref[rows, cols] = y_ref[rows, cols] + _dot(q_ref[ring_rows(c), cols], s_b)
            s_ref[g] = _dot(g_ref[ring_slot(c, g)], s_b) + h_ref[ring_slot(c, g)]
        yield

    def finish(half):
        hr = slice(half * HALF, (half + 1) * HALF)
        y = y_ref[hr, :]
        mean = _head_sum(y, ones_b) * (1.0 / HEAD_DIM)
        yield
        yc = y - mean
        var = _head_sum(yc * yc, ones_b) * (1.0 / HEAD_DIM)
        yield
        yn = yc * lax.rsqrt(var + GN_EPS) * lw_ref[...] + lb_ref[...] + bonus_ref[hr, :]
        yr_ref[hr, :] = (yn * gs_ref[hr, :].astype(F32)).astype(BF16)
        yield

    def project(sub):
        sr = slice(sub * SUB_CHUNKS * CHUNK, (sub + 1) * SUB_CHUNKS * CHUNK)
        o = _dot(yr_ref[sr, :], wor_ref[...]) + _dot(ya_ref[sr, :], woa_ref[...])
        yield
        ms = jnp.mean(o * o, axis=-1, keepdims=True)
        o_ref[sr, :] = x_ref[sr, :] + o * lax.rsqrt(ms + RMS_EPS) * gp_ref[...]
        yield

    n_sub = n_chunks // SUB_CHUNKS
    halves = lambda sub: (2 * sub, 2 * sub + 1)
    chunks = lambda sub: range(SUB_CHUNKS * sub, SUB_CHUNKS * (sub + 1))
    tail = lambda sub: ([recurrence(c) for c in chunks(sub)] + [finish(hf) for hf in halves(sub)]
                        + [project(sub)])
    run(seq(*[prep(hf) for hf in halves(0)]))
    for sub in range(n_sub):
        side = [prep(hf) for hf in halves(sub + 1)] if sub + 1 < n_sub else []
        side += tail(sub - 1) if sub > 0 else []
        run(chain_factors([(c, g) for c in chunks(sub) for g in range(n_groups)]), seq(*side), side_start=2)
    run(seq(*tail(n_sub - 1)))


def _rwkv_out(pr, gs, w0, w2a, a0, k_k, k_a, r_k, lnx_w, lnx_b, x2, ya, w_or, w_oa, g_post, batch, seq):
    m, d = x2.shape
    per_b = seq // RWKV_TILE
    W = RWKV_WIDTH
    n_cg = (RWKV_RING // CHUNK) * (W // GROUP_W)
    w2a_hi = w2a.astype(BF16)
    w2a_lo = (w2a - w2a_hi.astype(F32)).astype(BF16)
    w2a_hi = jnp.concatenate([w2a_hi, w2a_hi], axis=0)
    tile = lambda b, t: (b * per_b + t, 0)
    const = lambda b, t: (0, 0)
    vec = pl.BlockSpec((1, W), const)
    ring_scratch = pltpu.VMEM((RWKV_RING, W), F32)
    tile_scratch = pltpu.VMEM((RWKV_TILE, W), F32)
    return pl.pallas_call(
        _rwkv_body,
        grid=(batch, per_b),
        in_specs=[
            pl.BlockSpec((RWKV_TILE, N_RWKV_COLS), tile),
            pl.BlockSpec((RWKV_TILE, W), tile),
            vec,
            pl.BlockSpec(w2a_hi.shape, const),
            pl.BlockSpec(w2a_lo.shape, const),
            vec, vec, vec, vec, vec, vec,
            pl.BlockSpec((RWKV_TILE, d), tile),
            pl.BlockSpec((RWKV_TILE, ATTN_WIDTH), tile),
            pl.BlockSpec(w_or.shape, const),
            pl.BlockSpec(w_oa.shape, const),
            pl.BlockSpec((1, d), const),
        ],
        out_specs=pl.BlockSpec((RWKV_TILE, d), tile),
        out_shape=jax.ShapeDtypeStruct((m, d), F32),
        scratch_shapes=[
            pltpu.VMEM((W // GROUP_W, GROUP_W, GROUP_W), F32),
            ring_scratch, ring_scratch, ring_scratch, ring_scratch, ring_scratch, ring_scratch,
            tile_scratch, tile_scratch,
            pltpu.VMEM((RWKV_RING, W), BF16),
            pltpu.VMEM((n_cg, GROUP_W, GROUP_W), BF16),
            pltpu.VMEM((n_cg, GROUP_W, GROUP_W), F32),
            pltpu.VMEM((RWKV_TILE, W), BF16),
        ],
        compiler_params=pltpu.CompilerParams(
            dimension_semantics=("arbitrary", "arbitrary"), vmem_limit_bytes=VMEM_LIMIT),
        name="rwkv7_out",
    )(pr, gs, w0, w2a_hi, w2a_lo, a0, k_k, k_a, r_k, lnx_w, lnx_b, x2, ya, w_or, w_oa, g_post)


def kernel(x, g_pre, w_in, tshift_mu, w0, w2, a0, a2, k_k, k_a, r_k, lnx_w, lnx_b, w_out, g_post):
    batch, seq, d = x.shape
    depth = w_in.shape[0]
    assert seq % RWKV_TILE == 0 and RWKV_TILE % TOK_TILE == 0 and TOK_TILE == MOBA_BLOCK and d % LANES == 0
    x2 = x.reshape(batch * seq, d)
    slopes = 2.0 ** (-8.0 * jnp.arange(1, ATTN_HEADS + 1, dtype=F32) / ATTN_HEADS)
    row = lambda z: z.reshape(1, -1).astype(F32)
    for l in range(depth):
        wb = w_in[l].astype(BF16)
        w_r = wb[:, :N_RWKV_COLS]
        w_a = wb[:, N_RWKV_COLS:N_RWKV_COLS + N_ATTN_COLS]
        w_g = wb[:, N_RWKV_COLS + N_ATTN_COLS:]
        pr, q, k, vt, kmean, gs = _in_proj(x2, row(g_pre[l]), row(tshift_mu[l]), w_r, w_a, w_g, batch, seq)
        ya = _moba(slopes, q, k, _moba_key_aux(seq), vt,
                   kmean.reshape(batch, seq // MOBA_BLOCK, ATTN_WIDTH), gs, batch, seq)
        zeros = jnp.zeros((DECAY_RANK, RWKV_WIDTH), F32)
        w2a = jnp.concatenate([jnp.concatenate([w2[l], zeros], axis=1),
                               jnp.concatenate([zeros, a2[l]], axis=1)], axis=0)
        wo = w_out[l].astype(BF16)
        x2 = _rwkv_out(pr, gs, row(w0[l]), w2a, row(a0[l]), row(k_k[l]), row(k_a[l]), row(r_k[l]),
                       row(lnx_w[l]), row(lnx_b[l]), x2, ya, wo[:RWKV_WIDTH], wo[RWKV_WIDTH:],
                       row(g_post[l]), batch, seq)
    return x2.reshape(batch, seq, d)
```

```python
import functools

import jax
import jax.numpy as jnp
from jax import lax
from jax.experimental import pallas as pl
from jax.experimental.pallas import tpu as pltpu

F32 = jnp.float32
BF16 = jnp.bfloat16

HEAD_DIM = 64
RWKV_HEADS = 8
ATTN_HEADS = 8
RWKV_WIDTH = RWKV_HEADS * HEAD_DIM
ATTN_WIDTH = ATTN_HEADS * HEAD_DIM
D_MIX = RWKV_WIDTH + ATTN_WIDTH
DECAY_RANK = 64
ICLR_RANK = 64
MOBA_BLOCK = 256
MOBA_TOPK = 3
RMS_EPS = 1e-6
GN_EPS = 64e-5
EXP_M_HALF = 0.6065306597126334
NEG = -1e30
N_RWKV_COLS = 3 * RWKV_WIDTH + DECAY_RANK + ICLR_RANK
N_ATTN_COLS = 3 * ATTN_WIDTH

LANES = 128
MXU_DIM = 256
CHUNK = 64
GROUP_HEADS = MXU_DIM // HEAD_DIM
GROUP_W = GROUP_HEADS * HEAD_DIM
IN_TILE = 512
RWKV_TILE = 512
RWKV_RING = 512
VMEM_LIMIT = 48 * 1024 * 1024

_NT = (((1,), (1,)), ((), ()))
_TN = (((0,), (0,)), ((), ()))


def _dot(a, b, dims=None, precision=None):
    if dims is None:
        return jnp.dot(a, b, preferred_element_type=F32, precision=precision)
    return lax.dot_general(a, b, dims, preferred_element_type=F32, precision=precision)


def _inproj_body(per_b, x_ref, g_ref, mu_ref, wr_ref, wa_ref, wg_ref,
                 pr_ref, q_ref, k_ref, vt_ref, km_ref, gs_ref, prev_ref):
    x = x_ref[...]
    ms = jnp.mean(x * x, axis=-1, keepdims=True)
    h = (x * lax.rsqrt(ms + RMS_EPS) * g_ref[...]).astype(BF16)
    @pl.when(pl.program_id(0) % per_b == 0)
    def _():
        prev_ref[...] = jnp.zeros_like(prev_ref)

    row0 = lax.broadcasted_iota(jnp.int32, (IN_TILE, 1), 0) == 0
    for c0 in range(0, N_RWKV_COLS, MXU_DIM):
        cs = slice(c0, min(c0 + MXU_DIM, N_RWKV_COLS))
        p = _dot(h, wr_ref[:, cs])
        p_prev = jnp.where(row0, prev_ref[:, cs], pltpu.roll(p, 1, axis=0))
        prev_ref[:, cs] = p[IN_TILE - 1:IN_TILE, :]
        pr_ref[:, cs] = p + (p_prev - p) * mu_ref[:, cs]
    qkv = _dot(h, wa_ref[...])
    q_ref[...] = qkv[:, :ATTN_WIDTH] * (HEAD_DIM ** -0.5)
    kf = qkv[:, ATTN_WIDTH:2 * ATTN_WIDTH]
    k_ref[...] = kf.astype(BF16)
    for j in range(IN_TILE // MOBA_BLOCK):
        km_ref[j] = jnp.mean(kf[j * MOBA_BLOCK:(j + 1) * MOBA_BLOCK], axis=0, keepdims=True)
    vt_ref[0] = qkv[:, 2 * ATTN_WIDTH:].T.astype(BF16)
    g = _dot(h, wg_ref[...])
    gs_ref[...] = (g * jax.nn.sigmoid(g)).astype(BF16)


def _in_proj(x2, g_pre, mu, w_r, w_a, w_g, batch, seq):
    m, d = x2.shape
    nblk = m // IN_TILE
    per_b = seq // IN_TILE
    blocks = IN_TILE // MOBA_BLOCK
    const = lambda i: (0, 0)
    return pl.pallas_call(
        functools.partial(_inproj_body, per_b),
        grid=(nblk,),
        in_specs=[
            pl.BlockSpec((IN_TILE, d), lambda i: (i, 0)),
            pl.BlockSpec((1, d), const),
            pl.BlockSpec((1, N_RWKV_COLS), const),
            pl.BlockSpec(w_r.shape, const),
            pl.BlockSpec(w_a.shape, const),
            pl.BlockSpec(w_g.shape, const),
        ],
        out_specs=[
            pl.BlockSpec((IN_TILE, N_RWKV_COLS), lambda i: (i, 0)),
            pl.BlockSpec((IN_TILE, ATTN_WIDTH), lambda i: (i, 0)),
            pl.BlockSpec((IN_TILE, ATTN_WIDTH), lambda i: (i, 0)),
            pl.BlockSpec((1, ATTN_WIDTH, IN_TILE), lambda i: (i // per_b, 0, i % per_b)),
            pl.BlockSpec((blocks, 1, ATTN_WIDTH), lambda i: (i, 0, 0)),
            pl.BlockSpec((IN_TILE, D_MIX), lambda i: (i, 0)),
        ],
        out_shape=[
            jax.ShapeDtypeStruct((m, N_RWKV_COLS), F32),
            jax.ShapeDtypeStruct((m, ATTN_WIDTH), F32),
            jax.ShapeDtypeStruct((m, ATTN_WIDTH), BF16),
            jax.ShapeDtypeStruct((batch, ATTN_WIDTH, seq), BF16),
            jax.ShapeDtypeStruct((m // MOBA_BLOCK, 1, ATTN_WIDTH), F32),
            jax.ShapeDtypeStruct((m, D_MIX), BF16),
        ],
        scratch_shapes=[pltpu.VMEM((1, N_RWKV_COLS), F32)],
        compiler_params=pltpu.CompilerParams(
            dimension_semantics=("arbitrary",), vmem_limit_bytes=VMEM_LIMIT),
        name="in_proj",
    )(x2, g_pre, mu, w_r, w_a, w_g)


LOG2E = 1.4426950408889634
AUX_S, AUX_T, AUX_BLK, AUX_SEL = 0, 3, 6, 16
PV_ONES_ROWS = 16
MOBA_STEP_HEADS = 4
MOBA_STEP_BLOCKS = 2


def _moba_key_aux(seq):
    pos = jnp.arange(seq, dtype=jnp.int32)[:, None]
    lane = jnp.arange(LANES, dtype=jnp.int32)[None, :]
    blk = pos // MOBA_BLOCK
    tab = jnp.where(lane < AUX_T, pos % MOBA_BLOCK,
                    jnp.where(lane < AUX_BLK, 1,
                              jnp.where(lane < AUX_BLK + 3, blk * MOBA_BLOCK,
                                        jnp.where(lane == AUX_SEL + blk, 1, 0))))
    return tab.astype(BF16)


def _split2(x):
    hi = x.astype(BF16)
    lo = (x - hi.astype(F32)).astype(BF16)
    return hi, lo


def _split3(x):
    x1 = x.astype(BF16).astype(F32)
    r1 = x - x1
    x2 = r1.astype(BF16).astype(F32)
    x3 = (r1 - x2).astype(BF16).astype(F32)
    return x1, x2, x3


def _moba_body(slopes_ref, q_ref, k_ref, ak_ref, vt_ref, km_ref, gs_ref, o_ref, st_ref, p_ref):
    BS = MOBA_BLOCK
    grp = pl.program_id(1)
    step = pl.program_id(2)
    heads = MOBA_STEP_HEADS
    per_pair = LANES // HEAD_DIM
    nb = km_ref.shape[1]
    n_iota = lax.broadcasted_iota(jnp.int32, (nb, BS), 0)
    causal = (lax.broadcasted_iota(jnp.int32, (BS, BS), 0)
              <= lax.broadcasted_iota(jnp.int32, (BS, BS), 1))
    piece = n_iota % 3
    ones_rows = jnp.ones((PV_ONES_ROWS, BS), BF16)
    lane_head = lax.broadcasted_iota(jnp.int32, (1, heads * HEAD_DIM), 1) // HEAD_DIM
    row_h = lax.broadcasted_iota(jnp.int32, (LANES, 1), 0) // HEAD_DIM
    zero_b = jnp.zeros((), BF16)
    aux_pad = jnp.zeros((LANES - AUX_SEL - nb, BS), BF16)
    no_mask = jnp.zeros((nb, BS), BF16)
    km = km_ref[0]
    km_stack = jnp.concatenate([jnp.where(lane_head == h, km, 0.0) for h in range(heads)], axis=0)
    km_hi, km_lo = _split2(km_stack)
    km_cat = jnp.concatenate([km_hi, km_lo, km_hi], axis=1)

    def pair_lanes(h):
        p = h // per_pair
        return slice(p * LANES, (p + 1) * LANES)

    def query_block(u):
        i = MOBA_STEP_BLOCKS * step + u
        q_rows = slice(u * BS, (u + 1) * BS)

        def key_rows(n):
            b = jnp.where(n == 0, i, jnp.minimum(n - 1, i))
            return pl.ds(pl.multiple_of(b * BS, BS), BS)

        def scores(h, blk, q_aug_t):
            k_aug = jnp.concatenate([k_ref[blk, pair_lanes(h)], ak_ref[blk, :]], axis=1)
            return _dot(k_aug, q_aug_t)

        def pv(h, slot, blk):
            vt_aug = jnp.concatenate([vt_ref[0, h * HEAD_DIM:(h + 1) * HEAD_DIM, blk], ones_rows], axis=0)
            return _dot(vt_aug, p_ref[u, slot, h])

        def prologue():
            q_t = (q_ref[q_rows, :] * LOG2E).T
            q_hi, q_lo = _split2(q_t)
            gates = _dot(km_cat, jnp.concatenate([q_hi, q_hi, q_lo], axis=0))
            gate = [jnp.where(n_iota < i, gates[h * nb:(h + 1) * nb], NEG) for h in range(heads)]
            t_glob = (lax.broadcasted_iota(jnp.int32, (nb, BS), 1) + i * BS).astype(F32)
            qh_t, aux_top = [], []
            for h in range(heads):
                cvec = jnp.full((nb, BS), slopes_ref[grp * heads + h] * LOG2E, F32)
                c1, c2, c3 = _split3(cvec)
                e1, e2, e3 = _split3(-cvec * t_glob)
                cp = jnp.where(piece == 0, c1, jnp.where(piece == 1, c2, c3))
                ep = jnp.where(piece == 0, e1, jnp.where(piece == 1, e2, e3))
                top = jnp.where((n_iota >= AUX_T) & (n_iota < AUX_BLK), ep, cp)
                aux_top.append(jnp.where(n_iota < AUX_BLK + 3, top, 0.0).astype(BF16))
                qh_t.append(jnp.where(row_h == h % per_pair, q_hi[pair_lanes(h), :], zero_b))

            def q_aug(h, sel_rows):
                return jnp.concatenate([qh_t[h], aux_top[h], sel_rows, aux_pad], axis=0)

            for h in range(heads):
                st_ref[u, 0, h] = jnp.where(causal, scores(h, key_rows(0), q_aug(h, no_mask)), NEG)
                p_ref[u, 0, h] = jnp.zeros((BS, BS), BF16)
            sel = [jnp.zeros((nb, BS), F32) for _ in range(heads)]
            for rnk in range(MOBA_TOPK):
                rank_ok = jnp.where(rnk < i, 1.0, 0.0)
                for h in range(heads):
                    mx = jnp.max(gate[h], axis=0, keepdims=True)
                    idx = jnp.min(jnp.where(gate[h] == mx, n_iota, nb), axis=0, keepdims=True)
                    pick = n_iota == idx
                    sel[h] = jnp.where(pick, rank_ok, sel[h])
                    gate[h] = jnp.where(pick, -jnp.inf, gate[h])
            return [q_aug(h, jnp.where(sel[h] > 0.0, 0.0, NEG).astype(BF16)) for h in range(heads)]

        def loops(q_augs):
            def stage(n, carry, src, dst):
                ms, accs, alphas = carry
                prev_rows = key_rows(jnp.maximum(n - 1, 0))
                next_rows = key_rows(n + 1)
                pvs = [pv(h, src, prev_rows) for h in range(heads)]
                ms2, alphas2 = [], []
                for h in range(heads):
                    st = st_ref[u, src, h]
                    m_new = jnp.maximum(ms[h], jnp.max(st, axis=0, keepdims=True))
                    alphas2.append(jnp.exp2(ms[h] - m_new))
                    p_ref[u, dst, h] = jnp.exp2(st - m_new).astype(BF16)
                    ms2.append(m_new)
                for h in range(heads):
                    st_ref[u, dst, h] = scores(h, next_rows, q_augs[h])
                accs2 = [alphas[h] * accs[h] + pvs[h] for h in range(heads)]
                return tuple(ms2), tuple(accs2), tuple(alphas2)

            def pair_step(t, carry):
                return stage(2 * t + 1, stage(2 * t, carry, 0, 1), 1, 0)

            def single_step(t, carry):
                return stage(i, carry, 0, 0)

            m0 = tuple(jnp.full((1, BS), NEG, F32) for _ in range(heads))
            acc0 = tuple(jnp.zeros((HEAD_DIM + PV_ONES_ROWS, BS), F32) for _ in range(heads))
            alpha0 = tuple(jnp.ones((1, BS), F32) for _ in range(heads))
            n_blocks = i + 1
            carry = lax.fori_loop(0, n_blocks // 2, pair_step, (m0, acc0, alpha0))
            return lax.fori_loop(0, n_blocks % 2, single_step, carry)

        def epilogue(carry):
            ms, accs, alphas = carry
            last_rows = key_rows(i)
            outs = []
            for h in range(heads):
                acc = alphas[h] * accs[h] + pv(h, 0, last_rows)
                outs.append(acc[:HEAD_DIM] / acc[HEAD_DIM:HEAD_DIM + 1])
            ot = jnp.concatenate(outs, axis=0)
            o_ref[q_rows, :] = (ot.T * gs_ref[q_rows, :].astype(F32)).astype(BF16)

        return prologue, loops, epilogue

    parts = [query_block(u) for u in range(MOBA_STEP_BLOCKS)]
    carry = None
    for u, (prologue, loops, epilogue) in enumerate(parts):
        q_augs = prologue()
        if u > 0:
            parts[u - 1][2](carry)
        carry = loops(q_augs)
    parts[-1][2](carry)


def _moba(slopes, q, k, key_aux, vt, kmean, gs, batch, seq):
    m = q.shape[0]
    BS = MOBA_BLOCK
    nb = seq // BS
    assert nb == AUX_SEL and AUX_SEL + nb <= LANES and AUX_BLK + 3 <= AUX_SEL and nb % MOBA_STEP_BLOCKS == 0
    H, U = MOBA_STEP_HEADS, MOBA_STEP_BLOCKS
    W = H * HEAD_DIM
    n_groups = ATTN_WIDTH // W
    n_steps = nb // U
    gate_col0 = RWKV_WIDTH // W
    return pl.pallas_call(
        _moba_body,
        grid_spec=pltpu.PrefetchScalarGridSpec(
            num_scalar_prefetch=1,
            grid=(batch, n_groups, n_steps),
            in_specs=[
                pl.BlockSpec((U * BS, W), lambda b, g, i, s: (b * n_steps + i, g)),
                pl.BlockSpec((seq, W), lambda b, g, i, s: (b, g)),
                pl.BlockSpec((seq, LANES), lambda b, g, i, s: (0, 0)),
                pl.BlockSpec((1, W, seq), lambda b, g, i, s: (b, g, 0)),
                pl.BlockSpec((1, nb, W), lambda b, g, i, s: (b, 0, g)),
                pl.BlockSpec((U * BS, W), lambda b, g, i, s: (b * n_steps + i, gate_col0 + g)),
            ],
            out_specs=pl.BlockSpec((U * BS, W), lambda b, g, i, s: (b * n_steps + i, g)),
            scratch_shapes=[
                pltpu.VMEM((U, 2, H, BS, BS), F32),
                pltpu.VMEM((U, 2, H, BS, BS), BF16),
            ],
        ),
        out_shape=jax.ShapeDtypeStruct((m, ATTN_WIDTH), BF16),
        compiler_params=pltpu.CompilerParams(
            dimension_semantics=("arbitrary", "arbitrary", "arbitrary"), vmem_limit_bytes=VMEM_LIMIT),
        name="moba",
    )(slopes, q, k, key_aux, vt, kmean, gs)


def _split_dot(x, ones_b):
    hi, lo = _split2(x)
    return _dot(hi, ones_b) + _dot(lo, ones_b)


def _head_sum(x, ones_b, split=True):
    one = _split_dot if split else (lambda xs, o: _dot(xs.astype(BF16), o))
    parts = [one(x[:, g * GROUP_W:(g + 1) * GROUP_W], ones_b) for g in range(x.shape[1] // GROUP_W)]
    return jnp.concatenate(parts, axis=1)


def _rwkv_body(pr_ref, gs_ref, w0_ref, w2a_hi_ref, w2a_lo_ref, a0_ref, kk_ref, ka_ref, rk_ref, lw_ref,
               lb_ref, x_ref, ya_ref, wor_ref, woa_ref, gp_ref,
               o_ref, s_ref, at_ref, bt_ref, kt_ref, rt_ref, v_ref, gl_ref, y_ref, bonus_ref,
               q_ref, g_ref, h_ref, yr_ref):
    W = RWKV_WIDTH
    n_groups = W // GROUP_W
    n_chunks = RWKV_TILE // CHUNK

    @pl.when(pl.program_id(1) == 0)
    def _():
        s_ref[...] = jnp.zeros_like(s_ref)

    rr = lax.broadcasted_iota(jnp.int32, (GROUP_W, GROUP_W), 0)
    cc = lax.broadcasted_iota(jnp.int32, (GROUP_W, GROUP_W), 1)
    same_head = (rr // HEAD_DIM) == (cc // HEAD_DIM)
    ones_b = same_head.astype(BF16)
    eye_g = rr == cc
    pr_ = lax.broadcasted_iota(jnp.int32, (2 * CHUNK, 4 * CHUNK), 0)
    pc_ = lax.broadcasted_iota(jnp.int32, (2 * CHUNK, 4 * CHUNK), 1)
    tril_pair = ((pr_ // CHUNK == pc_ // (2 * CHUNK)) & (pc_ % CHUNK <= pr_ % CHUNK)).astype(BF16)
    wr = lax.broadcasted_iota(jnp.int32, (CHUNK, GROUP_W), 0)
    wc = lax.broadcasted_iota(jnp.int32, (CHUNK, GROUP_W), 1) % CHUNK
    strict_w = wc < wr
    incl_w = wc <= wr
    eye_w = (wc == wr).astype(F32)
    lane_g = lax.broadcasted_iota(jnp.int32, (1, GROUP_W), 1) // HEAD_DIM
    head_masks = [lane_g == h for h in range(GROUP_HEADS)]

    def mstack(x):
        xb = x.astype(BF16)
        zero = jnp.zeros_like(xb)
        return jnp.concatenate([jnp.where(hm, xb, zero) for hm in head_masks], axis=0)

    def block_diag(xw):
        return jnp.where(same_head, jnp.concatenate([xw] * GROUP_HEADS, axis=0), jnp.zeros((), BF16))

    HALF = 2 * CHUNK
    SUB_CHUNKS = 4
    RING_CHUNKS = RWKV_RING // CHUNK

    def ring_rows(c, n=1):
        r0 = (c % RING_CHUNKS) * CHUNK
        return slice(r0, r0 + n * CHUNK)

    def ring_slot(c, g):
        return (c % RING_CHUNKS) * n_groups + g

    lane_z = lax.broadcasted_iota(jnp.int32, (1, DECAY_RANK + ICLR_RANK), 1)

    def run(main, side=None, side_start=0):
        gens = [main] + ([side] if side is not None else [])
        stage_no = 0
        while gens:
            for gen in list(gens):
                if gen is side and stage_no < side_start and main in gens:
                    continue
                try:
                    next(gen)
                except StopIteration:
                    gens.remove(gen)
            stage_no += 1

    def seq(*gens):
        for gen in gens:
            yield from gen

    def prep(half):
        hr = slice(half * HALF, (half + 1) * HALF)
        rr_ = ring_rows(2 * half, 2)
        z = pr_ref[hr, 3 * W:3 * W + DECAY_RANK + ICLR_RANK]
        z = jnp.where(lane_z < DECAY_RANK, jnp.tanh(z), z)
        z_hi, z_lo = _split2(z)
        lin = _dot(jnp.concatenate([z_hi, z_lo], axis=1), w2a_hi_ref[...]) + _dot(z_hi, w2a_lo_ref[...])
        k = pr_ref[hr, W:2 * W]
        kkr = k * kk_ref[...]
        ss = _head_sum(kkr * kkr, ones_b, split=False)
        yield
        ld = -EXP_M_HALF * jax.nn.sigmoid(w0_ref[...] + lin[:, :W])
        ld_hi, ld_lo = _split2(ld)
        cs = _dot(tril_pair, jnp.concatenate([ld_hi[:CHUNK], ld_lo[:CHUNK], ld_hi[CHUNK:], ld_lo[CHUNK:]], axis=0))
        a = jax.nn.sigmoid(a0_ref[...] + lin[:, W:])
        k2 = k * (1.0 + (a - 1.0) * ka_ref[...])
        r = pr_ref[hr, 0:W]
        bsum = _head_sum(r * k2 * rk_ref[...], ones_b)
        yield
        kk = kkr * jnp.minimum(lax.rsqrt(ss), 1e12)
        g_inv = jnp.exp(-cs)
        at_ref[rr_, :] = (-kk) * jnp.exp(cs - ld)
        bt_ref[rr_, :] = kk * a * g_inv
        kt_ref[rr_, :] = k2 * g_inv
        yield
        g_incl = jnp.exp(cs)
        v = pr_ref[hr, 2 * W:3 * W]
        rt_ref[rr_, :] = r * g_incl
        v_ref[rr_, :] = v
        gl_ref[rr_, :] = g_incl
        bonus_ref[hr, :] = bsum * v
        yield

    def chain_factors(batch_chains):
        nc = range(len(batch_chains))
        rows = [ring_rows(c) for c, _ in batch_chains]
        yrows = [slice(c * CHUNK, (c + 1) * CHUNK) for c, _ in batch_chains]
        cols = [slice(g * GROUP_W, (g + 1) * GROUP_W) for _, g in batch_chains]
        x = []
        for j in nc:
            lhs = jnp.concatenate([at_ref[rows[j], cols[j]], rt_ref[rows[j], cols[j]]], axis=0).astype(BF16)
            rhs = jnp.concatenate([mstack(bt_ref[rows[j], cols[j]]), mstack(kt_ref[rows[j], cols[j]])], axis=0)
            x.append(_dot(lhs, rhs, _NT))
        yield
        a_ab = [jnp.where(strict_w, x[j][:CHUNK, :GROUP_W], 0.0) for j in nc]
        a_ak = [jnp.where(strict_w, x[j][:CHUNK, GROUP_W:], 0.0).astype(BF16) for j in nc]
        m_rb = [jnp.where(incl_w, x[j][CHUNK:, :GROUP_W], 0.0).astype(BF16) for j in nc]
        m_rk = [jnp.where(incl_w, x[j][CHUNK:, GROUP_W:], 0.0).astype(BF16) for j in nc]
        tw = [a_ab[j] + eye_w for j in nc]
        pw = [a_ab[j].astype(BF16) for j in nc]
        pw = [_dot(pw[j], block_diag(pw[j])).astype(BF16) for j in nc]
        yield
        for it in range(CHUNK.bit_length() - 3):
            both = [_dot(jnp.concatenate([pw[j], tw[j].astype(BF16)], axis=0), block_diag(pw[j])) for j in nc]
            tw = [tw[j] + both[j][CHUNK:] for j in nc]
            pw = [both[j][:CHUNK].astype(BF16) for j in nc]
            yield
        tw = [tw[j] + _dot(tw[j].astype(BF16), block_diag(pw[j])) for j in nc]
        twb = [tw[j].astype(BF16) for j in nc]
        yield
        v_m = [mstack(v_ref[rows[j], cols[j]]) for j in nc]
        wt = [_dot(twb[j], mstack(at_ref[rows[j], cols[j]])) for j in nc]
        x1 = [_dot(a_ak[j], v_m[j]) for j in nc]
        yield
        u0 = [_dot(twb[j], mstack(x1[j])) for j in nc]
        for j in nc:
            q_ref[rows[j], cols[j]] = (rt_ref[rows[j], cols[j]] + _dot(m_rb[j], mstack(wt[j]))).astype(BF16)
        yield
        for j in nc:
            y_ref[yrows[j], cols[j]] = _dot(m_rb[j], mstack(u0[j])) + _dot(m_rk[j], v_m[j])
        yield
        for j in nc:
            c, g = batch_chains[j]
            gl = gl_ref[rows[j].stop - 1:rows[j].stop, cols[j]]
            bh = (bt_ref[rows[j], cols[j]] * gl).astype(BF16)
            kh = (kt_ref[rows[j], cols[j]] * gl).astype(BF16)
            gm = jnp.where(eye_g, gl, 0.0) + jnp.where(same_head, _dot(bh, wt[j].astype(BF16), _TN), 0.0)
            g_ref[ring_slot(c, g)] = gm.astype(BF16)
            h_ref[ring_slot(c, g)] = jnp.where(
                same_head,
                _dot(jnp.concatenate([bh, kh], axis=0),
                     jnp.concatenate([u0[j], v_ref[rows[j], cols[j]]], axis=0).astype(BF16), _TN),
                0.0)
        yield

    def recurrence(c):
        rows = slice(c * CHUNK, (c + 1) * CHUNK)
        for g in range(n_groups):
            cols = slice(g * GROUP_W, (g + 1) * GROUP_W)
            s_b = s_ref[g].astype(BF16)
            y_ref[rows, cols] = y_ref[rows, cols] + _dot(q_ref[ring_rows(c), cols], s_b)
            s_ref[g] = _dot(g_ref[ring_slot(c, g)], s_b) + h_ref[ring_slot(c, g)]
        yield

    def finish(half):
        hr = slice(half * HALF, (half + 1) * HALF)
        y = y_ref[hr, :]
        mean = _head_sum(y, ones_b) * (1.0 / HEAD_DIM)
        yield
        yc = y - mean
        var = _head_sum(yc * yc, ones_b) * (1.0 / HEAD_DIM)
        yield
        yn = yc * lax.rsqrt(var + GN_EPS) * lw_ref[...] + lb_ref[...] + bonus_ref[hr, :]
        yr_ref[hr, :] = (yn * gs_ref[hr, :].astype(F32)).astype(BF16)
        yield

    def project(sub):
        sr = slice(sub * SUB_CHUNKS * CHUNK, (sub + 1) * SUB_CHUNKS * CHUNK)
        o = _dot(yr_ref[sr, :], wor_ref[...]) + _dot(ya_ref[sr, :], woa_ref[...])
        yield
        ms = jnp.mean(o * o, axis=-1, keepdims=True)
        o_ref[sr, :] = x_ref[sr, :] + o * lax.rsqrt(ms + RMS_EPS) * gp_ref[...]
        yield

    n_sub = n_chunks // SUB_CHUNKS
    halves = lambda sub: (2 * sub, 2 * sub + 1)
    chunks = lambda sub: range(SUB_CHUNKS * sub, SUB_CHUNKS * (sub + 1))
    tail = lambda sub: ([recurrence(c) for c in chunks(sub)] + [finish(hf) for hf in halves(sub)]
                        + [project(sub)])
    run(seq(*[prep(hf) for hf in halves(0)]))
    for sub in range(n_sub):
        side = [prep(hf) for hf in halves(sub + 1)] if sub + 1 < n_sub else []
        side += tail(sub - 1) if sub > 0 else []
        run(chain_factors([(c, g) for c in chunks(sub) for g in range(n_groups)]), seq(*side), side_start=2)
    run(seq(*tail(n_sub - 1)))


def _rwkv_out(pr, gs, w0, w2a, a0, k_k, k_a, r_k, lnx_w, lnx_b, x2, ya, w_or, w_oa, g_post, batch, seq):
    m, d = x2.shape
    per_b = seq // RWKV_TILE
    W = RWKV_WIDTH
    n_cg = (RWKV_RING // CHUNK) * (W // GROUP_W)
    w2a_hi = w2a.astype(BF16)
    w2a_lo = (w2a - w2a_hi.astype(F32)).astype(BF16)
    w2a_hi = jnp.concatenate([w2a_hi, w2a_hi], axis=0)
    tile = lambda b, t: (b * per_b + t, 0)
    const = lambda b, t: (0, 0)
    vec = pl.BlockSpec((1, W), const)
    ring_scratch = pltpu.VMEM((RWKV_RING, W), F32)
    tile_scratch = pltpu.VMEM((RWKV_TILE, W), F32)
    return pl.pallas_call(
        _rwkv_body,
        grid=(batch, per_b),
        in_specs=[
            pl.BlockSpec((RWKV_TILE, N_RWKV_COLS), tile),
            pl.BlockSpec((RWKV_TILE, W), tile),
            vec,
            pl.BlockSpec(w2a_hi.shape, const),
            pl.BlockSpec(w2a_lo.shape, const),
            vec, vec, vec, vec, vec, vec,
            pl.BlockSpec((RWKV_TILE, d), tile),
            pl.BlockSpec((RWKV_TILE, ATTN_WIDTH), tile),
            pl.BlockSpec(w_or.shape, const),
            pl.BlockSpec(w_oa.shape, const),
            pl.BlockSpec((1, d), const),
        ],
        out_specs=pl.BlockSpec((RWKV_TILE, d), tile),
        out_shape=jax.ShapeDtypeStruct((m, d), F32),
        scratch_shapes=[
            pltpu.VMEM((W // GROUP_W, GROUP_W, GROUP_W), F32),
            ring_scratch, ring_scratch, ring_scratch, ring_scratch, ring_scratch, ring_scratch,
            tile_scratch, tile_scratch,
            pltpu.VMEM((RWKV_RING, W), BF16),
            pltpu.VMEM((n_cg, GROUP_W, GROUP_W), BF16),
            pltpu.VMEM((n_cg, GROUP_W, GROUP_W), F32),
            pltpu.VMEM((RWKV_TILE, W), BF16),
        ],
        compiler_params=pltpu.CompilerParams(
            dimension_semantics=("arbitrary", "arbitrary"), vmem_limit_bytes=VMEM_LIMIT),
        name="rwkv7_out",
    )(pr, gs, w0, w2a_hi, w2a_lo, a0, k_k, k_a, r_k, lnx_w, lnx_b, x2, ya, w_or, w_oa, g_post)


def kernel(x, g_pre, w_in, tshift_mu, w0, w2, a0, a2, k_k, k_a, r_k, lnx_w, lnx_b, w_out, g_post):
    batch, seq, d = x.shape
    depth = w_in.shape[0]
    assert seq % RWKV_TILE == 0 and seq % IN_TILE == 0 and IN_TILE % MOBA_BLOCK == 0 and d % LANES == 0
    x2 = x.reshape(batch * seq, d)
    slopes = 2.0 ** (-8.0 * jnp.arange(1, ATTN_HEADS + 1, dtype=F32) / ATTN_HEADS)
    row = lambda z: z.reshape(1, -1).astype(F32)
    for l in range(depth):
        wb = w_in[l].astype(BF16)
        w_r = wb[:, :N_RWKV_COLS]
        w_a = wb[:, N_RWKV_COLS:N_RWKV_COLS + N_ATTN_COLS]
        w_g = wb[:, N_RWKV_COLS + N_ATTN_COLS:]
        pr, q, k, vt, kmean, gs = _in_proj(x2, row(g_pre[l]), row(tshift_mu[l]), w_r, w_a, w_g, batch, seq)
        ya = _moba(slopes, q, k, _moba_key_aux(seq), vt,
                   kmean.reshape(batch, seq // MOBA_BLOCK, ATTN_WIDTH), gs, batch, seq)
        zeros = jnp.zeros((DECAY_RANK, RWKV_WIDTH), F32)
        w2a = jnp.concatenate([jnp.concatenate([w2[l], zeros], axis=1),
                               jnp.concatenate([zeros, a2[l]], axis=1)], axis=0)
        wo = w_out[l].astype(BF16)
        x2 = _rwkv_out(pr, gs, row(w0[l]), w2a, row(a0[l]), row(k_k[l]), row(k_a[l]), row(r_k[l]),
                       row(lnx_w[l]), row(lnx_b[l]), x2, ya, wo[:RWKV_WIDTH], wo[RWKV_WIDTH:],
                       row(g_post[l]), batch, seq)
    return x2.reshape(batch, seq, d)
```

```python
import functools

import jax
import jax.numpy as jnp
from jax import lax
from jax.experimental import pallas as pl
from jax.experimental.pallas import tpu as pltpu

F32 = jnp.float32
BF16 = jnp.bfloat16

HEAD_DIM = 64
RWKV_HEADS = 8
ATTN_HEADS = 8
RWKV_WIDTH = RWKV_HEADS * HEAD_DIM
ATTN_WIDTH = ATTN_HEADS * HEAD_DIM
D_MIX = RWKV_WIDTH + ATTN_WIDTH
DECAY_RANK = 64
ICLR_RANK = 64
MOBA_BLOCK = 256
MOBA_TOPK = 3
RMS_EPS = 1e-6
GN_EPS = 64e-5
EXP_M_HALF = 0.6065306597126334
NEG = -1e30
N_RWKV_COLS = 3 * RWKV_WIDTH + DECAY_RANK + ICLR_RANK
N_ATTN_COLS = 3 * ATTN_WIDTH

LANES = 128
MXU_DIM = 256
CHUNK = 64
GROUP_HEADS = MXU_DIM // HEAD_DIM
GROUP_W = GROUP_HEADS * HEAD_DIM
IN_TILE = 512
RWKV_TILE = 512
RWKV_RING = 512
VMEM_LIMIT = 48 * 1024 * 1024

_NT = (((1,), (1,)), ((), ()))
_TN = (((0,), (0,)), ((), ()))


def _dot(a, b, dims=None, precision=None):
    if dims is None:
        return jnp.dot(a, b, preferred_element_type=F32, precision=precision)
    return lax.dot_general(a, b, dims, preferred_element_type=F32, precision=precision)


def _inproj_body(per_b, x_ref, g_ref, mu_ref, wr_ref, wa_ref, wg_ref,
                 pr_ref, q_ref, k_ref, vt_ref, km_ref, gs_ref, prev_ref):
    x = x_ref[...]
    ms = jnp.mean(x * x, axis=-1, keepdims=True)
    h = (x * lax.rsqrt(ms + RMS_EPS) * g_ref[...]).astype(BF16)
    @pl.when(pl.program_id(0) % per_b == 0)
    def _():
        prev_ref[...] = jnp.zeros_like(prev_ref)

    row0 = lax.broadcasted_iota(jnp.int32, (IN_TILE, 1), 0) == 0
    for c0 in range(0, N_RWKV_COLS, MXU_DIM):
        cs = slice(c0, min(c0 + MXU_DIM, N_RWKV_COLS))
        p = _dot(h, wr_ref[:, cs])
        p_prev = jnp.where(row0, prev_ref[:, cs], pltpu.roll(p, 1, axis=0))
        prev_ref[:, cs] = p[IN_TILE - 1:IN_TILE, :]
        pr_ref[:, cs] = p + (p_prev - p) * mu_ref[:, cs]
    qkv = _dot(h, wa_ref[...])
    q_ref[...] = qkv[:, :ATTN_WIDTH] * (HEAD_DIM ** -0.5)
    kf = qkv[:, ATTN_WIDTH:2 * ATTN_WIDTH]
    k_ref[...] = kf.astype(BF16)
    for j in range(IN_TILE // MOBA_BLOCK):
        km_ref[j] = jnp.mean(kf[j * MOBA_BLOCK:(j + 1) * MOBA_BLOCK], axis=0, keepdims=True)
    vt_ref[0] = qkv[:, 2 * ATTN_WIDTH:].T.astype(BF16)
    g = _dot(h, wg_ref[...])
    gs_ref[...] = (g * jax.nn.sigmoid(g)).astype(BF16)


def _in_proj(x2, g_pre, mu, w_r, w_a, w_g, batch, seq):
    m, d = x2.shape
    nblk = m // IN_TILE
    per_b = seq // IN_TILE
    blocks = IN_TILE // MOBA_BLOCK
    const = lambda i: (0, 0)
    return pl.pallas_call(
        functools.partial(_inproj_body, per_b),
        grid=(nblk,),
        in_specs=[
            pl.BlockSpec((IN_TILE, d), lambda i: (i, 0)),
            pl.BlockSpec((1, d), const),
            pl.BlockSpec((1, N_RWKV_COLS), const),
            pl.BlockSpec(w_r.shape, const),
            pl.BlockSpec(w_a.shape, const),
            pl.BlockSpec(w_g.shape, const),
        ],
        out_specs=[
            pl.BlockSpec((IN_TILE, N_RWKV_COLS), lambda i: (i, 0)),
            pl.BlockSpec((IN_TILE, ATTN_WIDTH), lambda i: (i, 0)),
            pl.BlockSpec((IN_TILE, ATTN_WIDTH), lambda i: (i, 0)),
            pl.BlockSpec((1, ATTN_WIDTH, IN_TILE), lambda i: (i // per_b, 0, i % per_b)),
            pl.BlockSpec((blocks, 1, ATTN_WIDTH), lambda i: (i, 0, 0)),
            pl.BlockSpec((IN_TILE, D_MIX), lambda i: (i, 0)),
        ],
        out_shape=[
            jax.ShapeDtypeStruct((m, N_RWKV_COLS), F32),
            jax.ShapeDtypeStruct((m, ATTN_WIDTH), F32),
            jax.ShapeDtypeStruct((m, ATTN_WIDTH), BF16),
            jax.ShapeDtypeStruct((batch, ATTN_WIDTH, seq), BF16),
            jax.ShapeDtypeStruct((m // MOBA_BLOCK, 1, ATTN_WIDTH), F32),
            jax.ShapeDtypeStruct((m, D_MIX), BF16),
        ],
        scratch_shapes=[pltpu.VMEM((1, N_RWKV_COLS), F32)],
        compiler_params=pltpu.CompilerParams(
            dimension_semantics=("arbitrary",), vmem_limit_bytes=VMEM_LIMIT),
        name="in_proj",
    )(x2, g_pre, mu, w_r, w_a, w_g)


LOG2E = 1.4426950408889634
AUX_S, AUX_T, AUX_BLK, AUX_SEL = 0, 3, 6, 16
PV_ONES_ROWS = 16
MOBA_STEP_HEADS = 4
MOBA_STEP_BLOCKS = 2
MOBA_TRIP_WIDTHS = (8, 4, 2, 1)


def _moba_key_aux(seq):
    pos = jnp.arange(seq, dtype=jnp.int32)[:, None]
    lane = jnp.arange(LANES, dtype=jnp.int32)[None, :]
    blk = pos // MOBA_BLOCK
    tab = jnp.where(lane < AUX_T, pos % MOBA_BLOCK,
                    jnp.where(lane < AUX_BLK, 1,
                              jnp.where(lane < AUX_BLK + 3, blk * MOBA_BLOCK,
                                        jnp.where(lane == AUX_SEL + blk, 1, 0))))
    return tab.astype(BF16)


def _split2(x):
    hi = x.astype(BF16)
    lo = (x - hi.astype(F32)).astype(BF16)
    return hi, lo


def _split3(x):
    x1 = x.astype(BF16).astype(F32)
    r1 = x - x1
    x2 = r1.astype(BF16).astype(F32)
    x3 = (r1 - x2).astype(BF16).astype(F32)
    return x1, x2, x3


def _moba_body(slopes_ref, q_ref, k_ref, ak_ref, vt_ref, km_ref, gs_ref, o_ref, st_ref, p_ref):
    BS = MOBA_BLOCK
    grp = pl.program_id(1)
    step = pl.program_id(2)
    heads = MOBA_STEP_HEADS
    per_pair = LANES // HEAD_DIM
    nb = km_ref.shape[1]
    n_iota = lax.broadcasted_iota(jnp.int32, (nb, BS), 0)
    causal = (lax.broadcasted_iota(jnp.int32, (BS, BS), 0)
              <= lax.broadcasted_iota(jnp.int32, (BS, BS), 1))
    piece = n_iota % 3
    ones_rows = jnp.ones((PV_ONES_ROWS, BS), BF16)
    lane_head = lax.broadcasted_iota(jnp.int32, (1, heads * HEAD_DIM), 1) // HEAD_DIM
    row_h = lax.broadcasted_iota(jnp.int32, (LANES, 1), 0) // HEAD_DIM
    zero_b = jnp.zeros((), BF16)
    aux_pad = jnp.zeros((LANES - AUX_SEL - nb, BS), BF16)
    no_mask = jnp.zeros((nb, BS), BF16)
    km = km_ref[0]
    km_stack = jnp.concatenate([jnp.where(lane_head == h, km, 0.0) for h in range(heads)], axis=0)
    km_hi, km_lo = _split2(km_stack)
    km_cat = jnp.concatenate([km_hi, km_lo, km_hi], axis=1)

    def pair_lanes(h):
        p = h // per_pair
        return slice(p * LANES, (p + 1) * LANES)

    def query_block(u):
        i = MOBA_STEP_BLOCKS * step + u
        q_rows = slice(u * BS, (u + 1) * BS)

        def key_rows(n):
            b = jnp.where(n == 0, i, jnp.minimum(n - 1, i))
            return pl.ds(pl.multiple_of(b * BS, BS), BS)

        def scores(h, blk, q_aug_t):
            k_aug = jnp.concatenate([k_ref[blk, pair_lanes(h)], ak_ref[blk, :]], axis=1)
            return _dot(k_aug, q_aug_t)

        def pv(h, slot, blk):
            vt_aug = jnp.concatenate([vt_ref[0, h * HEAD_DIM:(h + 1) * HEAD_DIM, blk], ones_rows], axis=0)
            return _dot(vt_aug, p_ref[u, slot, h])

        def prologue():
            q_t = (q_ref[q_rows, :] * LOG2E).T
            q_hi, q_lo = _split2(q_t)
            gates = _dot(km_cat, jnp.concatenate([q_hi, q_hi, q_lo], axis=0))
            gate = [jnp.where(n_iota < i, gates[h * nb:(h + 1) * nb], NEG) for h in range(heads)]
            t_glob = (lax.broadcasted_iota(jnp.int32, (nb, BS), 1) + i * BS).astype(F32)
            qh_t, aux_top = [], []
            for h in range(heads):
                cvec = jnp.full((nb, BS), slopes_ref[grp * heads + h] * LOG2E, F32)
                c1, c2, c3 = _split3(cvec)
                e1, e2, e3 = _split3(-cvec * t_glob)
                cp = jnp.where(piece == 0, c1, jnp.where(piece == 1, c2, c3))
                ep = jnp.where(piece == 0, e1, jnp.where(piece == 1, e2, e3))
                top = jnp.where((n_iota >= AUX_T) & (n_iota < AUX_BLK), ep, cp)
                aux_top.append(jnp.where(n_iota < AUX_BLK + 3, top, 0.0).astype(BF16))
                qh_t.append(jnp.where(row_h == h % per_pair, q_hi[pair_lanes(h), :], zero_b))

            def q_aug(h, sel_rows):
                return jnp.concatenate([qh_t[h], aux_top[h], sel_rows, aux_pad], axis=0)

            for h in range(heads):
                st_ref[u, 0, h] = jnp.where(causal, scores(h, key_rows(0), q_aug(h, no_mask)), NEG)
                p_ref[u, 0, h] = jnp.zeros((BS, BS), BF16)
            sel = [jnp.zeros((nb, BS), F32) for _ in range(heads)]
            for rnk in range(MOBA_TOPK):
                rank_ok = jnp.where(rnk < i, 1.0, 0.0)
                for h in range(heads):
                    mx = jnp.max(gate[h], axis=0, keepdims=True)
                    idx = jnp.min(jnp.where(gate[h] == mx, n_iota, nb), axis=0, keepdims=True)
                    pick = n_iota == idx
                    sel[h] = jnp.where(pick, rank_ok, sel[h])
                    gate[h] = jnp.where(pick, -jnp.inf, gate[h])
            return [q_aug(h, jnp.where(sel[h] > 0.0, 0.0, NEG).astype(BF16)) for h in range(heads)]

        def loops(q_augs):
            def stage(n, carry, src, dst):
                ms, accs, alphas = carry
                prev_rows = key_rows(jnp.maximum(n - 1, 0))
                next_rows = key_rows(n + 1)
                pvs = [pv(h, src, prev_rows) for h in range(heads)]
                ms2, alphas2 = [], []
                for h in range(heads):
                    st = st_ref[u, src, h]
                    m_new = jnp.maximum(ms[h], jnp.max(st, axis=0, keepdims=True))
                    alphas2.append(jnp.exp2(ms[h] - m_new))
                    p_ref[u, dst, h] = jnp.exp2(st - m_new).astype(BF16)
                    ms2.append(m_new)
                for h in range(heads):
                    st_ref[u, dst, h] = scores(h, next_rows, q_augs[h])
                accs2 = [alphas[h] * accs[h] + pvs[h] for h in range(heads)]
                return tuple(ms2), tuple(accs2), tuple(alphas2)

            def run_blocks(first, count, carry):
                for j in range(count):
                    last_odd = j == count - 1 and count % 2 == 1
                    carry = stage(first + j, carry, j % 2, 0 if last_odd else 1 - j % 2)
                return carry

            m0 = tuple(jnp.full((1, BS), NEG, F32) for _ in range(heads))
            acc0 = tuple(jnp.zeros((HEAD_DIM + PV_ONES_ROWS, BS), F32) for _ in range(heads))
            alpha0 = tuple(jnp.ones((1, BS), F32) for _ in range(heads))
            n_blocks = i + 1
            done = 0
            carry = (m0, acc0, alpha0)
            for width in MOBA_TRIP_WIDTHS:
                left = (n_blocks - done) // width
                trips = left if width == MOBA_TRIP_WIDTHS[0] else left % 2
                base = done
                carry = lax.fori_loop(
                    0, trips, lambda t, c, w=width, b=base: run_blocks(b + w * t, w, c), carry)
                done = done + trips * width
            return carry

        def epilogue(carry):
            ms, accs, alphas = carry
            last_rows = key_rows(i)
            outs = []
            for h in range(heads):
                acc = alphas[h] * accs[h] + pv(h, 0, last_rows)
                outs.append(acc[:HEAD_DIM] / acc[HEAD_DIM:HEAD_DIM + 1])
            ot = jnp.concatenate(outs, axis=0)
            o_ref[q_rows, :] = (ot.T * gs_ref[q_rows, :].astype(F32)).astype(BF16)

        return prologue, loops, epilogue

    parts = [query_block(u) for u in range(MOBA_STEP_BLOCKS)]
    carry = None
    for u, (prologue, loops, epilogue) in enumerate(parts):
        q_augs = prologue()
        if u > 0:
            parts[u - 1][2](carry)
        carry = loops(q_augs)
    parts[-1][2](carry)


def _moba(slopes, q, k, key_aux, vt, kmean, gs, batch, seq):
    m = q.shape[0]
    BS = MOBA_BLOCK
    nb = seq // BS
    assert nb == AUX_SEL and AUX_SEL + nb <= LANES and AUX_BLK + 3 <= AUX_SEL and nb % MOBA_STEP_BLOCKS == 0
    H, U = MOBA_STEP_HEADS, MOBA_STEP_BLOCKS
    W = H * HEAD_DIM
    n_groups = ATTN_WIDTH // W
    n_steps = nb // U
    gate_col0 = RWKV_WIDTH // W
    return pl.pallas_call(
        _moba_body,
        grid_spec=pltpu.PrefetchScalarGridSpec(
            num_scalar_prefetch=1,
            grid=(batch, n_groups, n_steps),
            in_specs=[
                pl.BlockSpec((U * BS, W), lambda b, g, i, s: (b * n_steps + i, g)),
                pl.BlockSpec((seq, W), lambda b, g, i, s: (b, g)),
                pl.BlockSpec((seq, LANES), lambda b, g, i, s: (0, 0)),
                pl.BlockSpec((1, W, seq), lambda b, g, i, s: (b, g, 0)),
                pl.BlockSpec((1, nb, W), lambda b, g, i, s: (b, 0, g)),
                pl.BlockSpec((U * BS, W), lambda b, g, i, s: (b * n_steps + i, gate_col0 + g)),
            ],
            out_specs=pl.BlockSpec((U * BS, W), lambda b, g, i, s: (b * n_steps + i, g)),
            scratch_shapes=[
                pltpu.VMEM((U, 2, H, BS, BS), F32),
                pltpu.VMEM((U, 2, H, BS, BS), BF16),
            ],
        ),
        out_shape=jax.ShapeDtypeStruct((m, ATTN_WIDTH), BF16),
        compiler_params=pltpu.CompilerParams(
            dimension_semantics=("arbitrary", "arbitrary", "arbitrary"), vmem_limit_bytes=VMEM_LIMIT),
        name="moba",
    )(slopes, q, k, key_aux, vt, kmean, gs)


def _split_dot(x, ones_b):
    hi, lo = _split2(x)
    return _dot(hi, ones_b) + _dot(lo, ones_b)


def _head_sum(x, ones_b, split=True):
    one = _split_dot if split else (lambda xs, o: _dot(xs.astype(BF16), o))
    parts = [one(x[:, g * GROUP_W:(g + 1) * GROUP_W], ones_b) for g in range(x.shape[1] // GROUP_W)]
    return jnp.concatenate(parts, axis=1)


def _rwkv_body(pr_ref, gs_ref, w0_ref, w2a_hi_ref, w2a_lo_ref, a0_ref, kk_ref, ka_ref, rk_ref, lw_ref,
               lb_ref, x_ref, ya_ref, wor_ref, woa_ref, gp_ref,
               o_ref, s_ref, at_ref, bt_ref, kt_ref, rt_ref, v_ref, gl_ref, y_ref, bonus_ref,
               q_ref, g_ref, h_ref, yr_ref):
    W = RWKV_WIDTH
    n_groups = W // GROUP_W
    n_chunks = RWKV_TILE // CHUNK

    @pl.when(pl.program_id(1) == 0)
    def _():
        s_ref[...] = jnp.zeros_like(s_ref)

    rr = lax.broadcasted_iota(jnp.int32, (GROUP_W, GROUP_W), 0)
    cc = lax.broadcasted_iota(jnp.int32, (GROUP_W, GROUP_W), 1)
    same_head = (rr // HEAD_DIM) == (cc // HEAD_DIM)
    ones_b = same_head.astype(BF16)
    pr_ = lax.broadcasted_iota(jnp.int32, (2 * CHUNK, 4 * CHUNK), 0)
    pc_ = lax.broadcasted_iota(jnp.int32, (2 * CHUNK, 4 * CHUNK), 1)
    tril_pair = ((pr_ // CHUNK == pc_ // (2 * CHUNK)) & (pc_ % CHUNK <= pr_ % CHUNK)).astype(BF16)
    wr = lax.broadcasted_iota(jnp.int32, (CHUNK, GROUP_W), 0)
    wc = lax.broadcasted_iota(jnp.int32, (CHUNK, GROUP_W), 1) % CHUNK
    strict_w = wc < wr
    incl_w = wc <= wr
    eye_w = (wc == wr).astype(F32)
    lane_g = lax.broadcasted_iota(jnp.int32, (1, GROUP_W), 1) // HEAD_DIM
    head_masks = [lane_g == h for h in range(GROUP_HEADS)]

    def mstack(x):
        xb = x.astype(BF16)
        zero = jnp.zeros_like(xb)
        return jnp.concatenate([jnp.where(hm, xb, zero) for hm in head_masks], axis=0)

    def head_rows(xb):
        return jnp.concatenate([xb[:, h * HEAD_DIM:(h + 1) * HEAD_DIM] for h in range(GROUP_HEADS)], axis=0)

    def block_diag(xw):
        return jnp.where(same_head, jnp.concatenate([xw] * GROUP_HEADS, axis=0), jnp.zeros((), BF16))

    HALF = 2 * CHUNK
    SUB_CHUNKS = 4
    RING_CHUNKS = RWKV_RING // CHUNK

    def ring_rows(c, n=1):
        r0 = (c % RING_CHUNKS) * CHUNK
        return slice(r0, r0 + n * CHUNK)

    def ring_slot(c, g):
        return (c % RING_CHUNKS) * n_groups + g

    lane_z = lax.broadcasted_iota(jnp.int32, (1, DECAY_RANK + ICLR_RANK), 1)

    def run(main, side=None, side_start=0):
        gens = [main] + ([side] if side is not None else [])
        stage_no = 0
        while gens:
            for gen in list(gens):
                if gen is side and stage_no < side_start and main in gens:
                    continue
                try:
                    next(gen)
                except StopIteration:
                    gens.remove(gen)
            stage_no += 1

    def seq(*gens):
        for gen in gens:
            yield from gen

    def prep(half):
        hr = slice(half * HALF, (half + 1) * HALF)
        rr_ = ring_rows(2 * half, 2)
        z = pr_ref[hr, 3 * W:3 * W + DECAY_RANK + ICLR_RANK]
        z = jnp.where(lane_z < DECAY_RANK, jnp.tanh(z), z)
        z_hi, z_lo = _split2(z)
        lin = _dot(jnp.concatenate([z_hi, z_lo], axis=1), w2a_hi_ref[...]) + _dot(z_hi, w2a_lo_ref[...])
        k = pr_ref[hr, W:2 * W]
        kkr = k * kk_ref[...]
        ss = _head_sum(kkr * kkr, ones_b, split=False)
        yield
        ld = -EXP_M_HALF * jax.nn.sigmoid(w0_ref[...] + lin[:, :W])
        ld_hi, ld_lo = _split2(ld)
        cs = _dot(tril_pair, jnp.concatenate([ld_hi[:CHUNK], ld_lo[:CHUNK], ld_hi[CHUNK:], ld_lo[CHUNK:]], axis=0))
        a = jax.nn.sigmoid(a0_ref[...] + lin[:, W:])
        k2 = k * (1.0 + (a - 1.0) * ka_ref[...])
        r = pr_ref[hr, 0:W]
        bsum = _head_sum(r * k2 * rk_ref[...], ones_b)
        yield
        kk = kkr * jnp.minimum(lax.rsqrt(ss), 1e12)
        g_inv = jnp.exp(-cs)
        at_ref[rr_, :] = (-kk) * jnp.exp(cs - ld)
        bt_ref[rr_, :] = kk * a * g_inv
        kt_ref[rr_, :] = k2 * g_inv
        yield
        g_incl = jnp.exp(cs)
        v = pr_ref[hr, 2 * W:3 * W]
        rt_ref[rr_, :] = r * g_incl
        v_ref[rr_, :] = v
        gl_ref[rr_, :] = g_incl
        bonus_ref[hr, :] = bsum * v
        yield

    def chain_factors(batch_chains):
        nc = range(len(batch_chains))
        rows = [ring_rows(c) for c, _ in batch_chains]
        yrows = [slice(c * CHUNK, (c + 1) * CHUNK) for c, _ in batch_chains]
        cols = [slice(g * GROUP_W, (g + 1) * GROUP_W) for _, g in batch_chains]
        x = []
        for j in nc:
            lhs = jnp.concatenate([at_ref[rows[j], cols[j]], rt_ref[rows[j], cols[j]]], axis=0).astype(BF16)
            rhs = jnp.concatenate([mstack(bt_ref[rows[j], cols[j]]), mstack(kt_ref[rows[j], cols[j]])], axis=0)
            x.append(_dot(lhs, rhs, _NT))
        yield
        a_ab = [jnp.where(strict_w, x[j][:CHUNK, :GROUP_W], 0.0) for j in nc]
        a_ak = [jnp.where(strict_w, x[j][:CHUNK, GROUP_W:], 0.0).astype(BF16) for j in nc]
        m_rb = [jnp.where(incl_w, x[j][CHUNK:, :GROUP_W], 0.0).astype(BF16) for j in nc]
        m_rk = [jnp.where(incl_w, x[j][CHUNK:, GROUP_W:], 0.0).astype(BF16) for j in nc]
        tw = [a_ab[j] + eye_w for j in nc]
        pw = [a_ab[j].astype(BF16) for j in nc]
        pw = [_dot(pw[j], block_diag(pw[j])).astype(BF16) for j in nc]
        yield
        for it in range(CHUNK.bit_length() - 3):
            both = [_dot(jnp.concatenate([pw[j], tw[j].astype(BF16)], axis=0), block_diag(pw[j])) for j in nc]
            tw = [tw[j] + both[j][CHUNK:] for j in nc]
            pw = [both[j][:CHUNK].astype(BF16) for j in nc]
            yield
        tw = [tw[j] + _dot(tw[j].astype(BF16), block_diag(pw[j])) for j in nc]
        twb = [tw[j].astype(BF16) for j in nc]
        yield
        v_m = [mstack(v_ref[rows[j], cols[j]]) for j in nc]
        wt = [_dot(twb[j], mstack(at_ref[rows[j], cols[j]])) for j in nc]
        x1 = [_dot(a_ak[j], v_m[j]) for j in nc]
        yield
        u0 = [_dot(twb[j], mstack(x1[j])) for j in nc]
        for j in nc:
            q_ref[rows[j], cols[j]] = (rt_ref[rows[j], cols[j]] + _dot(m_rb[j], mstack(wt[j]))).astype(BF16)
        yield
        for j in nc:
            y_ref[yrows[j], cols[j]] = _dot(m_rb[j], mstack(u0[j])) + _dot(m_rk[j], v_m[j])
        yield
        for j in nc:
            c, g = batch_chains[j]
            gl = gl_ref[rows[j].stop - 1:rows[j].stop, cols[j]]
            bh = (bt_ref[rows[j], cols[j]] * gl).astype(BF16)
            kh = (kt_ref[rows[j], cols[j]] * gl).astype(BF16)
            bh_rows, kh_rows = head_rows(bh), head_rows(kh)
            gw = eye_w * gl + _dot(bh_rows, mstack(wt[j]), _TN)
            g_ref[ring_slot(c, g)] = gw.astype(BF16)
            h_ref[ring_slot(c, g)] = _dot(
                jnp.concatenate([bh_rows, kh_rows], axis=0),
                jnp.concatenate([mstack(u0[j]), v_m[j]], axis=0), _TN)
        yield

    def recurrence(c):
        rows = slice(c * CHUNK, (c + 1) * CHUNK)
        for g in range(n_groups):
            cols = slice(g * GROUP_W, (g + 1) * GROUP_W)
            s_b = block_diag(s_ref[g].astype(BF16))
            y_ref[rows, cols] = y_ref[rows, cols] + _dot(q_ref[ring_rows(c), cols], s_b)
            s_ref[g] = _dot(g_ref[ring_slot(c, g)], s_b) + h_ref[ring_slot(c, g)]
        yield

    def finish(half):
        hr = slice(half * HALF, (half + 1) * HALF)
        y = y_ref[hr, :]
        mean = _head_sum(y, ones_b) * (1.0 / HEAD_DIM)
        yield
        yc = y - mean
        var = _head_sum(yc * yc, ones_b) * (1.0 / HEAD_DIM)
        yield
        yn = yc * lax.rsqrt(var + GN_EPS) * lw_ref[...] + lb_ref[...] + bonus_ref[hr, :]
        yr_ref[hr, :] = (yn * gs_ref[hr, :].astype(F32)).astype(BF16)
        yield

    def project(sub):
        sr = slice(sub * SUB_CHUNKS * CHUNK, (sub + 1) * SUB_CHUNKS * CHUNK)
        o = _dot(yr_ref[sr, :], wor_ref[...]) + _dot(ya_ref[sr, :], woa_ref[...])
        yield
        ms = jnp.mean(o * o, axis=-1, keepdims=True)
        o_ref[sr, :] = x_ref[sr, :] + o * lax.rsqrt(ms + RMS_EPS) * gp_ref[...]
        yield

    n_sub = n_chunks // SUB_CHUNKS
    halves = lambda sub: (2 * sub, 2 * sub + 1)
    chunks = lambda sub: range(SUB_CHUNKS * sub, SUB_CHUNKS * (sub + 1))
    tail = lambda sub: ([recurrence(c) for c in chunks(sub)] + [finish(hf) for hf in halves(sub)]
                        + [project(sub)])
    run(seq(*[prep(hf) for hf in halves(0)]))
    for sub in range(n_sub):
        side = [prep(hf) for hf in halves(sub + 1)] if sub + 1 < n_sub else []
        side += tail(sub - 1) if sub > 0 else []
        run(chain_factors([(c, g) for c in chunks(sub) for g in range(n_groups)]), seq(*side), side_start=2)
    run(seq(*tail(n_sub - 1)))


def _rwkv_out(pr, gs, w0, w2a, a0, k_k, k_a, r_k, lnx_w, lnx_b, x2, ya, w_or, w_oa, g_post, batch, seq):
    m, d = x2.shape
    per_b = seq // RWKV_TILE
    W = RWKV_WIDTH
    n_cg = (RWKV_RING // CHUNK) * (W // GROUP_W)
    w2a_hi = w2a.astype(BF16)
    w2a_lo = (w2a - w2a_hi.astype(F32)).astype(BF16)
    w2a_hi = jnp.concatenate([w2a_hi, w2a_hi], axis=0)
    tile = lambda b, t: (b * per_b + t, 0)
    const = lambda b, t: (0, 0)
    vec = pl.BlockSpec((1, W), const)
    ring_scratch = pltpu.VMEM((RWKV_RING, W), F32)
    tile_scratch = pltpu.VMEM((RWKV_TILE, W), F32)
    return pl.pallas_call(
        _rwkv_body,
        grid=(batch, per_b),
        in_specs=[
            pl.BlockSpec((RWKV_TILE, N_RWKV_COLS), tile),
            pl.BlockSpec((RWKV_TILE, W), tile),
            vec,
            pl.BlockSpec(w2a_hi.shape, const),
            pl.BlockSpec(w2a_lo.shape, const),
            vec, vec, vec, vec, vec, vec,
            pl.BlockSpec((RWKV_TILE, d), tile),
            pl.BlockSpec((RWKV_TILE, ATTN_WIDTH), tile),
            pl.BlockSpec(w_or.shape, const),
            pl.BlockSpec(w_oa.shape, const),
            pl.BlockSpec((1, d), const),
        ],
        out_specs=pl.BlockSpec((RWKV_TILE, d), tile),
        out_shape=jax.ShapeDtypeStruct((m, d), F32),
        scratch_shapes=[
            pltpu.VMEM((W // GROUP_W, HEAD_DIM, GROUP_W), F32),
            ring_scratch, ring_scratch, ring_scratch, ring_scratch, ring_scratch, ring_scratch,
            tile_scratch, tile_scratch,
            pltpu.VMEM((RWKV_RING, W), BF16),
            pltpu.VMEM((n_cg, HEAD_DIM, GROUP_W), BF16),
            pltpu.VMEM((n_cg, HEAD_DIM, GROUP_W), F32),
            pltpu.VMEM((RWKV_TILE, W), BF16),
        ],
        compiler_params=pltpu.CompilerParams(
            dimension_semantics=("arbitrary", "arbitrary"), vmem_limit_bytes=VMEM_LIMIT),
        name="rwkv7_out",
    )(pr, gs, w0, w2a_hi, w2a_lo, a0, k_k, k_a, r_k, lnx_w, lnx_b, x2, ya, w_or, w_oa, g_post)


def kernel(x, g_pre, w_in, tshift_mu, w0, w2, a0, a2, k_k, k_a, r_k, lnx_w, lnx_b, w_out, g_post):
    batch, seq, d = x.shape
    depth = w_in.shape[0]
    assert seq % RWKV_TILE == 0 and seq % IN_TILE == 0 and IN_TILE % MOBA_BLOCK == 0 and d % LANES == 0
    x2 = x.reshape(batch * seq, d)
    slopes = 2.0 ** (-8.0 * jnp.arange(1, ATTN_HEADS + 1, dtype=F32) / ATTN_HEADS)
    row = lambda z: z.reshape(1, -1).astype(F32)
    for l in range(depth):
        wb = w_in[l].astype(BF16)
        w_r = wb[:, :N_RWKV_COLS]
        w_a = wb[:, N_RWKV_COLS:N_RWKV_COLS + N_ATTN_COLS]
        w_g = wb[:, N_RWKV_COLS + N_ATTN_COLS:]
        pr, q, k, vt, kmean, gs = _in_proj(x2, row(g_pre[l]), row(tshift_mu[l]), w_r, w_a, w_g, batch, seq)
        ya = _moba(slopes, q, k, _moba_key_aux(seq), vt,
                   kmean.reshape(batch, seq // MOBA_BLOCK, ATTN_WIDTH), gs, batch, seq)
        zeros = jnp.zeros((DECAY_RANK, RWKV_WIDTH), F32)
        w2a = jnp.concatenate([jnp.concatenate([w2[l], zeros], axis=1),
                               jnp.concatenate([zeros, a2[l]], axis=1)], axis=0)
        wo = w_out[l].astype(BF16)
        x2 = _rwkv_out(pr, gs, row(w0[l]), w2a, row(a0[l]), row(k_k[l]), row(k_a[l]), row(r_k[l]),
                       row(lnx_w[l]), row(lnx_b[l]), x2, ya, wo[:RWKV_WIDTH], wo[RWKV_WIDTH:],
                       row(g_post[l]), batch, seq)
    return x2.reshape(batch, seq, d)
```

```python
import functools

import jax
import jax.numpy as jnp
from jax import lax
from jax.experimental import pallas as pl
from jax.experimental.pallas import tpu as pltpu

F32 = jnp.float32
BF16 = jnp.bfloat16

HEAD_DIM = 64
RWKV_HEADS = 8
ATTN_HEADS = 8
RWKV_WIDTH = RWKV_HEADS * HEAD_DIM
ATTN_WIDTH = ATTN_HEADS * HEAD_DIM
D_MIX = RWKV_WIDTH + ATTN_WIDTH
DECAY_RANK = 64
ICLR_RANK = 64
MOBA_BLOCK = 256
MOBA_TOPK = 3
RMS_EPS = 1e-6
GN_EPS = 64e-5
EXP_M_HALF = 0.6065306597126334
NEG = -1e30
N_RWKV_COLS = 3 * RWKV_WIDTH + DECAY_RANK + ICLR_RANK
N_ATTN_COLS = 3 * ATTN_WIDTH

LANES = 128
MXU_DIM = 256
CHUNK = 64
GROUP_HEADS = MXU_DIM // HEAD_DIM
GROUP_W = GROUP_HEADS * HEAD_DIM
IN_TILE = 512
RWKV_TILE = 512
RWKV_RING = 512
VMEM_LIMIT = 48 * 1024 * 1024

_NT = (((1,), (1,)), ((), ()))
_TN = (((0,), (0,)), ((), ()))


def _dot(a, b, dims=None, precision=None):
    if dims is None:
        return jnp.dot(a, b, preferred_element_type=F32, precision=precision)
    return lax.dot_general(a, b, dims, preferred_element_type=F32, precision=precision)


def _inproj_body(per_b, x_ref, g_ref, mu_ref, wr_ref, wa_ref, wg_ref,
                 pr_ref, q_ref, k_ref, vt_ref, km_ref, gs_ref, prev_ref):
    x = x_ref[...]
    ms = jnp.mean(x * x, axis=-1, keepdims=True)
    h = (x * lax.rsqrt(ms + RMS_EPS) * g_ref[...]).astype(BF16)
    @pl.when(pl.program_id(0) % per_b == 0)
    def _():
        prev_ref[...] = jnp.zeros_like(prev_ref)

    row0 = lax.broadcasted_iota(jnp.int32, (IN_TILE, 1), 0) == 0
    for c0 in range(0, N_RWKV_COLS, MXU_DIM):
        cs = slice(c0, min(c0 + MXU_DIM, N_RWKV_COLS))
        p = _dot(h, wr_ref[:, cs])
        p_prev = jnp.where(row0, prev_ref[:, cs], pltpu.roll(p, 1, axis=0))
        prev_ref[:, cs] = p[IN_TILE - 1:IN_TILE, :]
        pr_ref[:, cs] = p + (p_prev - p) * mu_ref[:, cs]
    qkv = _dot(h, wa_ref[...])
    q_ref[...] = qkv[:, :ATTN_WIDTH] * (HEAD_DIM ** -0.5)
    kf = qkv[:, ATTN_WIDTH:2 * ATTN_WIDTH]
    k_ref[...] = kf.astype(BF16)
    for j in range(IN_TILE // MOBA_BLOCK):
        km_ref[j] = jnp.mean(kf[j * MOBA_BLOCK:(j + 1) * MOBA_BLOCK], axis=0, keepdims=True)
    vt_ref[0] = qkv[:, 2 * ATTN_WIDTH:].T.astype(BF16)
    g = _dot(h, wg_ref[...])
    gs_ref[...] = (g * jax.nn.sigmoid(g)).astype(BF16)


def _in_proj(x2, g_pre, mu, w_r, w_a, w_g, batch, seq):
    m, d = x2.shape
    nblk = m // IN_TILE
    per_b = seq // IN_TILE
    blocks = IN_TILE // MOBA_BLOCK
    const = lambda i: (0, 0)
    return pl.pallas_call(
        functools.partial(_inproj_body, per_b),
        grid=(nblk,),
        in_specs=[
            pl.BlockSpec((IN_TILE, d), lambda i: (i, 0)),
            pl.BlockSpec((1, d), const),
            pl.BlockSpec((1, N_RWKV_COLS), const),
            pl.BlockSpec(w_r.shape, const),
            pl.BlockSpec(w_a.shape, const),
            pl.BlockSpec(w_g.shape, const),
        ],
        out_specs=[
            pl.BlockSpec((IN_TILE, N_RWKV_COLS), lambda i: (i, 0)),
            pl.BlockSpec((IN_TILE, ATTN_WIDTH), lambda i: (i, 0)),
            pl.BlockSpec((IN_TILE, ATTN_WIDTH), lambda i: (i, 0)),
            pl.BlockSpec((1, ATTN_WIDTH, IN_TILE), lambda i: (i // per_b, 0, i % per_b)),
            pl.BlockSpec((blocks, 1, ATTN_WIDTH), lambda i: (i, 0, 0)),
            pl.BlockSpec((IN_TILE, D_MIX), lambda i: (i, 0)),
        ],
        out_shape=[
            jax.ShapeDtypeStruct((m, N_RWKV_COLS), F32),
            jax.ShapeDtypeStruct((m, ATTN_WIDTH), F32),
            jax.ShapeDtypeStruct((m, ATTN_WIDTH), BF16),
            jax.ShapeDtypeStruct((batch, ATTN_WIDTH, seq), BF16),
            jax.ShapeDtypeStruct((m // MOBA_BLOCK, 1, ATTN_WIDTH), F32),
            jax.ShapeDtypeStruct((m, D_MIX), BF16),
        ],
        scratch_shapes=[pltpu.VMEM((1, N_RWKV_COLS), F32)],
        compiler_params=pltpu.CompilerParams(
            dimension_semantics=("arbitrary",), vmem_limit_bytes=VMEM_LIMIT),
        name="in_proj",
    )(x2, g_pre, mu, w_r, w_a, w_g)


LOG2E = 1.4426950408889634
AUX_S, AUX_T, AUX_BLK, AUX_SEL = 0, 3, 6, 16
PV_ONES_ROWS = 16
MOBA_STEP_HEADS = 4
MOBA_STEP_BLOCKS = 2
MOBA_TRIP_WIDTHS = (4, 2, 1)


def _moba_key_aux(seq):
    pos = jnp.arange(seq, dtype=jnp.int32)[:, None]
    lane = jnp.arange(LANES, dtype=jnp.int32)[None, :]
    blk = pos // MOBA_BLOCK
    tab = jnp.where(lane < AUX_T, pos % MOBA_BLOCK,
                    jnp.where(lane < AUX_BLK, 1,
                              jnp.where(lane < AUX_BLK + 3, blk * MOBA_BLOCK,
                                        jnp.where(lane == AUX_SEL + blk, 1, 0))))
    return tab.astype(BF16)


def _split2(x):
    hi = x.astype(BF16)
    lo = (x - hi.astype(F32)).astype(BF16)
    return hi, lo


def _split3(x):
    x1 = x.astype(BF16).astype(F32)
    r1 = x - x1
    x2 = r1.astype(BF16).astype(F32)
    x3 = (r1 - x2).astype(BF16).astype(F32)
    return x1, x2, x3


def _moba_body(slopes_ref, q_ref, k_ref, ak_ref, vt_ref, km_ref, gs_ref, o_ref, st_ref, p_ref):
    BS = MOBA_BLOCK
    grp = pl.program_id(1)
    step = pl.program_id(2)
    heads = MOBA_STEP_HEADS
    per_pair = LANES // HEAD_DIM
    nb = km_ref.shape[1]
    n_iota = lax.broadcasted_iota(jnp.int32, (nb, BS), 0)
    causal = (lax.broadcasted_iota(jnp.int32, (BS, BS), 0)
              <= lax.broadcasted_iota(jnp.int32, (BS, BS), 1))
    piece = n_iota % 3
    ones_rows = jnp.ones((PV_ONES_ROWS, BS), BF16)
    lane_head = lax.broadcasted_iota(jnp.int32, (1, heads * HEAD_DIM), 1) // HEAD_DIM
    row_h = lax.broadcasted_iota(jnp.int32, (LANES, 1), 0) // HEAD_DIM
    zero_b = jnp.zeros((), BF16)
    aux_pad = jnp.zeros((LANES - AUX_SEL - nb, BS), BF16)
    no_mask = jnp.zeros((nb, BS), BF16)
    km = km_ref[0]
    km_stack = jnp.concatenate([jnp.where(lane_head == h, km, 0.0) for h in range(heads)], axis=0)
    km_hi, km_lo = _split2(km_stack)
    km_cat = jnp.concatenate([km_hi, km_lo, km_hi], axis=1)

    def pair_lanes(h):
        p = h // per_pair
        return slice(p * LANES, (p + 1) * LANES)

    def query_block(u):
        i = MOBA_STEP_BLOCKS * step + u
        q_rows = slice(u * BS, (u + 1) * BS)

        def key_rows(n):
            b = jnp.where(n == 0, i, jnp.minimum(n - 1, i))
            return pl.ds(pl.multiple_of(b * BS, BS), BS)

        def scores(h, blk, q_aug_t):
            k_aug = jnp.concatenate([k_ref[blk, pair_lanes(h)], ak_ref[blk, :]], axis=1)
            return _dot(k_aug, q_aug_t)

        def pv(h, slot, blk):
            vt_aug = jnp.concatenate([vt_ref[0, h * HEAD_DIM:(h + 1) * HEAD_DIM, blk], ones_rows], axis=0)
            return _dot(vt_aug, p_ref[u, slot, h])

        def prologue():
            q_t = (q_ref[q_rows, :] * LOG2E).T
            q_hi, q_lo = _split2(q_t)
            gates = _dot(km_cat, jnp.concatenate([q_hi, q_hi, q_lo], axis=0))
            gate = [jnp.where(n_iota < i, gates[h * nb:(h + 1) * nb], NEG) for h in range(heads)]
            t_glob = (lax.broadcasted_iota(jnp.int32, (nb, BS), 1) + i * BS).astype(F32)
            qh_t, aux_top = [], []
            for h in range(heads):
                cvec = jnp.full((nb, BS), slopes_ref[grp * heads + h] * LOG2E, F32)
                c1, c2, c3 = _split3(cvec)
                e1, e2, e3 = _split3(-cvec * t_glob)
                cp = jnp.where(piece == 0, c1, jnp.where(piece == 1, c2, c3))
                ep = jnp.where(piece == 0, e1, jnp.where(piece == 1, e2, e3))
                top = jnp.where((n_iota >= AUX_T) & (n_iota < AUX_BLK), ep, cp)
                aux_top.append(jnp.where(n_iota < AUX_BLK + 3, top, 0.0).astype(BF16))
                qh_t.append(jnp.where(row_h == h % per_pair, q_hi[pair_lanes(h), :], zero_b))

            def q_aug(h, sel_rows):
                return jnp.concatenate([qh_t[h], aux_top[h], sel_rows, aux_pad], axis=0)

            for h in range(heads):
                st_ref[u, 0, h] = jnp.where(causal, scores(h, key_rows(0), q_aug(h, no_mask)), NEG)
                p_ref[u, 0, h] = jnp.zeros((BS, BS), BF16)
            sel = [jnp.zeros((nb, BS), F32) for _ in range(heads)]
            for rnk in range(MOBA_TOPK):
                rank_ok = jnp.where(rnk < i, 1.0, 0.0)
                for h in range(heads):
                    mx = jnp.max(gate[h], axis=0, keepdims=True)
                    idx = jnp.min(jnp.where(gate[h] == mx, n_iota, nb), axis=0, keepdims=True)
                    pick = n_iota == idx
                    sel[h] = jnp.where(pick, rank_ok, sel[h])
                    gate[h] = jnp.where(pick, -jnp.inf, gate[h])
            return [q_aug(h, jnp.where(sel[h] > 0.0, 0.0, NEG).astype(BF16)) for h in range(heads)]

        def loops(q_augs):
            def stage(n, carry, src, dst):
                ms, accs, alphas = carry
                prev_rows = key_rows(jnp.maximum(n - 1, 0))
                next_rows = key_rows(n + 1)
                pvs = [pv(h, src, prev_rows) for h in range(heads)]
                ms2, alphas2 = [], []
                for h in range(heads):
                    st = st_ref[u, src, h]
                    m_new = jnp.maximum(ms[h], jnp.max(st, axis=0, keepdims=True))
                    alphas2.append(jnp.exp2(ms[h] - m_new))
                    p_ref[u, dst, h] = jnp.exp2((st - m_new).astype(BF16))
                    ms2.append(m_new)
                for h in range(heads):
                    st_ref[u, dst, h] = scores(h, next_rows, q_augs[h])
                accs2 = [alphas[h] * accs[h] + pvs[h] for h in range(heads)]
                return tuple(ms2), tuple(accs2), tuple(alphas2)

            def run_blocks(first, count, carry):
                for j in range(count):
                    last_odd = j == count - 1 and count % 2 == 1
                    carry = stage(first + j, carry, j % 2, 0 if last_odd else 1 - j % 2)
                return carry

            m0 = tuple(jnp.full((1, BS), NEG, F32) for _ in range(heads))
            acc0 = tuple(jnp.zeros((HEAD_DIM + PV_ONES_ROWS, BS), F32) for _ in range(heads))
            alpha0 = tuple(jnp.ones((1, BS), F32) for _ in range(heads))
            n_blocks = i + 1
            done = 0
            carry = (m0, acc0, alpha0)
            for width in MOBA_TRIP_WIDTHS:
                left = (n_blocks - done) // width
                trips = left if width == MOBA_TRIP_WIDTHS[0] else left % 2
                base = done
                carry = lax.fori_loop(
                    0, trips, lambda t, c, w=width, b=base: run_blocks(b + w * t, w, c), carry)
                done = done + trips * width
            return carry

        def epilogue(carry):
            ms, accs, alphas = carry
            last_rows = key_rows(i)
            outs = []
            for h in range(heads):
                acc = alphas[h] * accs[h] + pv(h, 0, last_rows)
                outs.append(acc[:HEAD_DIM] * (1.0 / acc[HEAD_DIM:HEAD_DIM + 1]))
            ot = jnp.concatenate(outs, axis=0)
            o_ref[q_rows, :] = (ot.T * gs_ref[q_rows, :].astype(F32)).astype(BF16)

        return prologue, loops, epilogue

    parts = [query_block(u) for u in range(MOBA_STEP_BLOCKS)]
    carry = None
    for u, (prologue, loops, epilogue) in enumerate(parts):
        q_augs = prologue()
        if u > 0:
            parts[u - 1][2](carry)
        carry = loops(q_augs)
    parts[-1][2](carry)


def _moba(slopes, q, k, key_aux, vt, kmean, gs, batch, seq):
    m = q.shape[0]
    BS = MOBA_BLOCK
    nb = seq // BS
    assert nb == AUX_SEL and AUX_SEL + nb <= LANES and AUX_BLK + 3 <= AUX_SEL and nb % MOBA_STEP_BLOCKS == 0
    H, U = MOBA_STEP_HEADS, MOBA_STEP_BLOCKS
    W = H * HEAD_DIM
    n_groups = ATTN_WIDTH // W
    n_steps = nb // U
    gate_col0 = RWKV_WIDTH // W
    return pl.pallas_call(
        _moba_body,
        grid_spec=pltpu.PrefetchScalarGridSpec(
            num_scalar_prefetch=1,
            grid=(batch, n_groups, n_steps),
            in_specs=[
                pl.BlockSpec((U * BS, W), lambda b, g, i, s: (b * n_steps + i, g)),
                pl.BlockSpec((seq, W), lambda b, g, i, s: (b, g)),
                pl.BlockSpec((seq, LANES), lambda b, g, i, s: (0, 0)),
                pl.BlockSpec((1, W, seq), lambda b, g, i, s: (b, g, 0)),
                pl.BlockSpec((1, nb, W), lambda b, g, i, s: (b, 0, g)),
                pl.BlockSpec((U * BS, W), lambda b, g, i, s: (b * n_steps + i, gate_col0 + g)),
            ],
            out_specs=pl.BlockSpec((U * BS, W), lambda b, g, i, s: (b * n_steps + i, g)),
            scratch_shapes=[
                pltpu.VMEM((U, 2, H, BS, BS), F32),
                pltpu.VMEM((U, 2, H, BS, BS), BF16),
            ],
        ),
        out_shape=jax.ShapeDtypeStruct((m, ATTN_WIDTH), BF16),
        compiler_params=pltpu.CompilerParams(
            dimension_semantics=("arbitrary", "arbitrary", "arbitrary"), vmem_limit_bytes=VMEM_LIMIT),
        name="moba",
    )(slopes, q, k, key_aux, vt, kmean, gs)


def _split_dot(x, ones_b):
    hi, lo = _split2(x)
    return _dot(hi, ones_b) + _dot(lo, ones_b)


def _head_sum(x, ones_b, split=True):
    one = _split_dot if split else (lambda xs, o: _dot(xs.astype(BF16), o))
    parts = [one(x[:, g * GROUP_W:(g + 1) * GROUP_W], ones_b) for g in range(x.shape[1] // GROUP_W)]
    return jnp.concatenate(parts, axis=1)


def _rwkv_body(pr_ref, gs_ref, w0_ref, w2a_hi_ref, w2a_lo_ref, a0_ref, kk_ref, ka_ref, rk_ref, lw_ref,
               lb_ref, x_ref, ya_ref, wor_ref, woa_ref, gp_ref,
               o_ref, s_ref, at_ref, bt_ref, kt_ref, rt_ref, v_ref, gl_ref, y_ref, bonus_ref,
               q_ref, g_ref, h_ref, yr_ref):
    W = RWKV_WIDTH
    n_groups = W // GROUP_W
    n_chunks = RWKV_TILE // CHUNK

    @pl.when(pl.program_id(1) == 0)
    def _():
        s_ref[...] = jnp.zeros_like(s_ref)

    rr = lax.broadcasted_iota(jnp.int32, (GROUP_W, GROUP_W), 0)
    cc = lax.broadcasted_iota(jnp.int32, (GROUP_W, GROUP_W), 1)
    same_head = (rr // HEAD_DIM) == (cc // HEAD_DIM)
    ones_b = same_head.astype(BF16)
    pr_ = lax.broadcasted_iota(jnp.int32, (2 * CHUNK, 4 * CHUNK), 0)
    pc_ = lax.broadcasted_iota(jnp.int32, (2 * CHUNK, 4 * CHUNK), 1)
    tril_pair = ((pr_ // CHUNK == pc_ // (2 * CHUNK)) & (pc_ % CHUNK <= pr_ % CHUNK)).astype(BF16)
    wr = lax.broadcasted_iota(jnp.int32, (CHUNK, GROUP_W), 0)
    wc = lax.broadcasted_iota(jnp.int32, (CHUNK, GROUP_W), 1) % CHUNK
    strict_w = wc < wr
    incl_w = wc <= wr
    eye_w = (wc == wr).astype(F32)
    lane_g = lax.broadcasted_iota(jnp.int32, (1, GROUP_W), 1) // HEAD_DIM
    head_masks = [lane_g == h for h in range(GROUP_HEADS)]

    def mstack(x):
        xb = x.astype(BF16)
        zero = jnp.zeros_like(xb)
        return jnp.concatenate([jnp.where(hm, xb, zero) for hm in head_masks], axis=0)

    def head_rows(xb):
        return jnp.concatenate([xb[:, h * HEAD_DIM:(h + 1) * HEAD_DIM] for h in range(GROUP_HEADS)], axis=0)

    def block_diag(xw):
        return jnp.where(same_head, jnp.concatenate([xw] * GROUP_HEADS, axis=0), jnp.zeros((), BF16))

    HALF = 2 * CHUNK
    SUB_CHUNKS = 4
    RING_CHUNKS = RWKV_RING // CHUNK

    def ring_rows(c, n=1):
        r0 = (c % RING_CHUNKS) * CHUNK
        return slice(r0, r0 + n * CHUNK)

    def ring_slot(c, g):
        return (c % RING_CHUNKS) * n_groups + g

    lane_z = lax.broadcasted_iota(jnp.int32, (1, DECAY_RANK + ICLR_RANK), 1)

    def run(main, side=None, side_start=0):
        gens = [main] + ([side] if side is not None else [])
        stage_no = 0
        while gens:
            for gen in list(gens):
                if gen is side and stage_no < side_start and main in gens:
                    continue
                try:
                    next(gen)
                except StopIteration:
                    gens.remove(gen)
            stage_no += 1

    def seq(*gens):
        for gen in gens:
            yield from gen

    def prep(half):
        hr = slice(half * HALF, (half + 1) * HALF)
        rr_ = ring_rows(2 * half, 2)
        z = pr_ref[hr, 3 * W:3 * W + DECAY_RANK + ICLR_RANK]
        z = jnp.where(lane_z < DECAY_RANK, jnp.tanh(z), z)
        z_hi, z_lo = _split2(z)
        lin = _dot(jnp.concatenate([z_hi, z_lo], axis=1), w2a_hi_ref[...]) + _dot(z_hi, w2a_lo_ref[...])
        k = pr_ref[hr, W:2 * W]
        kkr = k * kk_ref[...]
        ss = _head_sum(kkr * kkr, ones_b, split=False)
        yield
        ld = -EXP_M_HALF * jax.nn.sigmoid(w0_ref[...] + lin[:, :W])
        ld_hi, ld_lo = _split2(ld)
        cs = _dot(tril_pair, jnp.concatenate([ld_hi[:CHUNK], ld_lo[:CHUNK], ld_hi[CHUNK:], ld_lo[CHUNK:]], axis=0))
        a = jax.nn.sigmoid(a0_ref[...] + lin[:, W:])
        k2 = k * (1.0 + (a - 1.0) * ka_ref[...])
        r = pr_ref[hr, 0:W]
        bsum = _head_sum(r * k2 * rk_ref[...], ones_b)
        yield
        kk = kkr * jnp.minimum(lax.rsqrt(ss), 1e12)
        g_inv = jnp.exp(-cs)
        at_ref[rr_, :] = (-kk) * jnp.exp(cs - ld)
        bt_ref[rr_, :] = kk * a * g_inv
        kt_ref[rr_, :] = k2 * g_inv
        yield
        g_incl = jnp.exp(cs)
        v = pr_ref[hr, 2 * W:3 * W]
        rt_ref[rr_, :] = r * g_incl
        v_ref[rr_, :] = v
        gl_ref[rr_, :] = g_incl
        bonus_ref[hr, :] = bsum * v
        yield

    def chain_factors(batch_chains):
        nc = range(len(batch_chains))
        rows = [ring_rows(c) for c, _ in batch_chains]
        yrows = [slice(c * CHUNK, (c + 1) * CHUNK) for c, _ in batch_chains]
        cols = [slice(g * GROUP_W, (g + 1) * GROUP_W) for _, g in batch_chains]
        x = []
        for j in nc:
            lhs = jnp.concatenate([at_ref[rows[j], cols[j]], rt_ref[rows[j], cols[j]]], axis=0).astype(BF16)
            rhs = jnp.concatenate([mstack(bt_ref[rows[j], cols[j]]), mstack(kt_ref[rows[j], cols[j]])], axis=0)
            x.append(_dot(lhs, rhs, _NT))
        yield
        a_ab = [jnp.where(strict_w, x[j][:CHUNK, :GROUP_W], 0.0) for j in nc]
        a_ak = [jnp.where(strict_w, x[j][:CHUNK, GROUP_W:], 0.0).astype(BF16) for j in nc]
        m_rb = [jnp.where(incl_w, x[j][CHUNK:, :GROUP_W], 0.0).astype(BF16) for j in nc]
        m_rk = [jnp.where(incl_w, x[j][CHUNK:, GROUP_W:], 0.0).astype(BF16) for j in nc]
        tw = [a_ab[j] + eye_w for j in nc]
        pw = [a_ab[j].astype(BF16) for j in nc]
        pw = [_dot(pw[j], block_diag(pw[j])).astype(BF16) for j in nc]
        yield
        for it in range(CHUNK.bit_length() - 3):
            both = [_dot(jnp.concatenate([pw[j], tw[j].astype(BF16)], axis=0), block_diag(pw[j])) for j in nc]
            tw = [tw[j] + both[j][CHUNK:] for j in nc]
            pw = [both[j][:CHUNK].astype(BF16) for j in nc]
            yield
        tw = [tw[j] + _dot(tw[j].astype(BF16), block_diag(pw[j])) for j in nc]
        twb = [tw[j].astype(BF16) for j in nc]
        yield
        v_m = [mstack(v_ref[rows[j], cols[j]]) for j in nc]
        wt = [_dot(twb[j], mstack(at_ref[rows[j], cols[j]])) for j in nc]
        x1 = [_dot(a_ak[j], v_m[j]) for j in nc]
        yield
        u0 = [_dot(twb[j], mstack(x1[j])) for j in nc]
        for j in nc:
            q_ref[rows[j], cols[j]] = (rt_ref[rows[j], cols[j]] + _dot(m_rb[j], mstack(wt[j]))).astype(BF16)
        yield
        for j in nc:
            y_ref[yrows[j], cols[j]] = _dot(m_rb[j], mstack(u0[j])) + _dot(m_rk[j], v_m[j])
        yield
        for j in nc:
            c, g = batch_chains[j]
            gl = gl_ref[rows[j].stop - 1:rows[j].stop, cols[j]]
            bh = (bt_ref[rows[j], cols[j]] * gl).astype(BF16)
            kh = (kt_ref[rows[j], cols[j]] * gl).astype(BF16)
            bh_rows, kh_rows = head_rows(bh), head_rows(kh)
            gw = eye_w * gl + _dot(bh_rows, mstack(wt[j]), _TN)
            g_ref[ring_slot(c, g)] = gw.astype(BF16)
            h_ref[ring_slot(c, g)] = _dot(
                jnp.concatenate([bh_rows, kh_rows], axis=0),
                jnp.concatenate([mstack(u0[j]), v_m[j]], axis=0), _TN)
        yield

    def recurrence(c):
        rows = slice(c * CHUNK, (c + 1) * CHUNK)
        for g in range(n_groups):
            cols = slice(g * GROUP_W, (g + 1) * GROUP_W)
            s_b = block_diag(s_ref[g].astype(BF16))
            y_ref[rows, cols] = y_ref[rows, cols] + _dot(q_ref[ring_rows(c), cols], s_b)
            s_ref[g] = _dot(g_ref[ring_slot(c, g)], s_b) + h_ref[ring_slot(c, g)]
        yield

    def finish(half):
        hr = slice(half * HALF, (half + 1) * HALF)
        y = y_ref[hr, :]
        mean = _head_sum(y, ones_b) * (1.0 / HEAD_DIM)
        yield
        yc = y - mean
        var = _head_sum(yc * yc, ones_b) * (1.0 / HEAD_DIM)
        yield
        yn = yc * lax.rsqrt(var + GN_EPS) * lw_ref[...] + lb_ref[...] + bonus_ref[hr, :]
        yr_ref[hr, :] = (yn * gs_ref[hr, :].astype(F32)).astype(BF16)
        yield

    def project(sub):
        sr = slice(sub * SUB_CHUNKS * CHUNK, (sub + 1) * SUB_CHUNKS * CHUNK)
        o = _dot(yr_ref[sr, :], wor_ref[...]) + _dot(ya_ref[sr, :], woa_ref[...])
        yield
        ms = jnp.mean(o * o, axis=-1, keepdims=True)
        o_ref[sr, :] = x_ref[sr, :] + o * lax.rsqrt(ms + RMS_EPS) * gp_ref[...]
        yield

    n_sub = n_chunks // SUB_CHUNKS
    halves = lambda sub: (2 * sub, 2 * sub + 1)
    chunks = lambda sub: range(SUB_CHUNKS * sub, SUB_CHUNKS * (sub + 1))
    tail = lambda sub: [recurrence(c) for c in chunks(sub)] + [finish(hf) for hf in halves(sub)]
    run(seq(*[prep(hf) for hf in halves(0)]))
    for sub in range(n_sub):
        side = [prep(hf) for hf in halves(sub + 1)] if sub + 1 < n_sub else []
        side += tail(sub - 1) if sub > 0 else []
        run(chain_factors([(c, g) for c in chunks(sub) for g in range(n_groups)]), seq(*side), side_start=2)
    run(seq(*tail(n_sub - 1)), seq(*[project(sub) for sub in range(n_sub - 1)]))
    run(project(n_sub - 1))


def _rwkv_out(pr, gs, w0, w2a, a0, k_k, k_a, r_k, lnx_w, lnx_b, x2, ya, w_or, w_oa, g_post, batch, seq):
    m, d = x2.shape
    per_b = seq // RWKV_TILE
    W = RWKV_WIDTH
    n_cg = (RWKV_RING // CHUNK) * (W // GROUP_W)
    w2a_hi = w2a.astype(BF16)
    w2a_lo = (w2a - w2a_hi.astype(F32)).astype(BF16)
    w2a_hi = jnp.concatenate([w2a_hi, w2a_hi], axis=0)
    tile = lambda b, t: (b * per_b + t, 0)
    const = lambda b, t: (0, 0)
    vec = pl.BlockSpec((1, W), const)
    ring_scratch = pltpu.VMEM((RWKV_RING, W), F32)
    tile_scratch = pltpu.VMEM((RWKV_TILE, W), F32)
    return pl.pallas_call(
        _rwkv_body,
        grid=(batch, per_b),
        in_specs=[
            pl.BlockSpec((RWKV_TILE, N_RWKV_COLS), tile),
            pl.BlockSpec((RWKV_TILE, W), tile),
            vec,
            pl.BlockSpec(w2a_hi.shape, const),
            pl.BlockSpec(w2a_lo.shape, const),
            vec, vec, vec, vec, vec, vec,
            pl.BlockSpec((RWKV_TILE, d), tile),
            pl.BlockSpec((RWKV_TILE, ATTN_WIDTH), tile),
            pl.BlockSpec(w_or.shape, const),
            pl.BlockSpec(w_oa.shape, const),
            pl.BlockSpec((1, d), const),
        ],
        out_specs=pl.BlockSpec((RWKV_TILE, d), tile),
        out_shape=jax.ShapeDtypeStruct((m, d), F32),
        scratch_shapes=[
            pltpu.VMEM((W // GROUP_W, HEAD_DIM, GROUP_W), F32),
            ring_scratch, ring_scratch, ring_scratch, ring_scratch, ring_scratch, ring_scratch,
            tile_scratch, tile_scratch,
            pltpu.VMEM((RWKV_RING, W), BF16),
            pltpu.VMEM((n_cg, HEAD_DIM, GROUP_W), BF16),
            pltpu.VMEM((n_cg, HEAD_DIM, GROUP_W), F32),
            pltpu.VMEM((RWKV_TILE, W), BF16),
        ],
        compiler_params=pltpu.CompilerParams(
            dimension_semantics=("arbitrary", "arbitrary"), vmem_limit_bytes=VMEM_LIMIT),
        name="rwkv7_out",
    )(pr, gs, w0, w2a_hi, w2a_lo, a0, k_k, k_a, r_k, lnx_w, lnx_b, x2, ya, w_or, w_oa, g_post)


def kernel(x, g_pre, w_in, tshift_mu, w0, w2, a0, a2, k_k, k_a, r_k, lnx_w, lnx_b, w_out, g_post):
    batch, seq, d = x.shape
    depth = w_in.shape[0]
    assert seq % RWKV_TILE == 0 and seq % IN_TILE == 0 and IN_TILE % MOBA_BLOCK == 0 and d % LANES == 0
    x2 = x.reshape(batch * seq, d)
    slopes = 2.0 ** (-8.0 * jnp.arange(1, ATTN_HEADS + 1, dtype=F32) / ATTN_HEADS)
    row = lambda z: z.reshape(1, -1).astype(F32)
    for l in range(depth):
        wb = w_in[l].astype(BF16)
        w_r = wb[:, :N_RWKV_COLS]
        w_a = wb[:, N_RWKV_COLS:N_RWKV_COLS + N_ATTN_COLS]
        w_g = wb[:, N_RWKV_COLS + N_ATTN_COLS:]
        pr, q, k, vt, kmean, gs = _in_proj(x2, row(g_pre[l]), row(tshift_mu[l]), w_r, w_a, w_g, batch, seq)
        ya = _moba(slopes, q, k, _moba_key_aux(seq), vt,
                   kmean.reshape(batch, seq // MOBA_BLOCK, ATTN_WIDTH), gs, batch, seq)
        zeros = jnp.zeros((DECAY_RANK, RWKV_WIDTH), F32)
        w2a = jnp.concatenate([jnp.concatenate([w2[l], zeros], axis=1),
                               jnp.concatenate([zeros, a2[l]], axis=1)], axis=0)
        wo = w_out[l].astype(BF16)
        x2 = _rwkv_out(pr, gs, row(w0[l]), w2a, row(a0[l]), row(k_k[l]), row(k_a[l]), row(r_k[l]),
                       row(lnx_w[l]), row(lnx_b[l]), x2, ya, wo[:RWKV_WIDTH], wo[RWKV_WIDTH:],
                       row(g_post[l]), batch, seq)
    return x2.reshape(batch, seq, d)
```

```python
import functools

import jax
import jax.numpy as jnp
from jax import lax
from jax.experimental import pallas as pl
from jax.experimental.pallas import tpu as pltpu

F32 = jnp.float32
BF16 = jnp.bfloat16

HEAD_DIM = 64
RWKV_HEADS = 8
ATTN_HEADS = 8
RWKV_WIDTH = RWKV_HEADS * HEAD_DIM
ATTN_WIDTH = ATTN_HEADS * HEAD_DIM
D_MIX = RWKV_WIDTH + ATTN_WIDTH
DECAY_RANK = 64
ICLR_RANK = 64
MOBA_BLOCK = 256
MOBA_TOPK = 3
RMS_EPS = 1e-6
GN_EPS = 64e-5
EXP_M_HALF = 0.6065306597126334
NEG = -1e30
N_RWKV_COLS = 3 * RWKV_WIDTH + DECAY_RANK + ICLR_RANK
N_ATTN_COLS = 3 * ATTN_WIDTH

LANES = 128
MXU_DIM = 256
CHUNK = 64
GROUP_HEADS = MXU_DIM // HEAD_DIM
GROUP_W = GROUP_HEADS * HEAD_DIM
IN_TILE = 512
RWKV_TILE = 512
RWKV_RING = 512
VMEM_LIMIT = 48 * 1024 * 1024

_NT = (((1,), (1,)), ((), ()))
_TN = (((0,), (0,)), ((), ()))


def _dot(a, b, dims=None, precision=None):
    if dims is None:
        return jnp.dot(a, b, preferred_element_type=F32, precision=precision)
    return lax.dot_general(a, b, dims, preferred_element_type=F32, precision=precision)


def _inproj_body(per_b, x_ref, g_ref, mu_ref, wr_ref, wa_ref, wg_ref,
                 pr_ref, q_ref, k_ref, vt_ref, km_ref, gs_ref, prev_ref):
    x = x_ref[...]
    ms = jnp.mean(x * x, axis=-1, keepdims=True)
    h = (x * lax.rsqrt(ms + RMS_EPS) * g_ref[...]).astype(BF16)
    @pl.when(pl.program_id(0) % per_b == 0)
    def _():
        prev_ref[...] = jnp.zeros_like(prev_ref)

    row0 = lax.broadcasted_iota(jnp.int32, (IN_TILE, 1), 0) == 0
    for c0 in range(0, N_RWKV_COLS, MXU_DIM):
        cs = slice(c0, min(c0 + MXU_DIM, N_RWKV_COLS))
        p = _dot(h, wr_ref[:, cs])
        p_prev = jnp.where(row0, prev_ref[:, cs], pltpu.roll(p, 1, axis=0))
        prev_ref[:, cs] = p[IN_TILE - 1:IN_TILE, :]
        pr_ref[:, cs] = p + (p_prev - p) * mu_ref[:, cs]
    qkv = _dot(h, wa_ref[...])
    q_ref[...] = qkv[:, :ATTN_WIDTH] * (HEAD_DIM ** -0.5)
    kf = qkv[:, ATTN_WIDTH:2 * ATTN_WIDTH]
    k_ref[...] = kf.astype(BF16)
    for j in range(IN_TILE // MOBA_BLOCK):
        km_ref[j] = jnp.mean(kf[j * MOBA_BLOCK:(j + 1) * MOBA_BLOCK], axis=0, keepdims=True)
    vt_ref[0] = qkv[:, 2 * ATTN_WIDTH:].T.astype(BF16)
    g = _dot(h, wg_ref[...])
    gs_ref[...] = (g * jax.nn.sigmoid(g)).astype(BF16)


def _in_proj(x2, g_pre, mu, w_r, w_a, w_g, batch, seq):
    m, d = x2.shape
    nblk = m // IN_TILE
    per_b = seq // IN_TILE
    blocks = IN_TILE // MOBA_BLOCK
    const = lambda i: (0, 0)
    return pl.pallas_call(
        functools.partial(_inproj_body, per_b),
        grid=(nblk,),
        in_specs=[
            pl.BlockSpec((IN_TILE, d), lambda i: (i, 0)),
            pl.BlockSpec((1, d), const),
            pl.BlockSpec((1, N_RWKV_COLS), const),
            pl.BlockSpec(w_r.shape, const),
            pl.BlockSpec(w_a.shape, const),
            pl.BlockSpec(w_g.shape, const),
        ],
        out_specs=[
            pl.BlockSpec((IN_TILE, N_RWKV_COLS), lambda i: (i, 0)),
            pl.BlockSpec((IN_TILE, ATTN_WIDTH), lambda i: (i, 0)),
            pl.BlockSpec((IN_TILE, ATTN_WIDTH), lambda i: (i, 0)),
            pl.BlockSpec((1, ATTN_WIDTH, IN_TILE), lambda i: (i // per_b, 0, i % per_b)),
            pl.BlockSpec((blocks, 1, ATTN_WIDTH), lambda i: (i, 0, 0)),
            pl.BlockSpec((IN_TILE, D_MIX), lambda i: (i, 0)),
        ],
        out_shape=[
            jax.ShapeDtypeStruct((m, N_RWKV_COLS), F32),
            jax.ShapeDtypeStruct((m, ATTN_WIDTH), F32),
            jax.ShapeDtypeStruct((m, ATTN_WIDTH), BF16),
            jax.ShapeDtypeStruct((batch, ATTN_WIDTH, seq), BF16),
            jax.ShapeDtypeStruct((m // MOBA_BLOCK, 1, ATTN_WIDTH), F32),
            jax.ShapeDtypeStruct((m, D_MIX), BF16),
        ],
        scratch_shapes=[pltpu.VMEM((1, N_RWKV_COLS), F32)],
        compiler_params=pltpu.CompilerParams(
            dimension_semantics=("arbitrary",), vmem_limit_bytes=VMEM_LIMIT),
        name="in_proj",
    )(x2, g_pre, mu, w_r, w_a, w_g)


LOG2E = 1.4426950408889634
AUX_S, AUX_T, AUX_BLK, AUX_SEL = 0, 3, 6, 16
PV_ONES_ROWS = 16
MOBA_STEP_HEADS = 4
MOBA_STEP_BLOCKS = 2
MOBA_TRIP_WIDTHS = (4, 2, 1)


def _moba_key_aux(seq):
    pos = jnp.arange(seq, dtype=jnp.int32)[:, None]
    lane = jnp.arange(LANES, dtype=jnp.int32)[None, :]
    blk = pos // MOBA_BLOCK
    tab = jnp.where(lane < AUX_T, pos % MOBA_BLOCK,
                    jnp.where(lane < AUX_BLK, 1,
                              jnp.where(lane < AUX_BLK + 3, blk * MOBA_BLOCK,
                                        jnp.where(lane == AUX_SEL + blk, 1, 0))))
    return tab.astype(BF16)


def _split2(x):
    hi = x.astype(BF16)
    lo = (x - hi.astype(F32)).astype(BF16)
    return hi, lo


def _split3(x):
    x1 = x.astype(BF16).astype(F32)
    r1 = x - x1
    x2 = r1.astype(BF16).astype(F32)
    x3 = (r1 - x2).astype(BF16).astype(F32)
    return x1, x2, x3


def _moba_body(slopes_ref, q_ref, k_ref, ak_ref, vt_ref, km_ref, gs_ref, o_ref, st_ref, p_ref):
    BS = MOBA_BLOCK
    grp = pl.program_id(1)
    step = pl.program_id(2)
    heads = MOBA_STEP_HEADS
    per_pair = LANES // HEAD_DIM
    nb = km_ref.shape[1]
    n_iota = lax.broadcasted_iota(jnp.int32, (nb, BS), 0)
    causal = (lax.broadcasted_iota(jnp.int32, (BS, BS), 0)
              <= lax.broadcasted_iota(jnp.int32, (BS, BS), 1))
    piece = n_iota % 3
    ones_rows = jnp.ones((PV_ONES_ROWS, BS), BF16)
    lane_head = lax.broadcasted_iota(jnp.int32, (1, heads * HEAD_DIM), 1) // HEAD_DIM
    row_h = lax.broadcasted_iota(jnp.int32, (LANES, 1), 0) // HEAD_DIM
    zero_b = jnp.zeros((), BF16)
    aux_pad = jnp.zeros((LANES - AUX_SEL - nb, BS), BF16)
    no_mask = jnp.zeros((nb, BS), BF16)
    km = km_ref[0]
    km_stack = jnp.concatenate([jnp.where(lane_head == h, km, 0.0) for h in range(heads)], axis=0)
    km_hi, km_lo = _split2(km_stack)
    km_cat = jnp.concatenate([km_hi, km_lo, km_hi], axis=1)

    def pair_lanes(h):
        p = h // per_pair
        return slice(p * LANES, (p + 1) * LANES)

    def query_block(u):
        i = MOBA_STEP_BLOCKS * step + u
        q_rows = slice(u * BS, (u + 1) * BS)

        def key_rows(n):
            b = jnp.where(n == 0, i, jnp.minimum(n - 1, i))
            return pl.ds(pl.multiple_of(b * BS, BS), BS)

        def scores(h, blk, q_aug_t):
            k_aug = jnp.concatenate([k_ref[blk, pair_lanes(h)], ak_ref[blk, :]], axis=1)
            return _dot(k_aug, q_aug_t)

        def pv(h, slot, blk):
            vt_aug = jnp.concatenate([vt_ref[0, h * HEAD_DIM:(h + 1) * HEAD_DIM, blk], ones_rows], axis=0)
            return _dot(vt_aug, p_ref[u, slot, h])

        def prologue():
            q_t = (q_ref[q_rows, :] * LOG2E).T
            q_hi, q_lo = _split2(q_t)
            gates = _dot(km_cat, jnp.concatenate([q_hi, q_hi, q_lo], axis=0))
            gate = [jnp.where(n_iota < i, gates[h * nb:(h + 1) * nb], NEG) for h in range(heads)]
            t_glob = (lax.broadcasted_iota(jnp.int32, (nb, BS), 1) + i * BS).astype(F32)
            qh_t, aux_top = [], []
            for h in range(heads):
                cvec = jnp.full((nb, BS), slopes_ref[grp * heads + h] * LOG2E, F32)
                c1, c2, c3 = _split3(cvec)
                e1, e2, e3 = _split3(-cvec * t_glob)
                cp = jnp.where(piece == 0, c1, jnp.where(piece == 1, c2, c3))
                ep = jnp.where(piece == 0, e1, jnp.where(piece == 1, e2, e3))
                top = jnp.where((n_iota >= AUX_T) & (n_iota < AUX_BLK), ep, cp)
                aux_top.append(jnp.where(n_iota < AUX_BLK + 3, top, 0.0).astype(BF16))
                qh_t.append(jnp.where(row_h == h % per_pair, q_hi[pair_lanes(h), :], zero_b))

            def q_aug(h, sel_rows):
                return jnp.concatenate([qh_t[h], aux_top[h], sel_rows, aux_pad], axis=0)

            own = [jnp.where(causal, scores(h, key_rows(0), q_aug(h, no_mask)), NEG) for h in range(heads)]
            sel = [jnp.zeros((nb, BS), F32) for _ in range(heads)]
            for rnk in range(MOBA_TOPK):
                rank_ok = jnp.where(rnk < i, 1.0, 0.0)
                for h in range(heads):
                    mx = jnp.max(gate[h], axis=0, keepdims=True)
                    idx = jnp.min(jnp.where(gate[h] == mx, n_iota, nb), axis=0, keepdims=True)
                    pick = n_iota == idx
                    sel[h] = jnp.where(pick, rank_ok, sel[h])
                    gate[h] = jnp.where(pick, -jnp.inf, gate[h])
            q_augs = [q_aug(h, jnp.where(sel[h] > 0.0, 0.0, NEG).astype(BF16)) for h in range(heads)]
            m_own = []
            for h in range(heads):
                m_own.append(jnp.max(own[h], axis=0, keepdims=True))
                p_ref[u, 0, h] = jnp.exp2(own[h] - m_own[h]).astype(BF16)
                st_ref[u, 0, h] = scores(h, key_rows(1), q_augs[h])
            return q_augs, tuple(m_own)

        def loops(q_augs, m_own):
            def stage(n, carry, src, dst):
                ms, accs, alphas = carry
                prev_rows = key_rows(n - 1)
                next_rows = key_rows(n + 1)
                pvs = [pv(h, src, prev_rows) for h in range(heads)]
                ms2, alphas2 = [], []
                for h in range(heads):
                    st = st_ref[u, src, h]
                    m_new = jnp.maximum(ms[h], jnp.max(st, axis=0, keepdims=True))
                    alphas2.append(jnp.exp2(ms[h] - m_new))
                    p_ref[u, dst, h] = jnp.exp2(st - m_new).astype(BF16)
                    ms2.append(m_new)
                for h in range(heads):
                    st_ref[u, dst, h] = scores(h, next_rows, q_augs[h])
                accs2 = [alphas[h] * accs[h] + pvs[h] for h in range(heads)]
                return tuple(ms2), tuple(accs2), tuple(alphas2)

            def run_blocks(first, count, carry):
                for j in range(count):
                    last_odd = j == count - 1 and count % 2 == 1
                    carry = stage(first + j, carry, j % 2, 0 if last_odd else 1 - j % 2)
                return carry

            acc0 = tuple(jnp.zeros((HEAD_DIM + PV_ONES_ROWS, BS), F32) for _ in range(heads))
            alpha0 = tuple(jnp.ones((1, BS), F32) for _ in range(heads))
            n_blocks = i + 1
            done = 1
            carry = (m_own, acc0, alpha0)
            for width in MOBA_TRIP_WIDTHS:
                left = (n_blocks - done) // width
                trips = left if width == MOBA_TRIP_WIDTHS[0] else left % 2
                base = done
                carry = lax.fori_loop(
                    0, trips, lambda t, c, w=width, b=base: run_blocks(b + w * t, w, c), carry)
                done = done + trips * width
            return carry

        def epilogue(carry):
            ms, accs, alphas = carry
            last_rows = key_rows(i)
            outs = []
            for h in range(heads):
                acc = alphas[h] * accs[h] + pv(h, 0, last_rows)
                outs.append(acc[:HEAD_DIM] * (1.0 / acc[HEAD_DIM:HEAD_DIM + 1]))
            ot = jnp.concatenate(outs, axis=0)
            o_ref[q_rows, :] = (ot.T * gs_ref[q_rows, :].astype(F32)).astype(BF16)

        return prologue, loops, epilogue

    parts = [query_block(u) for u in range(MOBA_STEP_BLOCKS)]
    carry = None
    for u, (prologue, loops, epilogue) in enumerate(parts):
        q_augs, m_own = prologue()
        if u > 0:
            parts[u - 1][2](carry)
        carry = loops(q_augs, m_own)
    parts[-1][2](carry)


def _moba(slopes, q, k, key_aux, vt, kmean, gs, batch, seq):
    m = q.shape[0]
    BS = MOBA_BLOCK
    nb = seq // BS
    assert nb == AUX_SEL and AUX_SEL + nb <= LANES and AUX_BLK + 3 <= AUX_SEL and nb % MOBA_STEP_BLOCKS == 0
    H, U = MOBA_STEP_HEADS, MOBA_STEP_BLOCKS
    W = H * HEAD_DIM
    n_groups = ATTN_WIDTH // W
    n_steps = nb // U
    gate_col0 = RWKV_WIDTH // W
    return pl.pallas_call(
        _moba_body,
        grid_spec=pltpu.PrefetchScalarGridSpec(
            num_scalar_prefetch=1,
            grid=(batch, n_groups, n_steps),
            in_specs=[
                pl.BlockSpec((U * BS, W), lambda b, g, i, s: (b * n_steps + i, g)),
                pl.BlockSpec((seq, W), lambda b, g, i, s: (b, g)),
                pl.BlockSpec((seq, LANES), lambda b, g, i, s: (0, 0)),
                pl.BlockSpec((1, W, seq), lambda b, g, i, s: (b, g, 0)),
                pl.BlockSpec((1, nb, W), lambda b, g, i, s: (b, 0, g)),
                pl.BlockSpec((U * BS, W), lambda b, g, i, s: (b * n_steps + i, gate_col0 + g)),
            ],
            out_specs=pl.BlockSpec((U * BS, W), lambda b, g, i, s: (b * n_steps + i, g)),
            scratch_shapes=[
                pltpu.VMEM((U, 2, H, BS, BS), F32),
                pltpu.VMEM((U, 2, H, BS, BS), BF16),
            ],
        ),
        out_shape=jax.ShapeDtypeStruct((m, ATTN_WIDTH), BF16),
        compiler_params=pltpu.CompilerParams(
            dimension_semantics=("arbitrary", "arbitrary", "arbitrary"), vmem_limit_bytes=VMEM_LIMIT),
        name="moba",
    )(slopes, q, k, key_aux, vt, kmean, gs)


def _split_dot(x, ones_b):
    hi, lo = _split2(x)
    return _dot(hi, ones_b) + _dot(lo, ones_b)


def _head_sum(x, ones_b, split=True):
    one = _split_dot if split else (lambda xs, o: _dot(xs.astype(BF16), o))
    parts = [one(x[:, g * GROUP_W:(g + 1) * GROUP_W], ones_b) for g in range(x.shape[1] // GROUP_W)]
    return jnp.concatenate(parts, axis=1)


def _rwkv_body(pr_ref, gs_ref, w0_ref, w2a_hi_ref, w2a_lo_ref, a0_ref, kk_ref, ka_ref, rk_ref, lw_ref,
               lb_ref, x_ref, ya_ref, wor_ref, woa_ref, gp_ref,
               o_ref, s_ref, at_ref, bt_ref, kt_ref, rt_ref, v_ref, gl_ref, y_ref, bonus_ref,
               q_ref, g_ref, h_ref, yr_ref):
    W = RWKV_WIDTH
    n_groups = W // GROUP_W
    n_chunks = RWKV_TILE // CHUNK

    @pl.when(pl.program_id(1) == 0)
    def _():
        s_ref[...] = jnp.zeros_like(s_ref)

    rr = lax.broadcasted_iota(jnp.int32, (GROUP_W, GROUP_W), 0)
    cc = lax.broadcasted_iota(jnp.int32, (GROUP_W, GROUP_W), 1)
    same_head = (rr // HEAD_DIM) == (cc // HEAD_DIM)
    ones_b = same_head.astype(BF16)
    pr_ = lax.broadcasted_iota(jnp.int32, (2 * CHUNK, 4 * CHUNK), 0)
    pc_ = lax.broadcasted_iota(jnp.int32, (2 * CHUNK, 4 * CHUNK), 1)
    tril_pair = ((pr_ // CHUNK == pc_ // (2 * CHUNK)) & (pc_ % CHUNK <= pr_ % CHUNK)).astype(BF16)
    wr = lax.broadcasted_iota(jnp.int32, (CHUNK, GROUP_W), 0)
    wc = lax.broadcasted_iota(jnp.int32, (CHUNK, GROUP_W), 1) % CHUNK
    strict_w = wc < wr
    incl_w = wc <= wr
    eye_w = (wc == wr).astype(F32)
    lane_g = lax.broadcasted_iota(jnp.int32, (1, GROUP_W), 1) // HEAD_DIM
    head_masks = [lane_g == h for h in range(GROUP_HEADS)]

    def mstack(x):
        xb = x.astype(BF16)
        zero = jnp.zeros_like(xb)
        return jnp.concatenate([jnp.where(hm, xb, zero) for hm in head_masks], axis=0)

    def head_rows(xb):
        return jnp.concatenate([xb[:, h * HEAD_DIM:(h + 1) * HEAD_DIM] for h in range(GROUP_HEADS)], axis=0)

    def block_diag(xw):
        return jnp.where(same_head, jnp.concatenate([xw] * GROUP_HEADS, axis=0), jnp.zeros((), BF16))

    HALF = 2 * CHUNK
    SUB_CHUNKS = 4
    RING_CHUNKS = RWKV_RING // CHUNK

    def ring_rows(c, n=1):
        r0 = (c % RING_CHUNKS) * CHUNK
        return slice(r0, r0 + n * CHUNK)

    def ring_slot(c, g):
        return (c % RING_CHUNKS) * n_groups + g

    lane_z = lax.broadcasted_iota(jnp.int32, (1, DECAY_RANK + ICLR_RANK), 1)

    def run(main, side=None, side_start=0):
        gens = [main] + ([side] if side is not None else [])
        stage_no = 0
        while gens:
            for gen in list(gens):
                if gen is side and stage_no < side_start and main in gens:
                    continue
                try:
                    next(gen)
                except StopIteration:
                    gens.remove(gen)
            stage_no += 1

    def seq(*gens):
        for gen in gens:
            yield from gen

    def prep(half):
        hr = slice(half * HALF, (half + 1) * HALF)
        rr_ = ring_rows(2 * half, 2)
        z = pr_ref[hr, 3 * W:3 * W + DECAY_RANK + ICLR_RANK]
        z = jnp.where(lane_z < DECAY_RANK, jnp.tanh(z), z)
        z_hi, z_lo = _split2(z)
        lin = _dot(jnp.concatenate([z_hi, z_lo], axis=1), w2a_hi_ref[...]) + _dot(z_hi, w2a_lo_ref[...])
        k = pr_ref[hr, W:2 * W]
        kkr = k * kk_ref[...]
        ss = _head_sum(kkr * kkr, ones_b, split=False)
        yield
        ld = -EXP_M_HALF * jax.nn.sigmoid(w0_ref[...] + lin[:, :W])
        ld_hi, ld_lo = _split2(ld)
        cs = _dot(tril_pair, jnp.concatenate([ld_hi[:CHUNK], ld_lo[:CHUNK], ld_hi[CHUNK:], ld_lo[CHUNK:]], axis=0))
        a = jax.nn.sigmoid(a0_ref[...] + lin[:, W:])
        k2 = k * (1.0 + (a - 1.0) * ka_ref[...])
        r = pr_ref[hr, 0:W]
        bsum = _head_sum(r * k2 * rk_ref[...], ones_b)
        yield
        kk = kkr * jnp.minimum(lax.rsqrt(ss), 1e12)
        g_inv = jnp.exp(-cs)
        at_ref[rr_, :] = (-kk) * jnp.exp(cs - ld)
        bt_ref[rr_, :] = kk * a * g_inv
        kt_ref[rr_, :] = k2 * g_inv
        yield
        g_incl = jnp.exp(cs)
        v = pr_ref[hr, 2 * W:3 * W]
        rt_ref[rr_, :] = r * g_incl
        v_ref[rr_, :] = v
        gl_ref[rr_, :] = g_incl
        bonus_ref[hr, :] = bsum * v
        yield

    def chain_factors(batch_chains):
        nc = range(len(batch_chains))
        rows = [ring_rows(c) for c, _ in batch_chains]
        yrows = [slice(c * CHUNK, (c + 1) * CHUNK) for c, _ in batch_chains]
        cols = [slice(g * GROUP_W, (g + 1) * GROUP_W) for _, g in batch_chains]
        x = []
        for j in nc:
            lhs = jnp.concatenate([at_ref[rows[j], cols[j]], rt_ref[rows[j], cols[j]]], axis=0).astype(BF16)
            rhs = jnp.concatenate([mstack(bt_ref[rows[j], cols[j]]), mstack(kt_ref[rows[j], cols[j]])], axis=0)
            x.append(_dot(lhs, rhs, _NT))
        yield
        a_ab = [jnp.where(strict_w, x[j][:CHUNK, :GROUP_W], 0.0) for j in nc]
        a_ak = [jnp.where(strict_w, x[j][:CHUNK, GROUP_W:], 0.0).astype(BF16) for j in nc]
        m_rb = [jnp.where(incl_w, x[j][CHUNK:, :GROUP_W], 0.0).astype(BF16) for j in nc]
        m_rk = [jnp.where(incl_w, x[j][CHUNK:, GROUP_W:], 0.0).astype(BF16) for j in nc]
        tw = [a_ab[j] + eye_w for j in nc]
        pw = [a_ab[j].astype(BF16) for j in nc]
        pw = [_dot(pw[j], block_diag(pw[j])).astype(BF16) for j in nc]
        yield
        for it in range(CHUNK.bit_length() - 3):
            both = [_dot(jnp.concatenate([pw[j], tw[j].astype(BF16)], axis=0), block_diag(pw[j])) for j in nc]
            tw = [tw[j] + both[j][CHUNK:] for j in nc]
            pw = [both[j][:CHUNK].astype(BF16) for j in nc]
            yield
        tw = [tw[j] + _dot(tw[j].astype(BF16), block_diag(pw[j])) for j in nc]
        twb = [tw[j].astype(BF16) for j in nc]
        yield
        v_m = [mstack(v_ref[rows[j], cols[j]]) for j in nc]
        wt = [_dot(twb[j], mstack(at_ref[rows[j], cols[j]])) for j in nc]
        x1 = [_dot(a_ak[j], v_m[j]) for j in nc]
        yield
        u0 = [_dot(twb[j], mstack(x1[j])) for j in nc]
        for j in nc:
            q_ref[rows[j], cols[j]] = (rt_ref[rows[j], cols[j]] + _dot(m_rb[j], mstack(wt[j]))).astype(BF16)
        yield
        for j in nc:
            y_ref[yrows[j], cols[j]] = _dot(m_rb[j], mstack(u0[j])) + _dot(m_rk[j], v_m[j])
        yield
        for j in nc:
            c, g = batch_chains[j]
            gl = gl_ref[rows[j].stop - 1:rows[j].stop, cols[j]]
            bh = (bt_ref[rows[j], cols[j]] * gl).astype(BF16)
            kh = (kt_ref[rows[j], cols[j]] * gl).astype(BF16)
            bh_rows, kh_rows = head_rows(bh), head_rows(kh)
            gw = eye_w * gl + _dot(bh_rows, mstack(wt[j]), _TN)
            g_ref[ring_slot(c, g)] = gw.astype(BF16)
            h_ref[ring_slot(c, g)] = _dot(
                jnp.concatenate([bh_rows, kh_rows], axis=0),
                jnp.concatenate([mstack(u0[j]), v_m[j]], axis=0), _TN)
        yield

    def recurrence(c):
        rows = slice(c * CHUNK, (c + 1) * CHUNK)
        for g in range(n_groups):
            cols = slice(g * GROUP_W, (g + 1) * GROUP_W)
            s_b = block_diag(s_ref[g].astype(BF16))
            y_ref[rows, cols] = y_ref[rows, cols] + _dot(q_ref[ring_rows(c), cols], s_b)
            s_ref[g] = _dot(g_ref[ring_slot(c, g)], s_b) + h_ref[ring_slot(c, g)]
        yield

    def finish(half):
        hr = slice(half * HALF, (half + 1) * HALF)
        y = y_ref[hr, :]
        mean = _head_sum(y, ones_b) * (1.0 / HEAD_DIM)
        yield
        yc = y - mean
        var = _head_sum(yc * yc, ones_b) * (1.0 / HEAD_DIM)
        yield
        yn = yc * lax.rsqrt(var + GN_EPS) * lw_ref[...] + lb_ref[...] + bonus_ref[hr, :]
        yr_ref[hr, :] = (yn * gs_ref[hr, :].astype(F32)).astype(BF16)
        yield

    def project(sub):
        sr = slice(sub * SUB_CHUNKS * CHUNK, (sub + 1) * SUB_CHUNKS * CHUNK)
        o = _dot(yr_ref[sr, :], wor_ref[...]) + _dot(ya_ref[sr, :], woa_ref[...])
        yield
        ms = jnp.mean(o * o, axis=-1, keepdims=True)
        o_ref[sr, :] = x_ref[sr, :] + o * lax.rsqrt(ms + RMS_EPS) * gp_ref[...]
        yield

    n_sub = n_chunks // SUB_CHUNKS
    halves = lambda sub: (2 * sub, 2 * sub + 1)
    chunks = lambda sub: range(SUB_CHUNKS * sub, SUB_CHUNKS * (sub + 1))
    tail = lambda sub: [recurrence(c) for c in chunks(sub)] + [finish(hf) for hf in halves(sub)]
    run(seq(*[prep(hf) for hf in halves(0)]))
    for sub in range(n_sub):
        side = [prep(hf) for hf in halves(sub + 1)] if sub + 1 < n_sub else []
        side += tail(sub - 1) if sub > 0 else []
        run(chain_factors([(c, g) for c in chunks(sub) for g in range(n_groups)]), seq(*side), side_start=2)
    run(seq(*tail(n_sub - 1)), seq(*[project(sub) for sub in range(n_sub - 1)]))
    run(project(n_sub - 1))


def _rwkv_out(pr, gs, w0, w2a, a0, k_k, k_a, r_k, lnx_w, lnx_b, x2, ya, w_or, w_oa, g_post, batch, seq):
    m, d = x2.shape
    per_b = seq // RWKV_TILE
    W = RWKV_WIDTH
    n_cg = (RWKV_RING // CHUNK) * (W // GROUP_W)
    w2a_hi = w2a.astype(BF16)
    w2a_lo = (w2a - w2a_hi.astype(F32)).astype(BF16)
    w2a_hi = jnp.concatenate([w2a_hi, w2a_hi], axis=0)
    tile = lambda b, t: (b * per_b + t, 0)
    const = lambda b, t: (0, 0)
    vec = pl.BlockSpec((1, W), const)
    ring_scratch = pltpu.VMEM((RWKV_RING, W), F32)
    tile_scratch = pltpu.VMEM((RWKV_TILE, W), F32)
    return pl.pallas_call(
        _rwkv_body,
        grid=(batch, per_b),
        in_specs=[
            pl.BlockSpec((RWKV_TILE, N_RWKV_COLS), tile),
            pl.BlockSpec((RWKV_TILE, W), tile),
            vec,
            pl.BlockSpec(w2a_hi.shape, const),
            pl.BlockSpec(w2a_lo.shape, const),
            vec, vec, vec, vec, vec, vec,
            pl.BlockSpec((RWKV_TILE, d), tile),
            pl.BlockSpec((RWKV_TILE, ATTN_WIDTH), tile),
            pl.BlockSpec(w_or.shape, const),
            pl.BlockSpec(w_oa.shape, const),
            pl.BlockSpec((1, d), const),
        ],
        out_specs=pl.BlockSpec((RWKV_TILE, d), tile),
        out_shape=jax.ShapeDtypeStruct((m, d), F32),
        scratch_shapes=[
            pltpu.VMEM((W // GROUP_W, HEAD_DIM, GROUP_W), F32),
            ring_scratch, ring_scratch, ring_scratch, ring_scratch, ring_scratch, ring_scratch,
            tile_scratch, tile_scratch,
            pltpu.VMEM((RWKV_RING, W), BF16),
            pltpu.VMEM((n_cg, HEAD_DIM, GROUP_W), BF16),
            pltpu.VMEM((n_cg, HEAD_DIM, GROUP_W), F32),
            pltpu.VMEM((RWKV_TILE, W), BF16),
        ],
        compiler_params=pltpu.CompilerParams(
            dimension_semantics=("arbitrary", "arbitrary"), vmem_limit_bytes=VMEM_LIMIT),
        name="rwkv7_out",
    )(pr, gs, w0, w2a_hi, w2a_lo, a0, k_k, k_a, r_k, lnx_w, lnx_b, x2, ya, w_or, w_oa, g_post)


def kernel(x, g_pre, w_in, tshift_mu, w0, w2, a0, a2, k_k, k_a, r_k, lnx_w, lnx_b, w_out, g_post):
    batch, seq, d = x.shape
    depth = w_in.shape[0]
    assert seq % RWKV_TILE == 0 and seq % IN_TILE == 0 and IN_TILE % MOBA_BLOCK == 0 and d % LANES == 0
    x2 = x.reshape(batch * seq, d)
    slopes = 2.0 ** (-8.0 * jnp.arange(1, ATTN_HEADS + 1, dtype=F32) / ATTN_HEADS)
    row = lambda z: z.reshape(1, -1).astype(F32)
    for l in range(depth):
        wb = w_in[l].astype(BF16)
        w_r = wb[:, :N_RWKV_COLS]
        w_a = wb[:, N_RWKV_COLS:N_RWKV_COLS + N_ATTN_COLS]
        w_g = wb[:, N_RWKV_COLS + N_ATTN_COLS:]
        pr, q, k, vt, kmean, gs = _in_proj(x2, row(g_pre[l]), row(tshift_mu[l]), w_r, w_a, w_g, batch, seq)
        ya = _moba(slopes, q, k, _moba_key_aux(seq), vt,
                   kmean.reshape(batch, seq // MOBA_BLOCK, ATTN_WIDTH), gs, batch, seq)
        zeros = jnp.zeros((DECAY_RANK, RWKV_WIDTH), F32)
        w2a = jnp.concatenate([jnp.concatenate([w2[l], zeros], axis=1),
                               jnp.concatenate([zeros, a2[l]], axis=1)], axis=0)
        wo = w_out[l].astype(BF16)
        x2 = _rwkv_out(pr, gs, row(w0[l]), w2a, row(a0[l]), row(k_k[l]), row(k_a[l]), row(r_k[l]),
                       row(lnx_w[l]), row(lnx_b[l]), x2, ya, wo[:RWKV_WIDTH], wo[RWKV_WIDTH:],
                       row(g_post[l]), batch, seq)
    return x2.reshape(batch, seq, d)
```

```python
import functools

import jax
import jax.numpy as jnp
from jax import lax
from jax.experimental import pallas as pl
from jax.experimental.pallas import tpu as pltpu

F32 = jnp.float32
BF16 = jnp.bfloat16

HEAD_DIM = 64
RWKV_HEADS = 8
ATTN_HEADS = 8
RWKV_WIDTH = RWKV_HEADS * HEAD_DIM
ATTN_WIDTH = ATTN_HEADS * HEAD_DIM
D_MIX = RWKV_WIDTH + ATTN_WIDTH
DECAY_RANK = 64
ICLR_RANK = 64
MOBA_BLOCK = 256
MOBA_TOPK = 3
RMS_EPS = 1e-6
GN_EPS = 64e-5
EXP_M_HALF = 0.6065306597126334
NEG = -1e30
N_RWKV_COLS = 3 * RWKV_WIDTH + DECAY_RANK + ICLR_RANK
N_ATTN_COLS = 3 * ATTN_WIDTH

LANES = 128
MXU_DIM = 256
CHUNK = 64
GROUP_HEADS = MXU_DIM // HEAD_DIM
GROUP_W = GROUP_HEADS * HEAD_DIM
IN_TILE = 512
RWKV_TILE = 512
RWKV_RING = 512
VMEM_LIMIT = 48 * 1024 * 1024

_NT = (((1,), (1,)), ((), ()))
_TN = (((0,), (0,)), ((), ()))


def _dot(a, b, dims=None, precision=None):
    if dims is None:
        return jnp.dot(a, b, preferred_element_type=F32, precision=precision)
    return lax.dot_general(a, b, dims, preferred_element_type=F32, precision=precision)


def _inproj_body(per_b, x_ref, g_ref, mu_ref, wr_ref, wa_ref, wg_ref,
                 pr_ref, q_ref, k_ref, vt_ref, km_ref, gs_ref, prev_ref):
    x = x_ref[...]
    ms = jnp.mean(x * x, axis=-1, keepdims=True)
    h = (x * lax.rsqrt(ms + RMS_EPS) * g_ref[...]).astype(BF16)
    @pl.when(pl.program_id(0) % per_b == 0)
    def _():
        prev_ref[...] = jnp.zeros_like(prev_ref)

    row0 = lax.broadcasted_iota(jnp.int32, (IN_TILE, 1), 0) == 0
    for c0 in range(0, N_RWKV_COLS, MXU_DIM):
        cs = slice(c0, min(c0 + MXU_DIM, N_RWKV_COLS))
        p = _dot(h, wr_ref[:, cs])
        p_prev = jnp.where(row0, prev_ref[:, cs], pltpu.roll(p, 1, axis=0))
        prev_ref[:, cs] = p[IN_TILE - 1:IN_TILE, :]
        pr_ref[:, cs] = p + (p_prev - p) * mu_ref[:, cs]
    qkv = _dot(h, wa_ref[...])
    q_ref[...] = qkv[:, :ATTN_WIDTH] * (HEAD_DIM ** -0.5)
    kf = qkv[:, ATTN_WIDTH:2 * ATTN_WIDTH]
    k_ref[...] = kf.astype(BF16)
    for j in range(IN_TILE // MOBA_BLOCK):
        km_ref[j] = jnp.mean(kf[j * MOBA_BLOCK:(j + 1) * MOBA_BLOCK], axis=0, keepdims=True)
    vt_ref[0] = qkv[:, 2 * ATTN_WIDTH:].T.astype(BF16)
    g = _dot(h, wg_ref[...])
    gs_ref[...] = (g * jax.nn.sigmoid(g)).astype(BF16)


def _in_proj(x2, g_pre, mu, w_r, w_a, w_g, batch, seq):
    m, d = x2.shape
    nblk = m // IN_TILE
    per_b = seq // IN_TILE
    blocks = IN_TILE // MOBA_BLOCK
    const = lambda i: (0, 0)
    return pl.pallas_call(
        functools.partial(_inproj_body, per_b),
        grid=(nblk,),
        in_specs=[
            pl.BlockSpec((IN_TILE, d), lambda i: (i, 0)),
            pl.BlockSpec((1, d), const),
            pl.BlockSpec((1, N_RWKV_COLS), const),
            pl.BlockSpec(w_r.shape, const),
            pl.BlockSpec(w_a.shape, const),
            pl.BlockSpec(w_g.shape, const),
        ],
        out_specs=[
            pl.BlockSpec((IN_TILE, N_RWKV_COLS), lambda i: (i, 0)),
            pl.BlockSpec((IN_TILE, ATTN_WIDTH), lambda i: (i, 0)),
            pl.BlockSpec((IN_TILE, ATTN_WIDTH), lambda i: (i, 0)),
            pl.BlockSpec((1, ATTN_WIDTH, IN_TILE), lambda i: (i // per_b, 0, i % per_b)),
            pl.BlockSpec((blocks, 1, ATTN_WIDTH), lambda i: (i, 0, 0)),
            pl.BlockSpec((IN_TILE, D_MIX), lambda i: (i, 0)),
        ],
        out_shape=[
            jax.ShapeDtypeStruct((m, N_RWKV_COLS), F32),
            jax.ShapeDtypeStruct((m, ATTN_WIDTH), F32),
            jax.ShapeDtypeStruct((m, ATTN_WIDTH), BF16),
            jax.ShapeDtypeStruct((batch, ATTN_WIDTH, seq), BF16),
            jax.ShapeDtypeStruct((m // MOBA_BLOCK, 1, ATTN_WIDTH), F32),
            jax.ShapeDtypeStruct((m, D_MIX), BF16),
        ],
        scratch_shapes=[pltpu.VMEM((1, N_RWKV_COLS), F32)],
        compiler_params=pltpu.CompilerParams(
            dimension_semantics=("arbitrary",), vmem_limit_bytes=VMEM_LIMIT),
        name="in_proj",
    )(x2, g_pre, mu, w_r, w_a, w_g)


LOG2E = 1.4426950408889634
AUX_S, AUX_T, AUX_BLK, AUX_SEL = 0, 3, 6, 16
PV_ONES_ROWS = 16
MOBA_STEP_HEADS = 4
MOBA_STEP_BLOCKS = 4
MOBA_TRIP_WIDTHS = (4, 2, 1)


def _moba_key_aux(seq):
    pos = jnp.arange(seq, dtype=jnp.int32)[:, None]
    lane = jnp.arange(LANES, dtype=jnp.int32)[None, :]
    blk = pos // MOBA_BLOCK
    tab = jnp.where(lane < AUX_T, pos % MOBA_BLOCK,
                    jnp.where(lane < AUX_BLK, 1,
                              jnp.where(lane < AUX_BLK + 3, blk * MOBA_BLOCK,
                                        jnp.where(lane == AUX_SEL + blk, 1, 0))))
    return tab.astype(BF16)


def _split2(x):
    hi = x.astype(BF16)
    lo = (x - hi.astype(F32)).astype(BF16)
    return hi, lo


def _split3(x):
    x1 = x.astype(BF16).astype(F32)
    r1 = x - x1
    x2 = r1.astype(BF16).astype(F32)
    x3 = (r1 - x2).astype(BF16).astype(F32)
    return x1, x2, x3


def _moba_body(slopes_ref, q_ref, k_ref, ak_ref, vt_ref, km_ref, gs_ref, o_ref, st_ref, p_ref):
    BS = MOBA_BLOCK
    grp = pl.program_id(1)
    step = pl.program_id(2)
    heads = MOBA_STEP_HEADS
    per_pair = LANES // HEAD_DIM
    nb = km_ref.shape[1]
    n_iota = lax.broadcasted_iota(jnp.int32, (nb, BS), 0)
    causal = (lax.broadcasted_iota(jnp.int32, (BS, BS), 0)
              <= lax.broadcasted_iota(jnp.int32, (BS, BS), 1))
    piece = n_iota % 3
    ones_rows = jnp.ones((PV_ONES_ROWS, BS), BF16)
    lane_head = lax.broadcasted_iota(jnp.int32, (1, heads * HEAD_DIM), 1) // HEAD_DIM
    row_h = lax.broadcasted_iota(jnp.int32, (LANES, 1), 0) // HEAD_DIM
    zero_b = jnp.zeros((), BF16)
    aux_pad = jnp.zeros((LANES - AUX_SEL - nb, BS), BF16)
    no_mask = jnp.zeros((nb, BS), BF16)
    km = km_ref[0]
    km_stack = jnp.concatenate([jnp.where(lane_head == h, km, 0.0) for h in range(heads)], axis=0)
    km_hi, km_lo = _split2(km_stack)
    km_cat = jnp.concatenate([km_hi, km_lo, km_hi], axis=1)

    def pair_lanes(h):
        p = h // per_pair
        return slice(p * LANES, (p + 1) * LANES)

    def query_block(u):
        i = MOBA_STEP_BLOCKS * step + u
        q_rows = slice(u * BS, (u + 1) * BS)

        def key_rows(n):
            b = jnp.where(n == 0, i, jnp.minimum(n - 1, i))
            return pl.ds(pl.multiple_of(b * BS, BS), BS)

        def scores(h, blk, q_aug_t):
            k_aug = jnp.concatenate([k_ref[blk, pair_lanes(h)], ak_ref[blk, :]], axis=1)
            return _dot(k_aug, q_aug_t)

        def pv(h, slot, blk):
            vt_aug = jnp.concatenate([vt_ref[0, h * HEAD_DIM:(h + 1) * HEAD_DIM, blk], ones_rows], axis=0)
            return _dot(vt_aug, p_ref[u, slot, h])

        def prologue():
            q_t = (q_ref[q_rows, :] * LOG2E).T
            q_hi, q_lo = _split2(q_t)
            gates = _dot(km_cat, jnp.concatenate([q_hi, q_hi, q_lo], axis=0))
            gate = [jnp.where(n_iota < i, gates[h * nb:(h + 1) * nb], NEG) for h in range(heads)]
            t_glob = (lax.broadcasted_iota(jnp.int32, (nb, BS), 1) + i * BS).astype(F32)
            qh_t, aux_top = [], []
            for h in range(heads):
                cvec = jnp.full((nb, BS), slopes_ref[grp * heads + h] * LOG2E, F32)
                c1, c2, c3 = _split3(cvec)
                e1, e2, e3 = _split3(-cvec * t_glob)
                cp = jnp.where(piece == 0, c1, jnp.where(piece == 1, c2, c3))
                ep = jnp.where(piece == 0, e1, jnp.where(piece == 1, e2, e3))
                top = jnp.where((n_iota >= AUX_T) & (n_iota < AUX_BLK), ep, cp)
                aux_top.append(jnp.where(n_iota < AUX_BLK + 3, top, 0.0).astype(BF16))
                qh_t.append(jnp.where(row_h == h % per_pair, q_hi[pair_lanes(h), :], zero_b))

            def q_aug(h, sel_rows):
                return jnp.concatenate([qh_t[h], aux_top[h], sel_rows, aux_pad], axis=0)

            own = [jnp.where(causal, scores(h, key_rows(0), q_aug(h, no_mask)), NEG) for h in range(heads)]
            sel = [jnp.zeros((nb, BS), F32) for _ in range(heads)]
            for rnk in range(MOBA_TOPK):
                rank_ok = jnp.where(rnk < i, 1.0, 0.0)
                for h in range(heads):
                    mx = jnp.max(gate[h], axis=0, keepdims=True)
                    idx = jnp.min(jnp.where(gate[h] == mx, n_iota, nb), axis=0, keepdims=True)
                    pick = n_iota == idx
                    sel[h] = jnp.where(pick, rank_ok, sel[h])
                    gate[h] = jnp.where(pick, -jnp.inf, gate[h])
            q_augs = [q_aug(h, jnp.where(sel[h] > 0.0, 0.0, NEG).astype(BF16)) for h in range(heads)]
            m_own = []
            for h in range(heads):
                m_own.append(jnp.max(own[h], axis=0, keepdims=True))
                p_ref[u, 0, h] = jnp.exp2(own[h] - m_own[h]).astype(BF16)
                st_ref[u, 0, h] = scores(h, key_rows(1), q_augs[h])
            return q_augs, tuple(m_own)

        def loops(q_augs, m_own):
            def stage(n, carry, src, dst):
                ms, accs, alphas = carry
                prev_rows = key_rows(n - 1)
                next_rows = key_rows(n + 1)
                pvs = [pv(h, src, prev_rows) for h in range(heads)]
                ms2, alphas2 = [], []
                for h in range(heads):
                    st = st_ref[u, src, h]
                    m_new = jnp.maximum(ms[h], jnp.max(st, axis=0, keepdims=True))
                    alphas2.append(jnp.exp2(ms[h] - m_new))
                    p_ref[u, dst, h] = jnp.exp2(st - m_new).astype(BF16)
                    ms2.append(m_new)
                for h in range(heads):
                    st_ref[u, dst, h] = scores(h, next_rows, q_augs[h])
                accs2 = [alphas[h] * accs[h] + pvs[h] for h in range(heads)]
                return tuple(ms2), tuple(accs2), tuple(alphas2)

            def run_blocks(first, count, carry):
                for j in range(count):
                    last_odd = j == count - 1 and count % 2 == 1
                    carry = stage(first + j, carry, j % 2, 0 if last_odd else 1 - j % 2)
                return carry

            acc0 = tuple(jnp.zeros((HEAD_DIM + PV_ONES_ROWS, BS), F32) for _ in range(heads))
            alpha0 = tuple(jnp.ones((1, BS), F32) for _ in range(heads))
            n_blocks = i + 1
            done = 1
            carry = (m_own, acc0, alpha0)
            for width in MOBA_TRIP_WIDTHS:
                left = (n_blocks - done) // width
                trips = left if width == MOBA_TRIP_WIDTHS[0] else left % 2
                base = done
                carry = lax.fori_loop(
                    0, trips, lambda t, c, w=width, b=base: run_blocks(b + w * t, w, c), carry)
                done = done + trips * width
            return carry

        def epilogue(carry):
            ms, accs, alphas = carry
            last_rows = key_rows(i)
            outs = []
            for h in range(heads):
                acc = alphas[h] * accs[h] + pv(h, 0, last_rows)
                outs.append(acc[:HEAD_DIM] * (1.0 / acc[HEAD_DIM:HEAD_DIM + 1]))
            ot = jnp.concatenate(outs, axis=0)
            o_ref[q_rows, :] = (ot.T * gs_ref[q_rows, :].astype(F32)).astype(BF16)

        return prologue, loops, epilogue

    parts = [query_block(u) for u in range(MOBA_STEP_BLOCKS)]
    carry = None
    for u, (prologue, loops, epilogue) in enumerate(parts):
        q_augs, m_own = prologue()
        if u > 0:
            parts[u - 1][2](carry)
        carry = loops(q_augs, m_own)
    parts[-1][2](carry)


def _moba(slopes, q, k, key_aux, vt, kmean, gs, batch, seq):
    m = q.shape[0]
    BS = MOBA_BLOCK
    nb = seq // BS
    assert nb == AUX_SEL and AUX_SEL + nb <= LANES and AUX_BLK + 3 <= AUX_SEL and nb % MOBA_STEP_BLOCKS == 0
    H, U = MOBA_STEP_HEADS, MOBA_STEP_BLOCKS
    W = H * HEAD_DIM
    n_groups = ATTN_WIDTH // W
    n_steps = nb // U
    gate_col0 = RWKV_WIDTH // W
    return pl.pallas_call(
        _moba_body,
        grid_spec=pltpu.PrefetchScalarGridSpec(
            num_scalar_prefetch=1,
            grid=(batch, n_groups, n_steps),
            in_specs=[
                pl.BlockSpec((U * BS, W), lambda b, g, i, s: (b * n_steps + i, g)),
                pl.BlockSpec((seq, W), lambda b, g, i, s: (b, g)),
                pl.BlockSpec((seq, LANES), lambda b, g, i, s: (0, 0)),
                pl.BlockSpec((1, W, seq), lambda b, g, i, s: (b, g, 0)),
                pl.BlockSpec((1, nb, W), lambda b, g, i, s: (b, 0, g)),
                pl.BlockSpec((U * BS, W), lambda b, g, i, s: (b * n_steps + i, gate_col0 + g)),
            ],
            out_specs=pl.BlockSpec((U * BS, W), lambda b, g, i, s: (b * n_steps + i, g)),
            scratch_shapes=[
                pltpu.VMEM((U, 2, H, BS, BS), F32),
                pltpu.VMEM((U, 2, H, BS, BS), BF16),
            ],
        ),
        out_shape=jax.ShapeDtypeStruct((m, ATTN_WIDTH), BF16),
        compiler_params=pltpu.CompilerParams(
            dimension_semantics=("arbitrary", "arbitrary", "arbitrary"), vmem_limit_bytes=VMEM_LIMIT),
        name="moba",
    )(slopes, q, k, key_aux, vt, kmean, gs)


def _split_dot(x, ones_b):
    hi, lo = _split2(x)
    return _dot(hi, ones_b) + _dot(lo, ones_b)


def _head_sum(x, ones_b, split=True):
    one = _split_dot if split else (lambda xs, o: _dot(xs.astype(BF16), o))
    parts = [one(x[:, g * GROUP_W:(g + 1) * GROUP_W], ones_b) for g in range(x.shape[1] // GROUP_W)]
    return jnp.concatenate(parts, axis=1)


def _rwkv_body(pr_ref, gs_ref, w0_ref, w2a_hi_ref, w2a_lo_ref, a0_ref, kk_ref, ka_ref, rk_ref, lw_ref,
               lb_ref, x_ref, ya_ref, wor_ref, woa_ref, gp_ref,
               o_ref, s_ref, at_ref, bt_ref, kt_ref, rt_ref, v_ref, gl_ref, y_ref, bonus_ref,
               q_ref, g_ref, h_ref, yr_ref):
    W = RWKV_WIDTH
    n_groups = W // GROUP_W
    n_chunks = RWKV_TILE // CHUNK

    @pl.when(pl.program_id(1) == 0)
    def _():
        s_ref[...] = jnp.zeros_like(s_ref)

    rr = lax.broadcasted_iota(jnp.int32, (GROUP_W, GROUP_W), 0)
    cc = lax.broadcasted_iota(jnp.int32, (GROUP_W, GROUP_W), 1)
    same_head = (rr // HEAD_DIM) == (cc // HEAD_DIM)
    ones_b = same_head.astype(BF16)
    pr_ = lax.broadcasted_iota(jnp.int32, (2 * CHUNK, 4 * CHUNK), 0)
    pc_ = lax.broadcasted_iota(jnp.int32, (2 * CHUNK, 4 * CHUNK), 1)
    tril_pair = ((pr_ // CHUNK == pc_ // (2 * CHUNK)) & (pc_ % CHUNK <= pr_ % CHUNK)).astype(BF16)
    wr = lax.broadcasted_iota(jnp.int32, (CHUNK, GROUP_W), 0)
    wc = lax.broadcasted_iota(jnp.int32, (CHUNK, GROUP_W), 1) % CHUNK
    strict_w = wc < wr
    incl_w = wc <= wr
    eye_w = (wc == wr).astype(F32)
    lane_g = lax.broadcasted_iota(jnp.int32, (1, GROUP_W), 1) // HEAD_DIM
    head_masks = [lane_g == h for h in range(GROUP_HEADS)]

    def mstack(x):
        xb = x.astype(BF16)
        zero = jnp.zeros_like(xb)
        return jnp.concatenate([jnp.where(hm, xb, zero) for hm in head_masks], axis=0)

    def head_rows(xb):
        return jnp.concatenate([xb[:, h * HEAD_DIM:(h + 1) * HEAD_DIM] for h in range(GROUP_HEADS)], axis=0)

    def block_diag(xw):
        return jnp.where(same_head, jnp.concatenate([xw] * GROUP_HEADS, axis=0), jnp.zeros((), BF16))

    HALF = 2 * CHUNK
    SUB_CHUNKS = 4
    RING_CHUNKS = RWKV_RING // CHUNK

    def ring_rows(c, n=1):
        r0 = (c % RING_CHUNKS) * CHUNK
        return slice(r0, r0 + n * CHUNK)

    def ring_slot(c, g):
        return (c % RING_CHUNKS) * n_groups + g

    lane_z = lax.broadcasted_iota(jnp.int32, (1, DECAY_RANK + ICLR_RANK), 1)

    def run(main, side=None, side_start=0):
        gens = [main] + ([side] if side is not None else [])
        stage_no = 0
        while gens:
            for gen in list(gens):
                if gen is side and stage_no < side_start and main in gens:
                    continue
                try:
                    next(gen)
                except StopIteration:
                    gens.remove(gen)
            stage_no += 1

    def seq(*gens):
        for gen in gens:
            yield from gen

    def prep(half):
        hr = slice(half * HALF, (half + 1) * HALF)
        rr_ = ring_rows(2 * half, 2)
        z = pr_ref[hr, 3 * W:3 * W + DECAY_RANK + ICLR_RANK]
        z = jnp.where(lane_z < DECAY_RANK, jnp.tanh(z), z)
        z_hi, z_lo = _split2(z)
        lin = _dot(jnp.concatenate([z_hi, z_lo], axis=1), w2a_hi_ref[...]) + _dot(z_hi, w2a_lo_ref[...])
        k = pr_ref[hr, W:2 * W]
        kkr = k * kk_ref[...]
        ss = _head_sum(kkr * kkr, ones_b, split=False)
        yield
        ld = -EXP_M_HALF * jax.nn.sigmoid(w0_ref[...] + lin[:, :W])
        ld_hi, ld_lo = _split2(ld)
        cs = _dot(tril_pair, jnp.concatenate([ld_hi[:CHUNK], ld_lo[:CHUNK], ld_hi[CHUNK:], ld_lo[CHUNK:]], axis=0))
        a = jax.nn.sigmoid(a0_ref[...] + lin[:, W:])
        k2 = k * (1.0 + (a - 1.0) * ka_ref[...])
        r = pr_ref[hr, 0:W]
        bsum = _head_sum(r * k2 * rk_ref[...], ones_b)
        yield
        kk = kkr * jnp.minimum(lax.rsqrt(ss), 1e12)
        g_inv = jnp.exp(-cs)
        at_ref[rr_, :] = (-kk) * jnp.exp(cs - ld)
        bt_ref[rr_, :] = kk * a * g_inv
        kt_ref[rr_, :] = k2 * g_inv
        yield
        g_incl = jnp.exp(cs)
        v = pr_ref[hr, 2 * W:3 * W]
        rt_ref[rr_, :] = r * g_incl
        v_ref[rr_, :] = v
        gl_ref[rr_, :] = g_incl
        bonus_ref[hr, :] = bsum * v
        yield

    def chain_factors(batch_chains):
        nc = range(len(batch_chains))
        rows = [ring_rows(c) for c, _ in batch_chains]
        yrows = [slice(c * CHUNK, (c + 1) * CHUNK) for c, _ in batch_chains]
        cols = [slice(g * GROUP_W, (g + 1) * GROUP_W) for _, g in batch_chains]
        x = []
        for j in nc:
            lhs = jnp.concatenate([at_ref[rows[j], cols[j]], rt_ref[rows[j], cols[j]]], axis=0).astype(BF16)
            rhs = jnp.concatenate([mstack(bt_ref[rows[j], cols[j]]), mstack(kt_ref[rows[j], cols[j]])], axis=0)
            x.append(_dot(lhs, rhs, _NT))
        yield
        a_ab = [jnp.where(strict_w, x[j][:CHUNK, :GROUP_W], 0.0) for j in nc]
        a_ak = [jnp.where(strict_w, x[j][:CHUNK, GROUP_W:], 0.0).astype(BF16) for j in nc]
        m_rb = [jnp.where(incl_w, x[j][CHUNK:, :GROUP_W], 0.0).astype(BF16) for j in nc]
        m_rk = [jnp.where(incl_w, x[j][CHUNK:, GROUP_W:], 0.0).astype(BF16) for j in nc]
        tw = [a_ab[j] + eye_w for j in nc]
        pw = [a_ab[j].astype(BF16) for j in nc]
        pw = [_dot(pw[j], block_diag(pw[j])).astype(BF16) for j in nc]
        yield
        for it in range(CHUNK.bit_length() - 3):
            both = [_dot(jnp.concatenate([pw[j], tw[j].astype(BF16)], axis=0), block_diag(pw[j])) for j in nc]
            tw = [tw[j] + both[j][CHUNK:] for j in nc]
            pw = [both[j][:CHUNK].astype(BF16) for j in nc]
            yield
        tw = [tw[j] + _dot(tw[j].astype(BF16), block_diag(pw[j])) for j in nc]
        twb = [tw[j].astype(BF16) for j in nc]
        yield
        v_m = [mstack(v_ref[rows[j], cols[j]]) for j in nc]
        wt = [_dot(twb[j], mstack(at_ref[rows[j], cols[j]])) for j in nc]
        x1 = [_dot(a_ak[j], v_m[j]) for j in nc]
        yield
        u0 = [_dot(twb[j], mstack(x1[j])) for j in nc]
        for j in nc:
            q_ref[rows[j], cols[j]] = (rt_ref[rows[j], cols[j]] + _dot(m_rb[j], mstack(wt[j]))).astype(BF16)
        yield
        for j in nc:
            y_ref[yrows[j], cols[j]] = _dot(m_rb[j], mstack(u0[j])) + _dot(m_rk[j], v_m[j])
        yield
        for j in nc:
            c, g = batch_chains[j]
            gl = gl_ref[rows[j].stop - 1:rows[j].stop, cols[j]]
            bh = (bt_ref[rows[j], cols[j]] * gl).astype(BF16)
            kh = (kt_ref[rows[j], cols[j]] * gl).astype(BF16)
            bh_rows, kh_rows = head_rows(bh), head_rows(kh)
            gw = eye_w * gl + _dot(bh_rows, mstack(wt[j]), _TN)
            g_ref[ring_slot(c, g)] = gw.astype(BF16)
            h_ref[ring_slot(c, g)] = _dot(
                jnp.concatenate([bh_rows, kh_rows], axis=0),
                jnp.concatenate([mstack(u0[j]), v_m[j]], axis=0), _TN)
        yield

    def recurrence(c):
        rows = slice(c * CHUNK, (c + 1) * CHUNK)
        for g in range(n_groups):
            cols = slice(g * GROUP_W, (g + 1) * GROUP_W)
            s_b = block_diag(s_ref[g].astype(BF16))
            y_ref[rows, cols] = y_ref[rows, cols] + _dot(q_ref[ring_rows(c), cols], s_b)
            s_ref[g] = _dot(g_ref[ring_slot(c, g)], s_b) + h_ref[ring_slot(c, g)]
        yield

    def finish(half):
        hr = slice(half * HALF, (half + 1) * HALF)
        y = y_ref[hr, :]
        mean = _head_sum(y, ones_b) * (1.0 / HEAD_DIM)
        yield
        yc = y - mean
        var = _head_sum(yc * yc, ones_b) * (1.0 / HEAD_DIM)
        yield
        yn = yc * lax.rsqrt(var + GN_EPS) * lw_ref[...] + lb_ref[...] + bonus_ref[hr, :]
        yr_ref[hr, :] = (yn * gs_ref[hr, :].astype(F32)).astype(BF16)
        yield

    def project(sub):
        sr = slice(sub * SUB_CHUNKS * CHUNK, (sub + 1) * SUB_CHUNKS * CHUNK)
        o = _dot(yr_ref[sr, :], wor_ref[...]) + _dot(ya_ref[sr, :], woa_ref[...])
        yield
        ms = jnp.mean(o * o, axis=-1, keepdims=True)
        o_ref[sr, :] = x_ref[sr, :] + o * lax.rsqrt(ms + RMS_EPS) * gp_ref[...]
        yield

    n_sub = n_chunks // SUB_CHUNKS
    halves = lambda sub: (2 * sub, 2 * sub + 1)
    chunks = lambda sub: range(SUB_CHUNKS * sub, SUB_CHUNKS * (sub + 1))
    tail = lambda sub: [recurrence(c) for c in chunks(sub)] + [finish(hf) for hf in halves(sub)]
    run(seq(*[prep(hf) for hf in halves(0)]))
    for sub in range(n_sub):
        side = [prep(hf) for hf in halves(sub + 1)] if sub + 1 < n_sub else []
        side += tail(sub - 1) if sub > 0 else []
        run(chain_factors([(c, g) for c in chunks(sub) for g in range(n_groups)]), seq(*side), side_start=2)
    run(seq(*tail(n_sub - 1)), seq(*[project(sub) for sub in range(n_sub - 1)]))
    run(project(n_sub - 1))


def _rwkv_out(pr, gs, w0, w2a, a0, k_k, k_a, r_k, lnx_w, lnx_b, x2, ya, w_or, w_oa, g_post, batch, seq):
    m, d = x2.shape
    per_b = seq // RWKV_TILE
    W = RWKV_WIDTH
    n_cg = (RWKV_RING // CHUNK) * (W // GROUP_W)
    w2a_hi = w2a.astype(BF16)
    w2a_lo = (w2a - w2a_hi.astype(F32)).astype(BF16)
    w2a_hi = jnp.concatenate([w2a_hi, w2a_hi], axis=0)
    tile = lambda b, t: (b * per_b + t, 0)
    const = lambda b, t: (0, 0)
    vec = pl.BlockSpec((1, W), const)
    ring_scratch = pltpu.VMEM((RWKV_RING, W), F32)
    tile_scratch = pltpu.VMEM((RWKV_TILE, W), F32)
    return pl.pallas_call(
        _rwkv_body,
        grid=(batch, per_b),
        in_specs=[
            pl.BlockSpec((RWKV_TILE, N_RWKV_COLS), tile),
            pl.BlockSpec((RWKV_TILE, W), tile),
            vec,
            pl.BlockSpec(w2a_hi.shape, const),
            pl.BlockSpec(w2a_lo.shape, const),
            vec, vec, vec, vec, vec, vec,
            pl.BlockSpec((RWKV_TILE, d), tile),
            pl.BlockSpec((RWKV_TILE, ATTN_WIDTH), tile),
            pl.BlockSpec(w_or.shape, const),
            pl.BlockSpec(w_oa.shape, const),
            pl.BlockSpec((1, d), const),
        ],
        out_specs=pl.BlockSpec((RWKV_TILE, d), tile),
        out_shape=jax.ShapeDtypeStruct((m, d), F32),
        scratch_shapes=[
            pltpu.VMEM((W // GROUP_W, HEAD_DIM, GROUP_W), F32),
            ring_scratch, ring_scratch, ring_scratch, ring_scratch, ring_scratch, ring_scratch,
            tile_scratch, tile_scratch,
            pltpu.VMEM((RWKV_RING, W), BF16),
            pltpu.VMEM((n_cg, HEAD_DIM, GROUP_W), BF16),
            pltpu.VMEM((n_cg, HEAD_DIM, GROUP_W), F32),
            pltpu.VMEM((RWKV_TILE, W), BF16),
        ],
        compiler_params=pltpu.CompilerParams(
            dimension_semantics=("arbitrary", "arbitrary"), vmem_limit_bytes=VMEM_LIMIT),
        name="rwkv7_out",
    )(pr, gs, w0, w2a_hi, w2a_lo, a0, k_k, k_a, r_k, lnx_w, lnx_b, x2, ya, w_or, w_oa, g_post)


def kernel(x, g_pre, w_in, tshift_mu, w0, w2, a0, a2, k_k, k_a, r_k, lnx_w, lnx_b, w_out, g_post):
    batch, seq, d = x.shape
    depth = w_in.shape[0]
    assert seq % RWKV_TILE == 0 and seq % IN_TILE == 0 and IN_TILE % MOBA_BLOCK == 0 and d % LANES == 0
    x2 = x.reshape(batch * seq, d)
    slopes = 2.0 ** (-8.0 * jnp.arange(1, ATTN_HEADS + 1, dtype=F32) / ATTN_HEADS)
    row = lambda z: z.reshape(1, -1).astype(F32)
    for l in range(depth):
        wb = w_in[l].astype(BF16)
        w_r = wb[:, :N_RWKV_COLS]
        w_a = wb[:, N_RWKV_COLS:N_RWKV_COLS + N_ATTN_COLS]
        w_g = wb[:, N_RWKV_COLS + N_ATTN_COLS:]
        pr, q, k, vt, kmean, gs = _in_proj(x2, row(g_pre[l]), row(tshift_mu[l]), w_r, w_a, w_g, batch, seq)
        ya = _moba(slopes, q, k, _moba_key_aux(seq), vt,
                   kmean.reshape(batch, seq // MOBA_BLOCK, ATTN_WIDTH), gs, batch, seq)
        zeros = jnp.zeros((DECAY_RANK, RWKV_WIDTH), F32)
        w2a = jnp.concatenate([jnp.concatenate([w2[l], zeros], axis=1),
                               jnp.concatenate([zeros, a2[l]], axis=1)], axis=0)
        wo = w_out[l].astype(BF16)
        x2 = _rwkv_out(pr, gs, row(w0[l]), w2a, row(a0[l]), row(k_k[l]), row(k_a[l]), row(r_k[l]),
                       row(lnx_w[l]), row(lnx_b[l]), x2, ya, wo[:RWKV_WIDTH], wo[RWKV_WIDTH:],
                       row(g_post[l]), batch, seq)
    return x2.reshape(batch, seq, d)
```

```python
import functools

import jax
import jax.numpy as jnp
from jax import lax
from jax.experimental import pallas as pl
from jax.experimental.pallas import tpu as pltpu

F32 = jnp.float32
BF16 = jnp.bfloat16

HEAD_DIM = 64
RWKV_HEADS = 8
ATTN_HEADS = 8
RWKV_WIDTH = RWKV_HEADS * HEAD_DIM
ATTN_WIDTH = ATTN_HEADS * HEAD_DIM
D_MIX = RWKV_WIDTH + ATTN_WIDTH
DECAY_RANK = 64
ICLR_RANK = 64
MOBA_BLOCK = 256
MOBA_TOPK = 3
RMS_EPS = 1e-6
GN_EPS = 64e-5
EXP_M_HALF = 0.6065306597126334
NEG = -1e30
N_RWKV_COLS = 3 * RWKV_WIDTH + DECAY_RANK + ICLR_RANK
N_ATTN_COLS = 3 * ATTN_WIDTH

LANES = 128
MXU_DIM = 256
CHUNK = 64
GROUP_HEADS = MXU_DIM // HEAD_DIM
GROUP_W = GROUP_HEADS * HEAD_DIM
IN_TILE = 512
RWKV_TILE = 512
RWKV_RING = 512
VMEM_LIMIT = 48 * 1024 * 1024

_NT = (((1,), (1,)), ((), ()))
_TN = (((0,), (0,)), ((), ()))


def _dot(a, b, dims=None, precision=None):
    if dims is None:
        return jnp.dot(a, b, preferred_element_type=F32, precision=precision)
    return lax.dot_general(a, b, dims, preferred_element_type=F32, precision=precision)


def _inproj_body(per_b, x_ref, g_ref, mu_ref, wr_ref, wa_ref, wg_ref,
                 pr_ref, q_ref, k_ref, vt_ref, km_ref, gs_ref, prev_ref):
    x = x_ref[...]
    ms = jnp.mean(x * x, axis=-1, keepdims=True)
    h = (x * lax.rsqrt(ms + RMS_EPS) * g_ref[...]).astype(BF16)
    @pl.when(pl.program_id(0) % per_b == 0)
    def _():
        prev_ref[...] = jnp.zeros_like(prev_ref)

    row0 = lax.broadcasted_iota(jnp.int32, (IN_TILE, 1), 0) == 0
    for c0 in range(0, N_RWKV_COLS, MXU_DIM):
        cs = slice(c0, min(c0 + MXU_DIM, N_RWKV_COLS))
        p = _dot(h, wr_ref[:, cs])
        p_prev = jnp.where(row0, prev_ref[:, cs], pltpu.roll(p, 1, axis=0))
        prev_ref[:, cs] = p[IN_TILE - 1:IN_TILE, :]
        pr_ref[:, cs] = p + (p_prev - p) * mu_ref[:, cs]
    qkv = _dot(h, wa_ref[...])
    q_ref[...] = qkv[:, :ATTN_WIDTH] * (HEAD_DIM ** -0.5)
    kf = qkv[:, ATTN_WIDTH:2 * ATTN_WIDTH]
    k_ref[...] = kf.astype(BF16)
    for j in range(IN_TILE // MOBA_BLOCK):
        km_ref[j] = jnp.mean(kf[j * MOBA_BLOCK:(j + 1) * MOBA_BLOCK], axis=0, keepdims=True)
    vt_ref[0] = qkv[:, 2 * ATTN_WIDTH:].T.astype(BF16)
    g = _dot(h, wg_ref[...])
    gs_ref[...] = (g * jax.nn.sigmoid(g)).astype(BF16)


def _in_proj(x2, g_pre, mu, w_r, w_a, w_g, batch, seq):
    m, d = x2.shape
    nblk = m // IN_TILE
    per_b = seq // IN_TILE
    blocks = IN_TILE // MOBA_BLOCK
    const = lambda i: (0, 0)
    return pl.pallas_call(
        functools.partial(_inproj_body, per_b),
        grid=(nblk,),
        in_specs=[
            pl.BlockSpec((IN_TILE, d), lambda i: (i, 0)),
            pl.BlockSpec((1, d), const),
            pl.BlockSpec((1, N_RWKV_COLS), const),
            pl.BlockSpec(w_r.shape, const),
            pl.BlockSpec(w_a.shape, const),
            pl.BlockSpec(w_g.shape, const),
        ],
        out_specs=[
            pl.BlockSpec((IN_TILE, N_RWKV_COLS), lambda i: (i, 0)),
            pl.BlockSpec((IN_TILE, ATTN_WIDTH), lambda i: (i, 0)),
            pl.BlockSpec((IN_TILE, ATTN_WIDTH), lambda i: (i, 0)),
            pl.BlockSpec((1, ATTN_WIDTH, IN_TILE), lambda i: (i // per_b, 0, i % per_b)),
            pl.BlockSpec((blocks, 1, ATTN_WIDTH), lambda i: (i, 0, 0)),
            pl.BlockSpec((IN_TILE, D_MIX), lambda i: (i, 0)),
        ],
        out_shape=[
            jax.ShapeDtypeStruct((m, N_RWKV_COLS), F32),
            jax.ShapeDtypeStruct((m, ATTN_WIDTH), F32),
            jax.ShapeDtypeStruct((m, ATTN_WIDTH), BF16),
            jax.ShapeDtypeStruct((batch, ATTN_WIDTH, seq), BF16),
            jax.ShapeDtypeStruct((m // MOBA_BLOCK, 1, ATTN_WIDTH), F32),
            jax.ShapeDtypeStruct((m, D_MIX), BF16),
        ],
        scratch_shapes=[pltpu.VMEM((1, N_RWKV_COLS), F32)],
        compiler_params=pltpu.CompilerParams(
            dimension_semantics=("arbitrary",), vmem_limit_bytes=VMEM_LIMIT),
        name="in_proj",
    )(x2, g_pre, mu, w_r, w_a, w_g)


LOG2E = 1.4426950408889634
AUX_S, AUX_T, AUX_BLK, AUX_SEL = 0, 3, 6, 16
PV_ONES_ROWS = 16
MOBA_STEP_HEADS = 4
MOBA_STEP_BLOCKS = 8
MOBA_TRIP_WIDTHS = (8, 4, 2, 1)


def _moba_key_aux(seq):
    pos = jnp.arange(seq, dtype=jnp.int32)[:, None]
    lane = jnp.arange(LANES, dtype=jnp.int32)[None, :]
    blk = pos // MOBA_BLOCK
    tab = jnp.where(lane < AUX_T, pos % MOBA_BLOCK,
                    jnp.where(lane < AUX_BLK, 1,
                              jnp.where(lane < AUX_BLK + 3, blk * MOBA_BLOCK,
                                        jnp.where(lane == AUX_SEL + blk, 1, 0))))
    return tab.astype(BF16)


def _split2(x):
    hi = x.astype(BF16)
    lo = (x - hi.astype(F32)).astype(BF16)
    return hi, lo


def _split3(x):
    x1 = x.astype(BF16).astype(F32)
    r1 = x - x1
    x2 = r1.astype(BF16).astype(F32)
    x3 = (r1 - x2).astype(BF16).astype(F32)
    return x1, x2, x3


def _moba_body(slopes_ref, q_ref, k_ref, ak_ref, vt_ref, km_ref, gs_ref, o_ref, st_ref, p_ref):
    BS = MOBA_BLOCK
    grp = pl.program_id(1)
    step = pl.program_id(2)
    heads = MOBA_STEP_HEADS
    per_pair = LANES // HEAD_DIM
    nb = km_ref.shape[1]
    n_iota = lax.broadcasted_iota(jnp.int32, (nb, BS), 0)
    causal = (lax.broadcasted_iota(jnp.int32, (BS, BS), 0)
              <= lax.broadcasted_iota(jnp.int32, (BS, BS), 1))
    piece = n_iota % 3
    ones_rows = jnp.ones((PV_ONES_ROWS, BS), BF16)
    lane_head = lax.broadcasted_iota(jnp.int32, (1, heads * HEAD_DIM), 1) // HEAD_DIM
    row_h = lax.broadcasted_iota(jnp.int32, (LANES, 1), 0) // HEAD_DIM
    zero_b = jnp.zeros((), BF16)
    aux_pad = jnp.zeros((LANES - AUX_SEL - nb, BS), BF16)
    no_mask = jnp.zeros((nb, BS), BF16)
    km = km_ref[0]
    km_stack = jnp.concatenate([jnp.where(lane_head == h, km, 0.0) for h in range(heads)], axis=0)
    km_hi, km_lo = _split2(km_stack)
    km_cat = jnp.concatenate([km_hi, km_lo, km_hi], axis=1)

    def pair_lanes(h):
        p = h // per_pair
        return slice(p * LANES, (p + 1) * LANES)

    def query_block(u):
        i = MOBA_STEP_BLOCKS * step + u
        q_rows = slice(u * BS, (u + 1) * BS)

        def key_rows(n):
            b = jnp.where(n == 0, i, jnp.minimum(n - 1, i))
            return pl.ds(pl.multiple_of(b * BS, BS), BS)

        def scores(h, blk, q_aug_t):
            k_aug = jnp.concatenate([k_ref[blk, pair_lanes(h)], ak_ref[blk, :]], axis=1)
            return _dot(k_aug, q_aug_t)

        def pv(h, slot, blk):
            vt_aug = jnp.concatenate([vt_ref[0, h * HEAD_DIM:(h + 1) * HEAD_DIM, blk], ones_rows], axis=0)
            return _dot(vt_aug, p_ref[u, slot, h])

        def prologue():
            q_t = (q_ref[q_rows, :] * LOG2E).T
            q_hi, q_lo = _split2(q_t)
            gates = _dot(km_cat, jnp.concatenate([q_hi, q_hi, q_lo], axis=0))
            gate = [jnp.where(n_iota < i, gates[h * nb:(h + 1) * nb], NEG) for h in range(heads)]
            t_glob = (lax.broadcasted_iota(jnp.int32, (nb, BS), 1) + i * BS).astype(F32)
            qh_t, aux_top = [], []
            for h in range(heads):
                cvec = jnp.full((nb, BS), slopes_ref[grp * heads + h] * LOG2E, F32)
                c1, c2, c3 = _split3(cvec)
                e1, e2, e3 = _split3(-cvec * t_glob)
                cp = jnp.where(piece == 0, c1, jnp.where(piece == 1, c2, c3))
                ep = jnp.where(piece == 0, e1, jnp.where(piece == 1, e2, e3))
                top = jnp.where((n_iota >= AUX_T) & (n_iota < AUX_BLK), ep, cp)
                aux_top.append(jnp.where(n_iota < AUX_BLK + 3, top, 0.0).astype(BF16))
                qh_t.append(jnp.where(row_h == h % per_pair, q_hi[pair_lanes(h), :], zero_b))

            def q_aug(h, sel_rows):
                return jnp.concatenate([qh_t[h], aux_top[h], sel_rows, aux_pad], axis=0)

            own = [jnp.where(causal, scores(h, key_rows(0), q_aug(h, no_mask)), NEG) for h in range(heads)]
            sel = [jnp.zeros((nb, BS), F32) for _ in range(heads)]
            for rnk in range(MOBA_TOPK):
                rank_ok = jnp.where(rnk < i, 1.0, 0.0)
                for h in range(heads):
                    mx = jnp.max(gate[h], axis=0, keepdims=True)
                    idx = jnp.min(jnp.where(gate[h] == mx, n_iota, nb), axis=0, keepdims=True)
                    pick = n_iota == idx
                    sel[h] = jnp.where(pick, rank_ok, sel[h])
                    gate[h] = jnp.where(pick, -jnp.inf, gate[h])
            q_augs = [q_aug(h, jnp.where(sel[h] > 0.0, 0.0, NEG).astype(BF16)) for h in range(heads)]
            m_own = []
            for h in range(heads):
                m_own.append(jnp.max(own[h], axis=0, keepdims=True))
                p_ref[u, 0, h] = jnp.exp2(own[h] - m_own[h]).astype(BF16)
                st_ref[u, 0, h] = scores(h, key_rows(1), q_augs[h])
            return q_augs, tuple(m_own)

        def loops(q_augs, m_own):
            def stage(n, carry, src, dst):
                ms, accs, alphas = carry
                prev_rows = key_rows(n - 1)
                next_rows = key_rows(n + 1)
                pvs = [pv(h, src, prev_rows) for h in range(heads)]
                ms2, alphas2 = [], []
                for h in range(heads):
                    st = st_ref[u, src, h]
                    m_new = jnp.maximum(ms[h], jnp.max(st, axis=0, keepdims=True))
                    alphas2.append(jnp.exp2(ms[h] - m_new))
                    p_ref[u, dst, h] = jnp.exp2(st - m_new).astype(BF16)
                    ms2.append(m_new)
                for h in range(heads):
                    st_ref[u, dst, h] = scores(h, next_rows, q_augs[h])
                accs2 = [alphas[h] * accs[h] + pvs[h] for h in range(heads)]
                return tuple(ms2), tuple(accs2), tuple(alphas2)

            def run_blocks(first, count, carry):
                for j in range(count):
                    last_odd = j == count - 1 and count % 2 == 1
                    carry = stage(first + j, carry, j % 2, 0 if last_odd else 1 - j % 2)
                return carry

            acc0 = tuple(jnp.zeros((HEAD_DIM + PV_ONES_ROWS, BS), F32) for _ in range(heads))
            alpha0 = tuple(jnp.ones((1, BS), F32) for _ in range(heads))
            n_blocks = i + 1
            done = 1
            carry = (m_own, acc0, alpha0)
            for width in MOBA_TRIP_WIDTHS:
                left = (n_blocks - done) // width
                trips = left if width == MOBA_TRIP_WIDTHS[0] else left % 2
                base = done
                carry = lax.fori_loop(
                    0, trips, lambda t, c, w=width, b=base: run_blocks(b + w * t, w, c), carry)
                done = done + trips * width
            return carry

        def epilogue(carry):
            ms, accs, alphas = carry
            last_rows = key_rows(i)
            outs = []
            for h in range(heads):
                acc = alphas[h] * accs[h] + pv(h, 0, last_rows)
                outs.append(acc[:HEAD_DIM] * (1.0 / acc[HEAD_DIM:HEAD_DIM + 1]))
            ot = jnp.concatenate(outs, axis=0)
            o_ref[q_rows, :] = (ot.T * gs_ref[q_rows, :].astype(F32)).astype(BF16)

        return prologue, loops, epilogue

    parts = [query_block(u) for u in range(MOBA_STEP_BLOCKS)]
    carry = None
    for u, (prologue, loops, epilogue) in enumerate(parts):
        q_augs, m_own = prologue()
        if u > 0:
            parts[u - 1][2](carry)
        carry = loops(q_augs, m_own)
    parts[-1][2](carry)


def _moba(slopes, q, k, key_aux, vt, kmean, gs, batch, seq):
    m = q.shape[0]
    BS = MOBA_BLOCK
    nb = seq // BS
    assert nb == AUX_SEL and AUX_SEL + nb <= LANES and AUX_BLK + 3 <= AUX_SEL and nb % MOBA_STEP_BLOCKS == 0
    H, U = MOBA_STEP_HEADS, MOBA_STEP_BLOCKS
    W = H * HEAD_DIM
    n_groups = ATTN_WIDTH // W
    n_steps = nb // U
    gate_col0 = RWKV_WIDTH // W
    return pl.pallas_call(
        _moba_body,
        grid_spec=pltpu.PrefetchScalarGridSpec(
            num_scalar_prefetch=1,
            grid=(batch, n_groups, n_steps),
            in_specs=[
                pl.BlockSpec((U * BS, W), lambda b, g, i, s: (b * n_steps + i, g)),
                pl.BlockSpec((seq, W), lambda b, g, i, s: (b, g)),
                pl.BlockSpec((seq, LANES), lambda b, g, i, s: (0, 0)),
                pl.BlockSpec((1, W, seq), lambda b, g, i, s: (b, g, 0)),
                pl.BlockSpec((1, nb, W), lambda b, g, i, s: (b, 0, g)),
                pl.BlockSpec((U * BS, W), lambda b, g, i, s: (b * n_steps + i, gate_col0 + g)),
            ],
            out_specs=pl.BlockSpec((U * BS, W), lambda b, g, i, s: (b * n_steps + i, g)),
            scratch_shapes=[
                pltpu.VMEM((U, 2, H, BS, BS), F32),
                pltpu.VMEM((U, 2, H, BS, BS), BF16),
            ],
        ),
        out_shape=jax.ShapeDtypeStruct((m, ATTN_WIDTH), BF16),
        compiler_params=pltpu.CompilerParams(
            dimension_semantics=("arbitrary", "arbitrary", "arbitrary"), vmem_limit_bytes=VMEM_LIMIT),
        name="moba",
    )(slopes, q, k, key_aux, vt, kmean, gs)


def _split_dot(x, ones_b):
    hi, lo = _split2(x)
    return _dot(hi, ones_b) + _dot(lo, ones_b)


def _head_sum(x, ones_b, split=True):
    one = _split_dot if split else (lambda xs, o: _dot(xs.astype(BF16), o))
    parts = [one(x[:, g * GROUP_W:(g + 1) * GROUP_W], ones_b) for g in range(x.shape[1] // GROUP_W)]
    return jnp.concatenate(parts, axis=1)


def _rwkv_body(pr_ref, gs_ref, w0_ref, w2a_hi_ref, w2a_lo_ref, a0_ref, kk_ref, ka_ref, rk_ref, lw_ref,
               lb_ref, x_ref, ya_ref, wor_ref, woa_ref, gp_ref,
               o_ref, s_ref, at_ref, bt_ref, kt_ref, rt_ref, v_ref, gl_ref, y_ref, bonus_ref,
               q_ref, g_ref, h_ref, yr_ref):
    W = RWKV_WIDTH
    n_groups = W // GROUP_W
    n_chunks = RWKV_TILE // CHUNK

    @pl.when(pl.program_id(1) == 0)
    def _():
        s_ref[...] = jnp.zeros_like(s_ref)

    rr = lax.broadcasted_iota(jnp.int32, (GROUP_W, GROUP_W), 0)
    cc = lax.broadcasted_iota(jnp.int32, (GROUP_W, GROUP_W), 1)
    same_head = (rr // HEAD_DIM) == (cc // HEAD_DIM)
    ones_b = same_head.astype(BF16)
    pr_ = lax.broadcasted_iota(jnp.int32, (2 * CHUNK, 4 * CHUNK), 0)
    pc_ = lax.broadcasted_iota(jnp.int32, (2 * CHUNK, 4 * CHUNK), 1)
    tril_pair = ((pr_ // CHUNK == pc_ // (2 * CHUNK)) & (pc_ % CHUNK <= pr_ % CHUNK)).astype(BF16)
    wr = lax.broadcasted_iota(jnp.int32, (CHUNK, GROUP_W), 0)
    wc = lax.broadcasted_iota(jnp.int32, (CHUNK, GROUP_W), 1) % CHUNK
    strict_w = wc < wr
    incl_w = wc <= wr
    eye_w = (wc == wr).astype(F32)
    lane_g = lax.broadcasted_iota(jnp.int32, (1, GROUP_W), 1) // HEAD_DIM
    head_masks = [lane_g == h for h in range(GROUP_HEADS)]

    def mstack(x):
        xb = x.astype(BF16)
        zero = jnp.zeros_like(xb)
        return jnp.concatenate([jnp.where(hm, xb, zero) for hm in head_masks], axis=0)

    def head_rows(xb):
        return jnp.concatenate([xb[:, h * HEAD_DIM:(h + 1) * HEAD_DIM] for h in range(GROUP_HEADS)], axis=0)

    def block_diag(xw):
        return jnp.where(same_head, jnp.concatenate([xw] * GROUP_HEADS, axis=0), jnp.zeros((), BF16))

    HALF = 2 * CHUNK
    SUB_CHUNKS = 4
    RING_CHUNKS = RWKV_RING // CHUNK

    def ring_rows(c, n=1):
        r0 = (c % RING_CHUNKS) * CHUNK
        return slice(r0, r0 + n * CHUNK)

    def ring_slot(c, g):
        return (c % RING_CHUNKS) * n_groups + g

    lane_z = lax.broadcasted_iota(jnp.int32, (1, DECAY_RANK + ICLR_RANK), 1)

    def run(main, side=None, side_start=0):
        gens = [main] + ([side] if side is not None else [])
        stage_no = 0
        while gens:
            for gen in list(gens):
                if gen is side and stage_no < side_start and main in gens:
                    continue
                try:
                    next(gen)
                except StopIteration:
                    gens.remove(gen)
            stage_no += 1

    def seq(*gens):
        for gen in gens:
            yield from gen

    def prep(half):
        hr = slice(half * HALF, (half + 1) * HALF)
        rr_ = ring_rows(2 * half, 2)
        z = pr_ref[hr, 3 * W:3 * W + DECAY_RANK + ICLR_RANK]
        z = jnp.where(lane_z < DECAY_RANK, jnp.tanh(z), z)
        z_hi, z_lo = _split2(z)
        lin = _dot(jnp.concatenate([z_hi, z_lo], axis=1), w2a_hi_ref[...]) + _dot(z_hi, w2a_lo_ref[...])
        k = pr_ref[hr, W:2 * W]
        kkr = k * kk_ref[...]
        ss = _head_sum(kkr * kkr, ones_b, split=False)
        yield
        ld = -EXP_M_HALF * jax.nn.sigmoid(w0_ref[...] + lin[:, :W])
        ld_hi, ld_lo = _split2(ld)
        cs = _dot(tril_pair, jnp.concatenate([ld_hi[:CHUNK], ld_lo[:CHUNK], ld_hi[CHUNK:], ld_lo[CHUNK:]], axis=0))
        a = jax.nn.sigmoid(a0_ref[...] + lin[:, W:])
        k2 = k * (1.0 + (a - 1.0) * ka_ref[...])
        r = pr_ref[hr, 0:W]
        bsum = _head_sum(r * k2 * rk_ref[...], ones_b)
        yield
        kk = kkr * jnp.minimum(lax.rsqrt(ss), 1e12)
        g_inv = jnp.exp(-cs)
        at_ref[rr_, :] = (-kk) * jnp.exp(cs - ld)
        bt_ref[rr_, :] = kk * a * g_inv
        kt_ref[rr_, :] = k2 * g_inv
        yield
        g_incl = jnp.exp(cs)
        v = pr_ref[hr, 2 * W:3 * W]
        rt_ref[rr_, :] = r * g_incl
        v_ref[rr_, :] = v
        gl_ref[rr_, :] = g_incl
        bonus_ref[hr, :] = bsum * v
        yield

    def chain_factors(batch_chains):
        nc = range(len(batch_chains))
        rows = [ring_rows(c) for c, _ in batch_chains]
        yrows = [slice(c * CHUNK, (c + 1) * CHUNK) for c, _ in batch_chains]
        cols = [slice(g * GROUP_W, (g + 1) * GROUP_W) for _, g in batch_chains]
        x = []
        for j in nc:
            lhs = jnp.concatenate([at_ref[rows[j], cols[j]], rt_ref[rows[j], cols[j]]], axis=0).astype(BF16)
            rhs = jnp.concatenate([mstack(bt_ref[rows[j], cols[j]]), mstack(kt_ref[rows[j], cols[j]])], axis=0)
            x.append(_dot(lhs, rhs, _NT))
        yield
        a_ab = [jnp.where(strict_w, x[j][:CHUNK, :GROUP_W], 0.0) for j in nc]
        a_ak = [jnp.where(strict_w, x[j][:CHUNK, GROUP_W:], 0.0).astype(BF16) for j in nc]
        m_rb = [jnp.where(incl_w, x[j][CHUNK:, :GROUP_W], 0.0).astype(BF16) for j in nc]
        m_rk = [jnp.where(incl_w, x[j][CHUNK:, GROUP_W:], 0.0).astype(BF16) for j in nc]
        tw = [a_ab[j] + eye_w for j in nc]
        pw = [a_ab[j].astype(BF16) for j in nc]
        pw = [_dot(pw[j], block_diag(pw[j])).astype(BF16) for j in nc]
        yield
        for it in range(CHUNK.bit_length() - 3):
            both = [_dot(jnp.concatenate([pw[j], tw[j].astype(BF16)], axis=0), block_diag(pw[j])) for j in nc]
            tw = [tw[j] + both[j][CHUNK:] for j in nc]
            pw = [both[j][:CHUNK].astype(BF16) for j in nc]
            yield
        tw = [tw[j] + _dot(tw[j].astype(BF16), block_diag(pw[j])) for j in nc]
        twb = [tw[j].astype(BF16) for j in nc]
        yield
        v_m = [mstack(v_ref[rows[j], cols[j]]) for j in nc]
        wt = [_dot(twb[j], mstack(at_ref[rows[j], cols[j]])) for j in nc]
        x1 = [_dot(a_ak[j], v_m[j]) for j in nc]
        yield
        u0 = [_dot(twb[j], mstack(x1[j])) for j in nc]
        for j in nc:
            q_ref[rows[j], cols[j]] = (rt_ref[rows[j], cols[j]] + _dot(m_rb[j], mstack(wt[j]))).astype(BF16)
        yield
        for j in nc:
            y_ref[yrows[j], cols[j]] = _dot(m_rb[j], mstack(u0[j])) + _dot(m_rk[j], v_m[j])
        yield
        for j in nc:
            c, g = batch_chains[j]
            gl = gl_ref[rows[j].stop - 1:rows[j].stop, cols[j]]
            bh = (bt_ref[rows[j], cols[j]] * gl).astype(BF16)
            kh = (kt_ref[rows[j], cols[j]] * gl).astype(BF16)
            bh_rows, kh_rows = head_rows(bh), head_rows(kh)
            gw = eye_w * gl + _dot(bh_rows, mstack(wt[j]), _TN)
            g_ref[ring_slot(c, g)] = gw.astype(BF16)
            h_ref[ring_slot(c, g)] = _dot(
                jnp.concatenate([bh_rows, kh_rows], axis=0),
                jnp.concatenate([mstack(u0[j]), v_m[j]], axis=0), _TN)
        yield

    def recurrence(c):
        rows = slice(c * CHUNK, (c + 1) * CHUNK)
        for g in range(n_groups):
            cols = slice(g * GROUP_W, (g + 1) * GROUP_W)
            s_b = block_diag(s_ref[g].astype(BF16))
            y_ref[rows, cols] = y_ref[rows, cols] + _dot(q_ref[ring_rows(c), cols], s_b)
            s_ref[g] = _dot(g_ref[ring_slot(c, g)], s_b) + h_ref[ring_slot(c, g)]
        yield

    def finish(half):
        hr = slice(half * HALF, (half + 1) * HALF)
        y = y_ref[hr, :]
        mean = _head_sum(y, ones_b) * (1.0 / HEAD_DIM)
        yield
        yc = y - mean
        var = _head_sum(yc * yc, ones_b) * (1.0 / HEAD_DIM)
        yield
        yn = yc * lax.rsqrt(var + GN_EPS) * lw_ref[...] + lb_ref[...] + bonus_ref[hr, :]
        yr_ref[hr, :] = (yn * gs_ref[hr, :].astype(F32)).astype(BF16)
        yield

    def project(sub):
        sr = slice(sub * SUB_CHUNKS * CHUNK, (sub + 1) * SUB_CHUNKS * CHUNK)
        o = _dot(yr_ref[sr, :], wor_ref[...]) + _dot(ya_ref[sr, :], woa_ref[...])
        yield
        ms = jnp.mean(o * o, axis=-1, keepdims=True)
        o_ref[sr, :] = x_ref[sr, :] + o * lax.rsqrt(ms + RMS_EPS) * gp_ref[...]
        yield

    n_sub = n_chunks // SUB_CHUNKS
    halves = lambda sub: (2 * sub, 2 * sub + 1)
    chunks = lambda sub: range(SUB_CHUNKS * sub, SUB_CHUNKS * (sub + 1))
    tail = lambda sub: [recurrence(c) for c in chunks(sub)] + [finish(hf) for hf in halves(sub)]
    run(seq(*[prep(hf) for hf in halves(0)]))
    for sub in range(n_sub):
        side = [prep(hf) for hf in halves(sub + 1)] if sub + 1 < n_sub else []
        side += tail(sub - 1) if sub > 0 else []
        run(chain_factors([(c, g) for c in chunks(sub) for g in range(n_groups)]), seq(*side), side_start=2)
    run(seq(*tail(n_sub - 1)), seq(*[project(sub) for sub in range(n_sub - 1)]))
    run(project(n_sub - 1))


def _rwkv_out(pr, gs, w0, w2a, a0, k_k, k_a, r_k, lnx_w, lnx_b, x2, ya, w_or, w_oa, g_post, batch, seq):
    m, d = x2.shape
    per_b = seq // RWKV_TILE
    W = RWKV_WIDTH
    n_cg = (RWKV_RING // CHUNK) * (W // GROUP_W)
    w2a_hi = w2a.astype(BF16)
    w2a_lo = (w2a - w2a_hi.astype(F32)).astype(BF16)
    w2a_hi = jnp.concatenate([w2a_hi, w2a_hi], axis=0)
    tile = lambda b, t: (b * per_b + t, 0)
    const = lambda b, t: (0, 0)
    vec = pl.BlockSpec((1, W), const)
    ring_scratch = pltpu.VMEM((RWKV_RING, W), F32)
    tile_scratch = pltpu.VMEM((RWKV_TILE, W), F32)
    return pl.pallas_call(
        _rwkv_body,
        grid=(batch, per_b),
        in_specs=[
            pl.BlockSpec((RWKV_TILE, N_RWKV_COLS), tile),
            pl.BlockSpec((RWKV_TILE, W), tile),
            vec,
            pl.BlockSpec(w2a_hi.shape, const),
            pl.BlockSpec(w2a_lo.shape, const),
            vec, vec, vec, vec, vec, vec,
            pl.BlockSpec((RWKV_TILE, d), tile),
            pl.BlockSpec((RWKV_TILE, ATTN_WIDTH), tile),
            pl.BlockSpec(w_or.shape, const),
            pl.BlockSpec(w_oa.shape, const),
            pl.BlockSpec((1, d), const),
        ],
        out_specs=pl.BlockSpec((RWKV_TILE, d), tile),
        out_shape=jax.ShapeDtypeStruct((m, d), F32),
        scratch_shapes=[
            pltpu.VMEM((W // GROUP_W, HEAD_DIM, GROUP_W), F32),
            ring_scratch, ring_scratch, ring_scratch, ring_scratch, ring_scratch, ring_scratch,
            tile_scratch, tile_scratch,
            pltpu.VMEM((RWKV_RING, W), BF16),
            pltpu.VMEM((n_cg, HEAD_DIM, GROUP_W), BF16),
            pltpu.VMEM((n_cg, HEAD_DIM, GROUP_W), F32),
            pltpu.VMEM((RWKV_TILE, W), BF16),
        ],
        compiler_params=pltpu.CompilerParams(
            dimension_semantics=("arbitrary", "arbitrary"), vmem_limit_bytes=VMEM_LIMIT),
        name="rwkv7_out",
    )(pr, gs, w0, w2a_hi, w2a_lo, a0, k_k, k_a, r_k, lnx_w, lnx_b, x2, ya, w_or, w_oa, g_post)


def kernel(x, g_pre, w_in, tshift_mu, w0, w2, a0, a2, k_k, k_a, r_k, lnx_w, lnx_b, w_out, g_post):
    batch, seq, d = x.shape
    depth = w_in.shape[0]
    assert seq % RWKV_TILE == 0 and seq % IN_TILE == 0 and IN_TILE % MOBA_BLOCK == 0 and d % LANES == 0
    x2 = x.reshape(batch * seq, d)
    slopes = 2.0 ** (-8.0 * jnp.arange(1, ATTN_HEADS + 1, dtype=F32) / ATTN_HEADS)
    row = lambda z: z.reshape(1, -1).astype(F32)
    for l in range(depth):
        wb = w_in[l].astype(BF16)
        w_r = wb[:, :N_RWKV_COLS]
        w_a = wb[:, N_RWKV_COLS:N_RWKV_COLS + N_ATTN_COLS]
        w_g = wb[:, N_RWKV_COLS + N_ATTN_COLS:]
        pr, q, k, vt, kmean, gs = _in_proj(x2, row(g_pre[l]), row(tshift_mu[l]), w_r, w_a, w_g, batch, seq)
        ya = _moba(slopes, q, k, _moba_key_aux(seq), vt,
                   kmean.reshape(batch, seq // MOBA_BLOCK, ATTN_WIDTH), gs, batch, seq)
        zeros = jnp.zeros((DECAY_RANK, RWKV_WIDTH), F32)
        w2a = jnp.concatenate([jnp.concatenate([w2[l], zeros], axis=1),
                               jnp.concatenate([zeros, a2[l]], axis=1)], axis=0)
        wo = w_out[l].astype(BF16)
        x2 = _rwkv_out(pr, gs, row(w0[l]), w2a, row(a0[l]), row(k_k[l]), row(k_a[l]), row(r_k[l]),
                       row(lnx_w[l]), row(lnx_b[l]), x2, ya, wo[:RWKV_WIDTH], wo[RWKV_WIDTH:],
                       row(g_post[l]), batch, seq)
    return x2.reshape(batch, seq, d)
```

```python
import functools

import jax
import jax.numpy as jnp
from jax import lax
from jax.experimental import pallas as pl
from jax.experimental.pallas import tpu as pltpu

F32 = jnp.float32
BF16 = jnp.bfloat16

HEAD_DIM = 64
RWKV_HEADS = 8
ATTN_HEADS = 8
RWKV_WIDTH = RWKV_HEADS * HEAD_DIM
ATTN_WIDTH = ATTN_HEADS * HEAD_DIM
D_MIX = RWKV_WIDTH + ATTN_WIDTH
DECAY_RANK = 64
ICLR_RANK = 64
MOBA_BLOCK = 256
MOBA_TOPK = 3
RMS_EPS = 1e-6
GN_EPS = 64e-5
EXP_M_HALF = 0.6065306597126334
NEG = -1e30
N_RWKV_COLS = 3 * RWKV_WIDTH + DECAY_RANK + ICLR_RANK
N_ATTN_COLS = 3 * ATTN_WIDTH

LANES = 128
MXU_DIM = 256
CHUNK = 64
GROUP_HEADS = MXU_DIM // HEAD_DIM
GROUP_W = GROUP_HEADS * HEAD_DIM
IN_TILE = 1024
RWKV_TILE = 512
RWKV_RING = 512
VMEM_LIMIT = 48 * 1024 * 1024

_NT = (((1,), (1,)), ((), ()))
_TN = (((0,), (0,)), ((), ()))


def _dot(a, b, dims=None, precision=None):
    if dims is None:
        return jnp.dot(a, b, preferred_element_type=F32, precision=precision)
    return lax.dot_general(a, b, dims, preferred_element_type=F32, precision=precision)


def _inproj_body(per_b, x_ref, g_ref, mu_ref, wr_ref, wa_ref, wg_ref,
                 pr_ref, q_ref, k_ref, vt_ref, km_ref, gs_ref, prev_ref):
    x = x_ref[...]
    ms = jnp.mean(x * x, axis=-1, keepdims=True)
    h = (x * lax.rsqrt(ms + RMS_EPS) * g_ref[...]).astype(BF16)
    @pl.when(pl.program_id(0) % per_b == 0)
    def _():
        prev_ref[...] = jnp.zeros_like(prev_ref)

    row0 = lax.broadcasted_iota(jnp.int32, (IN_TILE, 1), 0) == 0
    for c0 in range(0, N_RWKV_COLS, MXU_DIM):
        cs = slice(c0, min(c0 + MXU_DIM, N_RWKV_COLS))
        p = _dot(h, wr_ref[:, cs])
        p_prev = jnp.where(row0, prev_ref[:, cs], pltpu.roll(p, 1, axis=0))
        prev_ref[:, cs] = p[IN_TILE - 1:IN_TILE, :]
        pr_ref[:, cs] = p + (p_prev - p) * mu_ref[:, cs]
    qkv = _dot(h, wa_ref[...])
    q_ref[...] = qkv[:, :ATTN_WIDTH] * (HEAD_DIM ** -0.5)
    kf = qkv[:, ATTN_WIDTH:2 * ATTN_WIDTH]
    k_ref[...] = kf.astype(BF16)
    for j in range(IN_TILE // MOBA_BLOCK):
        km_ref[j] = jnp.mean(kf[j * MOBA_BLOCK:(j + 1) * MOBA_BLOCK], axis=0, keepdims=True)
    vt_ref[0] = qkv[:, 2 * ATTN_WIDTH:].T.astype(BF16)
    g = _dot(h, wg_ref[...])
    gs_ref[...] = (g * jax.nn.sigmoid(g)).astype(BF16)


def _in_proj(x2, g_pre, mu, w_r, w_a, w_g, batch, seq):
    m, d = x2.shape
    nblk = m // IN_TILE
    per_b = seq // IN_TILE
    blocks = IN_TILE // MOBA_BLOCK
    const = lambda i: (0, 0)
    return pl.pallas_call(
        functools.partial(_inproj_body, per_b),
        grid=(nblk,),
        in_specs=[
            pl.BlockSpec((IN_TILE, d), lambda i: (i, 0)),
            pl.BlockSpec((1, d), const),
            pl.BlockSpec((1, N_RWKV_COLS), const),
            pl.BlockSpec(w_r.shape, const, pipeline_mode=pl.Buffered(1)),
            pl.BlockSpec(w_a.shape, const, pipeline_mode=pl.Buffered(1)),
            pl.BlockSpec(w_g.shape, const, pipeline_mode=pl.Buffered(1)),
        ],
        out_specs=[
            pl.BlockSpec((IN_TILE, N_RWKV_COLS), lambda i: (i, 0)),
            pl.BlockSpec((IN_TILE, ATTN_WIDTH), lambda i: (i, 0)),
            pl.BlockSpec((IN_TILE, ATTN_WIDTH), lambda i: (i, 0)),
            pl.BlockSpec((1, ATTN_WIDTH, IN_TILE), lambda i: (i // per_b, 0, i % per_b)),
            pl.BlockSpec((blocks, 1, ATTN_WIDTH), lambda i: (i, 0, 0)),
            pl.BlockSpec((IN_TILE, D_MIX), lambda i: (i, 0)),
        ],
        out_shape=[
            jax.ShapeDtypeStruct((m, N_RWKV_COLS), F32),
            jax.ShapeDtypeStruct((m, ATTN_WIDTH), F32),
            jax.ShapeDtypeStruct((m, ATTN_WIDTH), BF16),
            jax.ShapeDtypeStruct((batch, ATTN_WIDTH, seq), BF16),
            jax.ShapeDtypeStruct((m // MOBA_BLOCK, 1, ATTN_WIDTH), F32),
            jax.ShapeDtypeStruct((m, D_MIX), BF16),
        ],
        scratch_shapes=[pltpu.VMEM((1, N_RWKV_COLS), F32)],
        compiler_params=pltpu.CompilerParams(
            dimension_semantics=("arbitrary",), vmem_limit_bytes=VMEM_LIMIT),
        name="in_proj",
    )(x2, g_pre, mu, w_r, w_a, w_g)


LOG2E = 1.4426950408889634
AUX_S, AUX_T, AUX_BLK, AUX_SEL = 0, 3, 6, 16
PV_ONES_ROWS = 16
MOBA_STEP_HEADS = 4
MOBA_STEP_BLOCKS = 4
MOBA_TRIP_WIDTHS = (4, 2, 1)


def _moba_key_aux(seq):
    pos = jnp.arange(seq, dtype=jnp.int32)[:, None]
    lane = jnp.arange(LANES, dtype=jnp.int32)[None, :]
    blk = pos // MOBA_BLOCK
    tab = jnp.where(lane < AUX_T, pos % MOBA_BLOCK,
                    jnp.where(lane < AUX_BLK, 1,
                              jnp.where(lane < AUX_BLK + 3, blk * MOBA_BLOCK,
                                        jnp.where(lane == AUX_SEL + blk, 1, 0))))
    return tab.astype(BF16)


def _split2(x):
    hi = x.astype(BF16)
    lo = (x - hi.astype(F32)).astype(BF16)
    return hi, lo


def _split3(x):
    x1 = x.astype(BF16).astype(F32)
    r1 = x - x1
    x2 = r1.astype(BF16).astype(F32)
    x3 = (r1 - x2).astype(BF16).astype(F32)
    return x1, x2, x3


def _moba_body(slopes_ref, q_ref, k_ref, ak_ref, vt_ref, km_ref, gs_ref, o_ref, st_ref, p_ref):
    BS = MOBA_BLOCK
    grp = pl.program_id(1)
    step = pl.program_id(2)
    heads = MOBA_STEP_HEADS
    per_pair = LANES // HEAD_DIM
    nb = km_ref.shape[1]
    n_iota = lax.broadcasted_iota(jnp.int32, (nb, BS), 0)
    causal = (lax.broadcasted_iota(jnp.int32, (BS, BS), 0)
              <= lax.broadcasted_iota(jnp.int32, (BS, BS), 1))
    piece = n_iota % 3
    ones_rows = jnp.ones((PV_ONES_ROWS, BS), BF16)
    lane_head = lax.broadcasted_iota(jnp.int32, (1, heads * HEAD_DIM), 1) // HEAD_DIM
    row_h = lax.broadcasted_iota(jnp.int32, (LANES, 1), 0) // HEAD_DIM
    zero_b = jnp.zeros((), BF16)
    aux_pad = jnp.zeros((LANES - AUX_SEL - nb, BS), BF16)
    no_mask = jnp.zeros((nb, BS), BF16)
    km = km_ref[0]
    km_stack = jnp.concatenate([jnp.where(lane_head == h, km, 0.0) for h in range(heads)], axis=0)
    km_hi, km_lo = _split2(km_stack)
    km_cat = jnp.concatenate([km_hi, km_lo, km_hi], axis=1)

    def pair_lanes(h):
        p = h // per_pair
        return slice(p * LANES, (p + 1) * LANES)

    def query_block(u):
        i = MOBA_STEP_BLOCKS * step + u
        q_rows = slice(u * BS, (u + 1) * BS)

        def key_rows(n):
            b = jnp.where(n == 0, i, jnp.minimum(n - 1, i))
            return pl.ds(pl.multiple_of(b * BS, BS), BS)

        def scores(h, blk, q_aug_t):
            k_aug = jnp.concatenate([k_ref[blk, pair_lanes(h)], ak_ref[blk, :]], axis=1)
            return _dot(k_aug, q_aug_t)

        def pv(h, slot, blk):
            vt_aug = jnp.concatenate([vt_ref[0, h * HEAD_DIM:(h + 1) * HEAD_DIM, blk], ones_rows], axis=0)
            return _dot(vt_aug, p_ref[u, slot, h])

        def prologue():
            q_t = (q_ref[q_rows, :] * LOG2E).T
            q_hi, q_lo = _split2(q_t)
            gates = _dot(km_cat, jnp.concatenate([q_hi, q_hi, q_lo], axis=0))
            gate = [jnp.where(n_iota < i, gates[h * nb:(h + 1) * nb], NEG) for h in range(heads)]
            t_glob = (lax.broadcasted_iota(jnp.int32, (nb, BS), 1) + i * BS).astype(F32)
            qh_t, aux_top = [], []
            for h in range(heads):
                cvec = jnp.full((nb, BS), slopes_ref[grp * heads + h] * LOG2E, F32)
                c1, c2, c3 = _split3(cvec)
                e1, e2, e3 = _split3(-cvec * t_glob)
                cp = jnp.where(piece == 0, c1, jnp.where(piece == 1, c2, c3))
                ep = jnp.where(piece == 0, e1, jnp.where(piece == 1, e2, e3))
                top = jnp.where((n_iota >= AUX_T) & (n_iota < AUX_BLK), ep, cp)
                aux_top.append(jnp.where(n_iota < AUX_BLK + 3, top, 0.0).astype(BF16))
                qh_t.append(jnp.where(row_h == h % per_pair, q_hi[pair_lanes(h), :], zero_b))

            def q_aug(h, sel_rows):
                return jnp.concatenate([qh_t[h], aux_top[h], sel_rows, aux_pad], axis=0)

            own = [jnp.where(causal, scores(h, key_rows(0), q_aug(h, no_mask)), NEG) for h in range(heads)]
            sel = [jnp.zeros((nb, BS), F32) for _ in range(heads)]
            for rnk in range(MOBA_TOPK):
                rank_ok = jnp.where(rnk < i, 1.0, 0.0)
                for h in range(heads):
                    mx = jnp.max(gate[h], axis=0, keepdims=True)
                    idx = jnp.min(jnp.where(gate[h] == mx, n_iota, nb), axis=0, keepdims=True)
                    pick = n_iota == idx
                    sel[h] = jnp.where(pick, rank_ok, sel[h])
                    gate[h] = jnp.where(pick, -jnp.inf, gate[h])
            q_augs = [q_aug(h, jnp.where(sel[h] > 0.0, 0.0, NEG).astype(BF16)) for h in range(heads)]
            m_own = []
            for h in range(heads):
                m_own.append(jnp.max(own[h], axis=0, keepdims=True))
                p_ref[u, 0, h] = jnp.exp2(own[h] - m_own[h]).astype(BF16)
                st_ref[u, 0, h] = scores(h, key_rows(1), q_augs[h])
            return q_augs, tuple(m_own)

        def loops(q_augs, m_own):
            def stage(n, carry, src, dst):
                ms, accs, alphas = carry
                prev_rows = key_rows(n - 1)
                next_rows = key_rows(n + 1)
                pvs = [pv(h, src, prev_rows) for h in range(heads)]
                ms2, alphas2 = [], []
                for h in range(heads):
                    st = st_ref[u, src, h]
                    m_new = jnp.maximum(ms[h], jnp.max(st, axis=0, keepdims=True))
                    alphas2.append(jnp.exp2(ms[h] - m_new))
                    p_ref[u, dst, h] = jnp.exp2(st - m_new).astype(BF16)
                    ms2.append(m_new)
                for h in range(heads):
                    st_ref[u, dst, h] = scores(h, next_rows, q_augs[h])
                accs2 = [alphas[h] * accs[h] + pvs[h] for h in range(heads)]
                return tuple(ms2), tuple(accs2), tuple(alphas2)

            def run_blocks(first, count, carry):
                for j in range(count):
                    last_odd = j == count - 1 and count % 2 == 1
                    carry = stage(first + j, carry, j % 2, 0 if last_odd else 1 - j % 2)
                return carry

            acc0 = tuple(jnp.zeros((HEAD_DIM + PV_ONES_ROWS, BS), F32) for _ in range(heads))
            alpha0 = tuple(jnp.ones((1, BS), F32) for _ in range(heads))
            n_blocks = i + 1
            done = 1
            carry = (m_own, acc0, alpha0)
            for width in MOBA_TRIP_WIDTHS:
                left = (n_blocks - done) // width
                trips = left if width == MOBA_TRIP_WIDTHS[0] else left % 2
                base = done
                carry = lax.fori_loop(
                    0, trips, lambda t, c, w=width, b=base: run_blocks(b + w * t, w, c), carry)
                done = done + trips * width
            return carry

        def epilogue(carry):
            ms, accs, alphas = carry
            last_rows = key_rows(i)
            outs = []
            for h in range(heads):
                acc = alphas[h] * accs[h] + pv(h, 0, last_rows)
                outs.append(acc[:HEAD_DIM] * (1.0 / acc[HEAD_DIM:HEAD_DIM + 1]))
            ot = jnp.concatenate(outs, axis=0)
            o_ref[q_rows, :] = (ot.T * gs_ref[q_rows, :].astype(F32)).astype(BF16)

        return prologue, loops, epilogue

    parts = [query_block(u) for u in range(MOBA_STEP_BLOCKS)]
    carry = None
    for u, (prologue, loops, epilogue) in enumerate(parts):
        q_augs, m_own = prologue()
        if u > 0:
            parts[u - 1][2](carry)
        carry = loops(q_augs, m_own)
    parts[-1][2](carry)


def _moba(slopes, q, k, key_aux, vt, kmean, gs, batch, seq):
    m = q.shape[0]
    BS = MOBA_BLOCK
    nb = seq // BS
    assert nb == AUX_SEL and AUX_SEL + nb <= LANES and AUX_BLK + 3 <= AUX_SEL and nb % MOBA_STEP_BLOCKS == 0
    H, U = MOBA_STEP_HEADS, MOBA_STEP_BLOCKS
    W = H * HEAD_DIM
    n_groups = ATTN_WIDTH // W
    n_steps = nb // U
    gate_col0 = RWKV_WIDTH // W
    return pl.pallas_call(
        _moba_body,
        grid_spec=pltpu.PrefetchScalarGridSpec(
            num_scalar_prefetch=1,
            grid=(batch, n_groups, n_steps),
            in_specs=[
                pl.BlockSpec((U * BS, W), lambda b, g, i, s: (b * n_steps + i, g)),
                pl.BlockSpec((seq, W), lambda b, g, i, s: (b, g)),
                pl.BlockSpec((seq, LANES), lambda b, g, i, s: (0, 0)),
                pl.BlockSpec((1, W, seq), lambda b, g, i, s: (b, g, 0)),
                pl.BlockSpec((1, nb, W), lambda b, g, i, s: (b, 0, g)),
                pl.BlockSpec((U * BS, W), lambda b, g, i, s: (b * n_steps + i, gate_col0 + g)),
            ],
            out_specs=pl.BlockSpec((U * BS, W), lambda b, g, i, s: (b * n_steps + i, g)),
            scratch_shapes=[
                pltpu.VMEM((U, 2, H, BS, BS), F32),
                pltpu.VMEM((U, 2, H, BS, BS), BF16),
            ],
        ),
        out_shape=jax.ShapeDtypeStruct((m, ATTN_WIDTH), BF16),
        compiler_params=pltpu.CompilerParams(
            dimension_semantics=("arbitrary", "arbitrary", "arbitrary"), vmem_limit_bytes=VMEM_LIMIT),
        name="moba",
    )(slopes, q, k, key_aux, vt, kmean, gs)


def _split_dot(x, ones_b):
    hi, lo = _split2(x)
    return _dot(hi, ones_b) + _dot(lo, ones_b)


def _head_sum(x, ones_b, split=True):
    one = _split_dot if split else (lambda xs, o: _dot(xs.astype(BF16), o))
    parts = [one(x[:, g * GROUP_W:(g + 1) * GROUP_W], ones_b) for g in range(x.shape[1] // GROUP_W)]
    return jnp.concatenate(parts, axis=1)


def _rwkv_body(pr_ref, gs_ref, w0_ref, w2a_hi_ref, w2a_lo_ref, a0_ref, kk_ref, ka_ref, rk_ref, lw_ref,
               lb_ref, x_ref, ya_ref, wor_ref, woa_ref, gp_ref,
               o_ref, s_ref, at_ref, bt_ref, kt_ref, rt_ref, v_ref, gl_ref, y_ref, bonus_ref,
               q_ref, g_ref, h_ref, yr_ref):
    W = RWKV_WIDTH
    n_groups = W // GROUP_W
    n_chunks = RWKV_TILE // CHUNK

    @pl.when(pl.program_id(1) == 0)
    def _():
        s_ref[...] = jnp.zeros_like(s_ref)

    rr = lax.broadcasted_iota(jnp.int32, (GROUP_W, GROUP_W), 0)
    cc = lax.broadcasted_iota(jnp.int32, (GROUP_W, GROUP_W), 1)
    same_head = (rr // HEAD_DIM) == (cc // HEAD_DIM)
    ones_b = same_head.astype(BF16)
    pr_ = lax.broadcasted_iota(jnp.int32, (2 * CHUNK, 4 * CHUNK), 0)
    pc_ = lax.broadcasted_iota(jnp.int32, (2 * CHUNK, 4 * CHUNK), 1)
    tril_pair = ((pr_ // CHUNK == pc_ // (2 * CHUNK)) & (pc_ % CHUNK <= pr_ % CHUNK)).astype(BF16)
    wr = lax.broadcasted_iota(jnp.int32, (CHUNK, GROUP_W), 0)
    wc = lax.broadcasted_iota(jnp.int32, (CHUNK, GROUP_W), 1) % CHUNK
    strict_w = wc < wr
    incl_w = wc <= wr
    eye_w = (wc == wr).astype(F32)
    lane_g = lax.broadcasted_iota(jnp.int32, (1, GROUP_W), 1) // HEAD_DIM
    head_masks = [lane_g == h for h in range(GROUP_HEADS)]

    def mstack(x):
        xb = x.astype(BF16)
        zero = jnp.zeros_like(xb)
        return jnp.concatenate([jnp.where(hm, xb, zero) for hm in head_masks], axis=0)

    def head_rows(xb):
        return jnp.concatenate([xb[:, h * HEAD_DIM:(h + 1) * HEAD_DIM] for h in range(GROUP_HEADS)], axis=0)

    def block_diag(xw):
        return jnp.where(same_head, jnp.concatenate([xw] * GROUP_HEADS, axis=0), jnp.zeros((), BF16))

    HALF = 2 * CHUNK
    SUB_CHUNKS = 4
    RING_CHUNKS = RWKV_RING // CHUNK

    def ring_rows(c, n=1):
        r0 = (c % RING_CHUNKS) * CHUNK
        return slice(r0, r0 + n * CHUNK)

    def ring_slot(c, g):
        return (c % RING_CHUNKS) * n_groups + g

    lane_z = lax.broadcasted_iota(jnp.int32, (1, DECAY_RANK + ICLR_RANK), 1)

    def run(main, side=None, side_start=0):
        gens = [main] + ([side] if side is not None else [])
        stage_no = 0
        while gens:
            for gen in list(gens):
                if gen is side and stage_no < side_start and main in gens:
                    continue
                try:
                    next(gen)
                except StopIteration:
                    gens.remove(gen)
            stage_no += 1

    def seq(*gens):
        for gen in gens:
            yield from gen

    def prep(half):
        hr = slice(half * HALF, (half + 1) * HALF)
        rr_ = ring_rows(2 * half, 2)
        z = pr_ref[hr, 3 * W:3 * W + DECAY_RANK + ICLR_RANK]
        z = jnp.where(lane_z < DECAY_RANK, jnp.tanh(z), z)
        z_hi, z_lo = _split2(z)
        lin = _dot(jnp.concatenate([z_hi, z_lo], axis=1), w2a_hi_ref[...]) + _dot(z_hi, w2a_lo_ref[...])
        k = pr_ref[hr, W:2 * W]
        kkr = k * kk_ref[...]
        ss = _head_sum(kkr * kkr, ones_b, split=False)
        yield
        ld = -EXP_M_HALF * jax.nn.sigmoid(w0_ref[...] + lin[:, :W])
        ld_hi, ld_lo = _split2(ld)
        cs = _dot(tril_pair, jnp.concatenate([ld_hi[:CHUNK], ld_lo[:CHUNK], ld_hi[CHUNK:], ld_lo[CHUNK:]], axis=0))
        a = jax.nn.sigmoid(a0_ref[...] + lin[:, W:])
        k2 = k * (1.0 + (a - 1.0) * ka_ref[...])
        r = pr_ref[hr, 0:W]
        bsum = _head_sum(r * k2 * rk_ref[...], ones_b)
        yield
        kk = kkr * jnp.minimum(lax.rsqrt(ss), 1e12)
        g_inv = jnp.exp(-cs)
        at_ref[rr_, :] = (-kk) * jnp.exp(cs - ld)
        bt_ref[rr_, :] = kk * a * g_inv
        kt_ref[rr_, :] = k2 * g_inv
        yield
        g_incl = jnp.exp(cs)
        v = pr_ref[hr, 2 * W:3 * W]
        rt_ref[rr_, :] = r * g_incl
        v_ref[rr_, :] = v
        gl_ref[rr_, :] = g_incl
        bonus_ref[hr, :] = bsum * v
        yield

    def chain_factors(batch_chains):
        nc = range(len(batch_chains))
        rows = [ring_rows(c) for c, _ in batch_chains]
        yrows = [slice(c * CHUNK, (c + 1) * CHUNK) for c, _ in batch_chains]
        cols = [slice(g * GROUP_W, (g + 1) * GROUP_W) for _, g in batch_chains]
        x = []
        for j in nc:
            lhs = jnp.concatenate([at_ref[rows[j], cols[j]], rt_ref[rows[j], cols[j]]], axis=0).astype(BF16)
            rhs = jnp.concatenate([mstack(bt_ref[rows[j], cols[j]]), mstack(kt_ref[rows[j], cols[j]])], axis=0)
            x.append(_dot(lhs, rhs, _NT))
        yield
        a_ab = [jnp.where(strict_w, x[j][:CHUNK, :GROUP_W], 0.0) for j in nc]
        a_ak = [jnp.where(strict_w, x[j][:CHUNK, GROUP_W:], 0.0).astype(BF16) for j in nc]
        m_rb = [jnp.where(incl_w, x[j][CHUNK:, :GROUP_W], 0.0).astype(BF16) for j in nc]
        m_rk = [jnp.where(incl_w, x[j][CHUNK:, GROUP_W:], 0.0).astype(BF16) for j in nc]
        tw = [a_ab[j] + eye_w for j in nc]
        pw = [a_ab[j].astype(BF16) for j in nc]
        pw = [_dot(pw[j], block_diag(pw[j])).astype(BF16) for j in nc]
        yield
        for it in range(CHUNK.bit_length() - 3):
            both = [_dot(jnp.concatenate([pw[j], tw[j].astype(BF16)], axis=0), block_diag(pw[j])) for j in nc]
            tw = [tw[j] + both[j][CHUNK:] for j in nc]
            pw = [both[j][:CHUNK].astype(BF16) for j in nc]
            yield
        tw = [tw[j] + _dot(tw[j].astype(BF16), block_diag(pw[j])) for j in nc]
        twb = [tw[j].astype(BF16) for j in nc]
        yield
        v_m = [mstack(v_ref[rows[j], cols[j]]) for j in nc]
        wt = [_dot(twb[j], mstack(at_ref[rows[j], cols[j]])) for j in nc]
        x1 = [_dot(a_ak[j], v_m[j]) for j in nc]
        yield
        u0 = [_dot(twb[j], mstack(x1[j])) for j in nc]
        for j in nc:
            q_ref[rows[j], cols[j]] = (rt_ref[rows[j], cols[j]] + _dot(m_rb[j], mstack(wt[j]))).astype(BF16)
        yield
        for j in nc:
            y_ref[yrows[j], cols[j]] = _dot(m_rb[j], mstack(u0[j])) + _dot(m_rk[j], v_m[j])
        yield
        for j in nc:
            c, g = batch_chains[j]
            gl = gl_ref[rows[j].stop - 1:rows[j].stop, cols[j]]
            bh = (bt_ref[rows[j], cols[j]] * gl).astype(BF16)
            kh = (kt_ref[rows[j], cols[j]] * gl).astype(BF16)
            bh_rows, kh_rows = head_rows(bh), head_rows(kh)
            gw = eye_w * gl + _dot(bh_rows, mstack(wt[j]), _TN)
            g_ref[ring_slot(c, g)] = gw.astype(BF16)
            h_ref[ring_slot(c, g)] = _dot(
                jnp.concatenate([bh_rows, kh_rows], axis=0),
                jnp.concatenate([mstack(u0[j]), v_m[j]], axis=0), _TN)
        yield

    def recurrence(c):
        rows = slice(c * CHUNK, (c + 1) * CHUNK)
        for g in range(n_groups):
            cols = slice(g * GROUP_W, (g + 1) * GROUP_W)
            s_b = block_diag(s_ref[g].astype(BF16))
            y_ref[rows, cols] = y_ref[rows, cols] + _dot(q_ref[ring_rows(c), cols], s_b)
            s_ref[g] = _dot(g_ref[ring_slot(c, g)], s_b) + h_ref[ring_slot(c, g)]
        yield

    def finish(half):
        hr = slice(half * HALF, (half + 1) * HALF)
        y = y_ref[hr, :]
        mean = _head_sum(y, ones_b) * (1.0 / HEAD_DIM)
        yield
        yc = y - mean
        var = _head_sum(yc * yc, ones_b) * (1.0 / HEAD_DIM)
        yield
        yn = yc * lax.rsqrt(var + GN_EPS) * lw_ref[...] + lb_ref[...] + bonus_ref[hr, :]
        yr_ref[hr, :] = (yn * gs_ref[hr, :].astype(F32)).astype(BF16)
        yield

    def project(sub):
        sr = slice(sub * SUB_CHUNKS * CHUNK, (sub + 1) * SUB_CHUNKS * CHUNK)
        o = _dot(yr_ref[sr, :], wor_ref[...]) + _dot(ya_ref[sr, :], woa_ref[...])
        yield
        ms = jnp.mean(o * o, axis=-1, keepdims=True)
        o_ref[sr, :] = x_ref[sr, :] + o * lax.rsqrt(ms + RMS_EPS) * gp_ref[...]
        yield

    n_sub = n_chunks // SUB_CHUNKS
    halves = lambda sub: (2 * sub, 2 * sub + 1)
    chunks = lambda sub: range(SUB_CHUNKS * sub, SUB_CHUNKS * (sub + 1))
    tail = lambda sub: [recurrence(c) for c in chunks(sub)] + [finish(hf) for hf in halves(sub)]
    run(seq(*[prep(hf) for hf in halves(0)]))
    for sub in range(n_sub):
        side = [prep(hf) for hf in halves(sub + 1)] if sub + 1 < n_sub else []
        side += tail(sub - 1) if sub > 0 else []
        run(chain_factors([(c, g) for c in chunks(sub) for g in range(n_groups)]), seq(*side), side_start=2)
    run(seq(*tail(n_sub - 1)), seq(*[project(sub) for sub in range(n_sub - 1)]))
    run(project(n_sub - 1))


def _rwkv_out(pr, gs, w0, w2a, a0, k_k, k_a, r_k, lnx_w, lnx_b, x2, ya, w_or, w_oa, g_post, batch, seq):
    m, d = x2.shape
    per_b = seq // RWKV_TILE
    W = RWKV_WIDTH
    n_cg = (RWKV_RING // CHUNK) * (W // GROUP_W)
    w2a_hi = w2a.astype(BF16)
    w2a_lo = (w2a - w2a_hi.astype(F32)).astype(BF16)
    w2a_hi = jnp.concatenate([w2a_hi, w2a_hi], axis=0)
    tile = lambda b, t: (b * per_b + t, 0)
    const = lambda b, t: (0, 0)
    vec = pl.BlockSpec((1, W), const)
    ring_scratch = pltpu.VMEM((RWKV_RING, W), F32)
    tile_scratch = pltpu.VMEM((RWKV_TILE, W), F32)
    return pl.pallas_call(
        _rwkv_body,
        grid=(batch, per_b),
        in_specs=[
            pl.BlockSpec((RWKV_TILE, N_RWKV_COLS), tile),
            pl.BlockSpec((RWKV_TILE, W), tile),
            vec,
            pl.BlockSpec(w2a_hi.shape, const),
            pl.BlockSpec(w2a_lo.shape, const),
            vec, vec, vec, vec, vec, vec,
            pl.BlockSpec((RWKV_TILE, d), tile),
            pl.BlockSpec((RWKV_TILE, ATTN_WIDTH), tile),
            pl.BlockSpec(w_or.shape, const),
            pl.BlockSpec(w_oa.shape, const),
            pl.BlockSpec((1, d), const),
        ],
        out_specs=pl.BlockSpec((RWKV_TILE, d), tile),
        out_shape=jax.ShapeDtypeStruct((m, d), F32),
        scratch_shapes=[
            pltpu.VMEM((W // GROUP_W, HEAD_DIM, GROUP_W), F32),
            ring_scratch, ring_scratch, ring_scratch, ring_scratch, ring_scratch, ring_scratch,
            tile_scratch, tile_scratch,
            pltpu.VMEM((RWKV_RING, W), BF16),
            pltpu.VMEM((n_cg, HEAD_DIM, GROUP_W), BF16),
            pltpu.VMEM((n_cg, HEAD_DIM, GROUP_W), F32),
            pltpu.VMEM((RWKV_TILE, W), BF16),
        ],
        compiler_params=pltpu.CompilerParams(
            dimension_semantics=("arbitrary", "arbitrary"), vmem_limit_bytes=VMEM_LIMIT),
        name="rwkv7_out",
    )(pr, gs, w0, w2a_hi, w2a_lo, a0, k_k, k_a, r_k, lnx_w, lnx_b, x2, ya, w_or, w_oa, g_post)


def kernel(x, g_pre, w_in, tshift_mu, w0, w2, a0, a2, k_k, k_a, r_k, lnx_w, lnx_b, w_out, g_post):
    batch, seq, d = x.shape
    depth = w_in.shape[0]
    assert seq % RWKV_TILE == 0 and seq % IN_TILE == 0 and IN_TILE % MOBA_BLOCK == 0 and d % LANES == 0
    x2 = x.reshape(batch * seq, d)
    slopes = 2.0 ** (-8.0 * jnp.arange(1, ATTN_HEADS + 1, dtype=F32) / ATTN_HEADS)
    row = lambda z: z.reshape(1, -1).astype(F32)
    for l in range(depth):
        wb = w_in[l].astype(BF16)
        w_r = wb[:, :N_RWKV_COLS]
        w_a = wb[:, N_RWKV_COLS:N_RWKV_COLS + N_ATTN_COLS]
        w_g = wb[:, N_RWKV_COLS + N_ATTN_COLS:]
        pr, q, k, vt, kmean, gs = _in_proj(x2, row(g_pre[l]), row(tshift_mu[l]), w_r, w_a, w_g, batch, seq)
        ya = _moba(slopes, q, k, _moba_key_aux(seq), vt,
                   kmean.reshape(batch, seq // MOBA_BLOCK, ATTN_WIDTH), gs, batch, seq)
        zeros = jnp.zeros((DECAY_RANK, RWKV_WIDTH), F32)
        w2a = jnp.concatenate([jnp.concatenate([w2[l], zeros], axis=1),
                               jnp.concatenate([zeros, a2[l]], axis=1)], axis=0)
        wo = w_out[l].astype(BF16)
        x2 = _rwkv_out(pr, gs, row(w0[l]), w2a, row(a0[l]), row(k_k[l]), row(k_a[l]), row(r_k[l]),
                       row(lnx_w[l]), row(lnx_b[l]), x2, ya, wo[:RWKV_WIDTH], wo[RWKV_WIDTH:],
                       row(g_post[l]), batch, seq)
    return x2.reshape(batch, seq, d)
```
